```python
import math
import jax
import jax.numpy as jnp
from jax import lax
import numpy as np

D_MODEL = 1024
BATCH = 16
SEQ = 256
DEPTH = 1
DEC_BATCH = 8
DEC_SEQ = 4096
PAST_LEN = 512

GRID_W = 64
N_HEADS = 16
N_KV_HEADS = 4
HEAD_DIM = D_MODEL // N_HEADS
GQA_GROUP = N_HEADS // N_KV_HEADS
WINDOW = 128
BLOCK = 128
ROPE_BASE = 10000.0
D_HYENA = D_MODEL
HYENA_ORDER = 2
N_BANDS = 16
FILTER_EMB = 1 + 2 * N_BANDS
FILTER_WIDTH = 64
DECAY_MIN = math.log(100.0) / 1.5
DECAY_MAX = math.log(100.0) / 0.3
D_FF = 2816
N_MOD = 9
EPS = 1e-6
NEG_INF = -1e30

Q_COLS = N_HEADS * HEAD_DIM
KV_COLS = N_KV_HEADS * HEAD_DIM
HY_COLS = 3 * D_HYENA
GATE_COLS = 2 * D_MODEL
IN_COLS = Q_COLS + 2 * KV_COLS + HY_COLS + GATE_COLS
SPLIT_AT = (Q_COLS, Q_COLS + KV_COLS, Q_COLS + 2 * KV_COLS, Q_COLS + 2 * KV_COLS + HY_COLS)

kernel_name = 'hybrid_hyena_swa_dit_step'


def _rmsnorm(x, g):
    x32 = x.astype(jnp.float32)
    y = x32 * lax.rsqrt(jnp.mean(x32 * x32, axis=-1, keepdims=True) + EPS)
    return (y * g.astype(jnp.float32)).astype(x.dtype)


def _modulate(h, shift, scale):
    return h * (1.0 + scale) + shift


def _swiglu(h, w_gate_up, w_down):
    gate, up = jnp.split(h @ w_gate_up, 2, axis=-1)
    return (jax.nn.silu(gate) * up) @ w_down


def _axial_rope(x):
    n = x.shape[1]
    rows = n // GRID_W
    row = jnp.repeat(jnp.arange(rows), GRID_W)
    col = jnp.tile(jnp.arange(GRID_W), rows)
    n_freq = HEAD_DIM // 4
    inv = ROPE_BASE ** (-jnp.arange(n_freq, dtype=jnp.float32) / n_freq)

    def rot(xh, pos):
        ang = pos.astype(jnp.float32)[:, None] * inv[None, :]
        cos = jnp.cos(ang)[None, :, None, :]
        sin = jnp.sin(ang)[None, :, None, :]
        x1, x2 = jnp.split(xh.astype(jnp.float32), 2, axis=-1)
        return jnp.concatenate([x1 * cos - x2 * sin, x2 * cos + x1 * sin], axis=-1)

    xr, xc = jnp.split(x, 2, axis=-1)
    return jnp.concatenate([rot(xr, row), rot(xc, col)], axis=-1).astype(x.dtype)


def _attend(q, k, v, mask, sink):
    s = jnp.einsum('bqhgd,bkhd->bhgqk', q, k).astype(jnp.float32) * (1.0 / math.sqrt(HEAD_DIM))
    if mask is not None:
        s = jnp.where(mask, s, NEG_INF)
    sink_col = jnp.broadcast_to(sink.astype(jnp.float32).reshape(1, N_KV_HEADS, GQA_GROUP, 1, 1), s.shape[:-1] + (1,))
    p = jax.nn.softmax(jnp.concatenate([sink_col, s], axis=-1), axis=-1)[..., 1:]
    return jnp.einsum('bhgqk,bkhd->bqhgd', p.astype(v.dtype), v)


def _context_attention(q, k, v, sink):
    b, s_len = q.shape[:2]
    nb = s_len // BLOCK
    qb = q.reshape(b, nb, BLOCK, N_KV_HEADS, GQA_GROUP, HEAD_DIM)

    def block(i):
        qi = lax.dynamic_index_in_dim(qb, i, axis=1, keepdims=False)
        return _attend(qi, k, v, None, sink)

    o = lax.map(block, jnp.arange(nb))
    return jnp.moveaxis(o, 0, 1).reshape(b, s_len, Q_COLS)


def _latent_attention(q, k, v, ck, cv, sink):
    b, n = q.shape[:2]
    nb = n // BLOCK
    p_len = ck.shape[1]
    qb = q.reshape(b, nb, BLOCK, N_KV_HEADS, GQA_GROUP, HEAD_DIM)
    pad = ((0, 0), (BLOCK, BLOCK), (0, 0), (0, 0))
    kp = jnp.pad(k, pad)
    vp = jnp.pad(v, pad)
    ctx_mask = jnp.ones((BLOCK, p_len), dtype=bool)

    def block(i):
        qi = lax.dynamic_index_in_dim(qb, i, axis=1, keepdims=False)
        kw = lax.dynamic_slice_in_dim(kp, i * BLOCK, 3 * BLOCK, axis=1)
        vw = lax.dynamic_slice_in_dim(vp, i * BLOCK, 3 * BLOCK, axis=1)
        qpos = i * BLOCK + jnp.arange(BLOCK)
        kpos = (i - 1) * BLOCK + jnp.arange(3 * BLOCK)
        band = (jnp.abs(qpos[:, None] - kpos[None, :]) <= WINDOW) & (kpos[None, :] >= 0) & (kpos[None, :] < n)
        mask = jnp.concatenate([ctx_mask, band], axis=1)
        keys = jnp.concatenate([ck, kw], axis=1)
        vals = jnp.concatenate([cv, vw], axis=1)
        return _attend(qi, keys, vals, mask, sink)

    o = lax.map(block, jnp.arange(nb))
    return jnp.moveaxis(o, 0, 1).reshape(b, n, Q_COLS)


def _short_conv(u, w, bias):
    up = jnp.pad(u, ((0, 0), (1, 1), (0, 0)))
    return up[:, :-2] * w[0] + up[:, 1:-1] * w[1] + up[:, 2:] * w[2] + bias


def _hyena_filters_rfft(n, p):
    t = jnp.arange(n, dtype=jnp.float32) / max(n - 1, 1)
    bands = jnp.arange(1, N_BANDS + 1, dtype=jnp.float32)
    ang = 2.0 * math.pi * t[:, None] * bands[None, :]
    z = jnp.concatenate([t[:, None], jnp.cos(ang), jnp.sin(ang)], axis=-1)
    freq = p['filt_freq'].astype(jnp.float32)
    h = jnp.sin(freq * (z @ p['filt_w1'].astype(jnp.float32) + p['filt_b1'].astype(jnp.float32)))
    h = jnp.sin(freq * (h @ p['filt_w2'].astype(jnp.float32) + p['filt_b2'].astype(jnp.float32)))
    h = (h @ p['filt_w3'].astype(jnp.float32) + p['filt_b3'].astype(jnp.float32)).reshape(n, HYENA_ORDER, 2, D_HYENA)
    h = h * jnp.exp(-t[:, None, None, None] * jnp.abs(p['filt_decay'].astype(jnp.float32))[None])
    h = h / (jnp.sum(jnp.abs(h), axis=(0, 2), keepdims=True) + EPS)
    fwd = h[:, :, 0]
    bwd = h[:, :, 1]
    circ = jnp.concatenate([fwd, jnp.zeros_like(fwd[:1]), bwd[1:][::-1]], axis=0)
    return jnp.fft.rfft(circ, axis=0)


def _fftconv(z, kf, skip):
    n = z.shape[1]
    zf = jnp.fft.rfft(z, n=2 * n, axis=1)
    y = jnp.fft.irfft(zf * kf[None], n=2 * n, axis=1)[:, :n]
    return y + z * skip


def _hyena(u, p):
    n = u.shape[1]
    u32 = _short_conv(u, p['conv_w'], p['conv_b']).astype(jnp.float32)
    x1, x2, v = jnp.split(u32, 3, axis=-1)
    kf = _hyena_filters_rfft(n, p)
    skip = p['hyena_skip'].astype(jnp.float32)
    z = x1 * _fftconv(v, kf[:, 0], skip[0])
    z = x2 * _fftconv(z, kf[:, 1], skip[1])
    return z.astype(u.dtype)


def _layer(x, mod, p, ctx_kv):
    sh1, sc1, g1, sh2, sc2, g2, sh3, sc3, g3 = jnp.split(mod, N_MOD, axis=-1)
    h = _modulate(_rmsnorm(x, p['norm_ffn1']), sh1, sc1)
    x = x + 0.5 * g1 * _swiglu(h, p['ffn1_wi'], p['ffn1_wo'])

    h = _modulate(_rmsnorm(x, p['norm_mix']), sh2, sc2)
    b, n = h.shape[:2]
    q, k, v, hy_in, gates = jnp.split(h @ p['w_in'], SPLIT_AT, axis=-1)
    q = _rmsnorm(q.reshape(b, n, N_HEADS, HEAD_DIM), p['q_norm'])
    k = _rmsnorm(k.reshape(b, n, N_KV_HEADS, HEAD_DIM), p['k_norm'])
    v = v.reshape(b, n, N_KV_HEADS, HEAD_DIM)
    if ctx_kv is None:
        attn = _context_attention(q, k, v, p['attn_sink'])
        state = (k, v)
    else:
        attn = _latent_attention(_axial_rope(q), _axial_rope(k), v, ctx_kv[0], ctx_kv[1], p['attn_sink'])
        state = None
    hy = _hyena(hy_in, p)
    gate_a, gate_h = jnp.split(gates, 2, axis=-1)
    merged = jax.nn.sigmoid(gate_a) * (attn @ p['w_attn_branch']) + jax.nn.sigmoid(gate_h) * (hy @ p['w_hyena_branch'])
    x = x + g2 * (merged @ p['w_out'])

    h = _modulate(_rmsnorm(x, p['norm_ffn2']), sh3, sc3)
    x = x + 0.5 * g3 * _swiglu(h, p['ffn2_wi'], p['ffn2_wo'])
    return x, state


def _nrm(k, shape, scale=1.0):
    return scale * jax.random.normal(k, shape, jnp.float32)


def setup_inputs(seed: int = 0) -> dict:
    key = jax.random.key(seed)
    k = jax.random.split(key, 33)
    decay0 = jnp.broadcast_to(jnp.linspace(DECAY_MIN, DECAY_MAX, D_HYENA, dtype=jnp.float32), (DEPTH, HYENA_ORDER, 2, D_HYENA))
    return {
        'x_prompt': _nrm(k[0], (BATCH, SEQ, D_MODEL)),
        'x_sample': _nrm(k[1], (DEC_BATCH, DEC_SEQ, D_MODEL)),
        'cache_k': _nrm(k[2], (DEC_BATCH, DEPTH, PAST_LEN, N_KV_HEADS, HEAD_DIM)),
        'cache_v': _nrm(k[3], (DEC_BATCH, DEPTH, PAST_LEN, N_KV_HEADS, HEAD_DIM)),
        'c': _nrm(k[4], (DEC_BATCH, D_MODEL)),
        'c_ctx': _nrm(k[5], (D_MODEL,)),
        'w_mod': _nrm(k[6], (DEPTH, D_MODEL, N_MOD * D_MODEL), D_MODEL ** -0.5),
        'b_mod': _nrm(k[7], (DEPTH, N_MOD * D_MODEL), 0.02),
        'norm_ffn1': 1.0 + _nrm(k[8], (DEPTH, D_MODEL), 0.02),
        'ffn1_wi': _nrm(k[9], (DEPTH, D_MODEL, 2 * D_FF), D_MODEL ** -0.5),
        'ffn1_wo': _nrm(k[10], (DEPTH, D_FF, D_MODEL), D_FF ** -0.5),
        'norm_mix': 1.0 + _nrm(k[11], (DEPTH, D_MODEL), 0.02),
        'w_in': _nrm(k[12], (DEPTH, D_MODEL, IN_COLS), D_MODEL ** -0.5),
        'q_norm': 1.0 + _nrm(k[13], (DEPTH, HEAD_DIM), 0.02),
        'k_norm': 1.0 + _nrm(k[14], (DEPTH, HEAD_DIM), 0.02),
        'attn_sink': _nrm(k[15], (DEPTH, N_HEADS), 0.5),
        'conv_w': _nrm(k[16], (DEPTH, 3, HY_COLS), 3.0 ** -0.5),
        'conv_b': _nrm(k[17], (DEPTH, HY_COLS), 0.01),
        'filt_w1': _nrm(k[18], (DEPTH, FILTER_EMB, FILTER_WIDTH), FILTER_EMB ** -0.5),
        'filt_b1': _nrm(k[19], (DEPTH, FILTER_WIDTH), 0.02),
        'filt_w2': _nrm(k[20], (DEPTH, FILTER_WIDTH, FILTER_WIDTH), FILTER_WIDTH ** -0.5),
        'filt_b2': _nrm(k[21], (DEPTH, FILTER_WIDTH), 0.02),
        'filt_w3': _nrm(k[22], (DEPTH, FILTER_WIDTH, HYENA_ORDER * 2 * D_HYENA), FILTER_WIDTH ** -0.5),
        'filt_b3': _nrm(k[23], (DEPTH, HYENA_ORDER * 2 * D_HYENA), 0.02),
        'filt_freq': 1.0 + _nrm(k[24], (DEPTH, FILTER_WIDTH), 0.02),
        'filt_decay': decay0 + _nrm(k[25], (DEPTH, HYENA_ORDER, 2, D_HYENA), 0.1),
        'hyena_skip': _nrm(k[26], (DEPTH, HYENA_ORDER, D_HYENA), 0.1),
        'w_attn_branch': _nrm(k[27], (DEPTH, Q_COLS, D_MODEL), Q_COLS ** -0.5),
        'w_hyena_branch': _nrm(k[28], (DEPTH, D_HYENA, D_MODEL), D_HYENA ** -0.5),
        'w_out': _nrm(k[29], (DEPTH, D_MODEL, D_MODEL), D_MODEL ** -0.5),
        'norm_ffn2': 1.0 + _nrm(k[30], (DEPTH, D_MODEL), 0.02),
        'ffn2_wi': _nrm(k[31], (DEPTH, D_MODEL, 2 * D_FF), D_MODEL ** -0.5),
        'ffn2_wo': _nrm(k[32], (DEPTH, D_FF, D_MODEL), D_FF ** -0.5),
    }


def reference(x_prompt, x_sample, cache_k, cache_v, c, c_ctx, w_mod, b_mod, norm_ffn1, ffn1_wi, ffn1_wo,
              norm_mix, w_in, q_norm, k_norm, attn_sink, conv_w, conv_b, filt_w1, filt_b1, filt_w2, filt_b2,
              filt_w3, filt_b3, filt_freq, filt_decay, hyena_skip, w_attn_branch, w_hyena_branch, w_out,
              norm_ffn2, ffn2_wi, ffn2_wo):
    y_prompt = x_prompt
    y_sample = x_sample
    new_ks = []
    new_vs = []
    for l in range(DEPTH):
        p = {
            'norm_ffn1': norm_ffn1[l], 'ffn1_wi': ffn1_wi[l], 'ffn1_wo': ffn1_wo[l],
            'norm_mix': norm_mix[l], 'w_in': w_in[l], 'q_norm': q_norm[l], 'k_norm': k_norm[l],
            'attn_sink': attn_sink[l], 'conv_w': conv_w[l], 'conv_b': conv_b[l],
            'filt_w1': filt_w1[l], 'filt_b1': filt_b1[l], 'filt_w2': filt_w2[l], 'filt_b2': filt_b2[l],
            'filt_w3': filt_w3[l], 'filt_b3': filt_b3[l], 'filt_freq': filt_freq[l], 'filt_decay': filt_decay[l],
            'hyena_skip': hyena_skip[l], 'w_attn_branch': w_attn_branch[l], 'w_hyena_branch': w_hyena_branch[l],
            'w_out': w_out[l], 'norm_ffn2': norm_ffn2[l], 'ffn2_wi': ffn2_wi[l], 'ffn2_wo': ffn2_wo[l],
        }
        mod_ctx = jax.nn.silu(c_ctx) @ w_mod[l] + b_mod[l]
        mod_lat = (jax.nn.silu(c) @ w_mod[l] + b_mod[l])[:, None, :]
        y_prompt, (k_ctx, v_ctx) = _layer(y_prompt, mod_ctx, p, None)
        y_sample, _ = _layer(y_sample, mod_lat, p, (cache_k[:, l], cache_v[:, l]))
        new_ks.append(k_ctx)
        new_vs.append(v_ctx)
    new_k = jnp.stack(new_ks, axis=1)
    new_v = jnp.stack(new_vs, axis=1)
    return (y_prompt, y_sample, new_k, new_v)
```

```python
import functools
import math

import numpy as np
import jax
import jax.numpy as jnp
from jax import lax
from jax.experimental import pallas as pl
from jax.experimental.pallas import tpu as pltpu

F32 = jnp.float32
BF16 = jnp.bfloat16
HIGHEST = lax.Precision.HIGHEST

D_MODEL = 1024
N_HEADS = 16
N_KV_HEADS = 4
HEAD_DIM = 64
GRID_W = 64
WINDOW = 128
ROPE_BASE = 10000.0
D_HYENA = 1024
N_BANDS = 16
FILTER_EMB = 1 + 2 * N_BANDS
FILTER_WIDTH = 64
D_FF = 2816
N_MOD = 9
EPS = 1e-6
NEG_INF = -1e30

OFF_Q = 0
OFF_GA = 1024
OFF_GH = 2048
OFF_X1 = 3072
OFF_X2 = 4096
OFF_HV = 5120
OFF_K = 6144
OFF_V = 6400
IN_COLS = 6656
SUB = 256

LANES = 128
VMEM_LIMIT = 56 * 1024 * 1024


def _cparams(sem):
    return pltpu.CompilerParams(dimension_semantics=sem, vmem_limit_bytes=VMEM_LIMIT)


def _mod_kernel(c_ref, w_ref, b_ref, o_ref):
    c = c_ref[...]
    s = c * jax.nn.sigmoid(c)
    o_ref[...] = jnp.dot(s, w_ref[...], precision=HIGHEST, preferred_element_type=F32) + b_ref[...]


def _modulation(c_rows, w_mod, b_mod):
    rows = c_rows.shape[0]
    n_out = w_mod.shape[1]
    tn = 1024
    return pl.pallas_call(
        _mod_kernel,
        grid=(n_out // tn,),
        in_specs=[
            pl.BlockSpec((rows, D_MODEL), lambda j: (0, 0)),
            pl.BlockSpec((D_MODEL, tn), lambda j: (0, j)),
            pl.BlockSpec((1, tn), lambda j: (0, j)),
        ],
        out_specs=pl.BlockSpec((rows, tn), lambda j: (0, j)),
        out_shape=jax.ShapeDtypeStruct((rows, n_out), F32),
        compiler_params=_cparams(("arbitrary",)),
        name="modulation",
    )(c_rows, w_mod, b_mod.reshape(1, n_out))


def _norm_modulate(x, g, shift, scale):
    ms = jnp.mean(x * x, axis=-1, keepdims=True)
    y = x * lax.rsqrt(ms + EPS) * g
    return y * (1.0 + scale) + shift


def _ffn_kernel(x_ref, mod_ref, g_ref, wg_ref, wu_ref, wo_ref, o_ref, h_scr, acc_scr, *, mod_base):
    j = pl.program_id(1)

    @pl.when(j == 0)
    def _():
        shift = mod_ref[0, mod_base:mod_base + 1, :]
        scale = mod_ref[0, mod_base + 1:mod_base + 2, :]
        h_scr[...] = _norm_modulate(x_ref[...], g_ref[...], shift, scale).astype(BF16)
        acc_scr[...] = jnp.zeros_like(acc_scr)

    h = h_scr[...]
    gate = jnp.dot(h, wg_ref[...], preferred_element_type=F32)
    up = jnp.dot(h, wu_ref[...], preferred_element_type=F32)
    a = (gate * jax.nn.sigmoid(gate) * up).astype(BF16)
    acc_scr[...] += jnp.dot(a, wo_ref[...], preferred_element_type=F32)

    @pl.when(j == pl.num_programs(1) - 1)
    def _():
        gmod = mod_ref[0, mod_base + 2:mod_base + 3, :]
        o_ref[...] = x_ref[...] + (0.5 * gmod) * acc_scr[...]


def _ffn(x, mod3, tiles_per_mod, g, wi, wo, mod_base, tm, tf):
    t = x.shape[0]
    nj = D_FF // tf
    mod_map = (lambda i, j: (i // tiles_per_mod, 0, 0)) if tiles_per_mod else (lambda i, j: (0, 0, 0))
    return pl.pallas_call(
        functools.partial(_ffn_kernel, mod_base=mod_base),
        grid=(t // tm, nj),
        in_specs=[
            pl.BlockSpec((tm, D_MODEL), lambda i, j: (i, 0)),
            pl.BlockSpec((1, N_MOD, D_MODEL), mod_map),
            pl.BlockSpec((1, D_MODEL), lambda i, j: (0, 0)),
            pl.BlockSpec((D_MODEL, tf), lambda i, j: (0, j)),
            pl.BlockSpec((D_MODEL, tf), lambda i, j: (0, j + nj)),
            pl.BlockSpec((tf, D_MODEL), lambda i, j: (j, 0)),
        ],
        out_specs=pl.BlockSpec((tm, D_MODEL), lambda i, j: (i, 0)),
        out_shape=jax.ShapeDtypeStruct((t, D_MODEL), F32),
        scratch_shapes=[pltpu.VMEM((tm, D_MODEL), BF16), pltpu.VMEM((tm, D_MODEL), F32)],
        compiler_params=_cparams(("parallel", "arbitrary")),
        name="ffn",
    )(x, mod3, g.reshape(1, D_MODEL), wi, wi, wo)


def _rope_partner(y):
    lane = lax.broadcasted_iota(jnp.int32, (1, LANES), 1)
    first = (lane % 32) < 16
    parts = []
    for c in range(y.shape[1] // LANES):
        yc = y[:, c * LANES:(c + 1) * LANES]
        fwd = pltpu.roll(yc, LANES - 16, axis=1)
        bwd = pltpu.roll(yc, 16, axis=1)
        parts.append(jnp.where(first, fwd, bwd))
    return jnp.concatenate(parts, axis=1)


def _inproj_kernel(*refs, rope, sub_per_tile):
    if rope:
        x_ref, mod_ref, g_ref, w_ref, qg_ref, kg_ref, bd_ref, cos_ref, sin_ref, o_ref, h_scr = refs
    else:
        x_ref, mod_ref, g_ref, w_ref, qg_ref, kg_ref, bd_ref, o_ref, h_scr = refs
    j = pl.program_id(1)

    @pl.when(j == 0)
    def _():
        shift = mod_ref[0, 3:4, :]
        scale = mod_ref[0, 4:5, :]
        h_scr[...] = _norm_modulate(x_ref[...], g_ref[...], shift, scale).astype(BF16)

    acc = jnp.dot(h_scr[...], w_ref[...], preferred_element_type=F32)

    for c in range(sub_per_tile):
        gs = j * sub_per_tile + c
        cols = slice(c * SUB, (c + 1) * SUB)
        t = acc[:, cols]
        is_q = gs < (OFF_GA // SUB)
        is_k = gs == (OFF_K // SUB)
        is_gate = (gs >= (OFF_GA // SUB)) & (gs < (OFF_X1 // SUB))

        @pl.when(is_q | is_k)
        def _():
            sq = t * t
            hi = sq.astype(BF16)
            lo = (sq - hi.astype(F32)).astype(BF16)
            bd = bd_ref[...]
            ss = (jnp.dot(hi, bd, preferred_element_type=F32)
                  + jnp.dot(lo, bd, preferred_element_type=F32))
            gain = jnp.where(is_q, qg_ref[...] * 0.125, kg_ref[...])
            y = t * lax.rsqrt(ss * (1.0 / HEAD_DIM) + EPS) * gain
            if rope:
                y = y * cos_ref[...] + _rope_partner(y) * sin_ref[...]
            o_ref[:, cols] = y

        @pl.when(is_gate)
        def _():
            o_ref[:, cols] = jax.nn.sigmoid(t)

        @pl.when(jnp.logical_not(is_q | is_k | is_gate))
        def _():
            o_ref[:, cols] = t


def _inproj(x, mod3, tiles_per_mod, g, w, qg, kg, bd, rope_tabs, tm, tn):
    t = x.shape[0]
    rope = rope_tabs is not None
    mod_map = (lambda i, j: (i // tiles_per_mod, 0, 0)) if tiles_per_mod else (lambda i, j: (0, 0, 0))
    in_specs = [
        pl.BlockSpec((tm, D_MODEL), lambda i, j: (i, 0)),
        pl.BlockSpec((1, N_MOD, D_MODEL), mod_map),
        pl.BlockSpec((1, D_MODEL), lambda i, j: (0, 0)),
        pl.BlockSpec((D_MODEL, tn), lambda i, j: (0, j)),
        pl.BlockSpec((1, SUB), lambda i, j: (0, 0)),
        pl.BlockSpec((1, SUB), lambda i, j: (0, 0)),
        pl.BlockSpec((SUB, SUB), lambda i, j: (0, 0)),
    ]
    args = [x, mod3, g.reshape(1, D_MODEL), w, qg, kg, bd]
    if rope:
        seq_tiles = rope_tabs[0].shape[0] // tm
        for tab in rope_tabs:
            in_specs.append(pl.BlockSpec((tm, SUB), lambda i, j: (i % seq_tiles, 0)))
            args.append(tab)
    return pl.pallas_call(
        functools.partial(_inproj_kernel, rope=rope, sub_per_tile=tn // SUB),
        grid=(t // tm, IN_COLS // tn),
        in_specs=in_specs,
        out_specs=pl.BlockSpec((tm, tn), lambda i, j: (i, j)),
        out_shape=jax.ShapeDtypeStruct((t, IN_COLS), F32),
        scratch_shapes=[pltpu.VMEM((tm, D_MODEL), BF16)],
        compiler_params=_cparams(("parallel", "arbitrary")),
        name="inproj",
    )(*args)


def _place_halves(x, own_half):
    lane = lax.broadcasted_iota(jnp.int32, (1, LANES), 1)
    in_half = [lane < 64, lane >= 64]
    swapped = pltpu.roll(x, 64, axis=1)
    out = [None, None]
    out[own_half] = jnp.where(in_half[own_half], x, 0.0).astype(BF16)
    out[1 - own_half] = jnp.where(in_half[1 - own_half], swapped, 0.0).astype(BF16)
    return out


def _attn_kernel(*refs, window, tq, n_qblocks):
    if window:
        (sink_ref, q_ref, kc_ref, vc_ref, kp_ref, kcur_ref, kn_ref,
         vp_ref, vcur_ref, vn_ref, bias_ref, o_ref) = refs
    else:
        sink_ref, q_ref, kc_ref, vc_ref, o_ref = refs
    kc = kc_ref[...]
    vc = vc_ref[...]
    if window:
        i = pl.program_id(1)
        kw = jnp.concatenate([kp_ref[...], kcur_ref[...], kn_ref[...]], axis=0)
        vw = jnp.concatenate([vp_ref[...], vcur_ref[...], vn_ref[...]], axis=0)
        col = lax.broadcasted_iota(jnp.int32, (1, 3 * WINDOW), 1)
        edge = (jnp.where((col < WINDOW) & (i == 0), NEG_INF, 0.0)
                + jnp.where((col >= 2 * WINDOW) & (i == n_qblocks - 1), NEG_INF, 0.0))
        bias = bias_ref[...] + edge
    row = lax.broadcasted_iota(jnp.int32, (2 * tq, 1), 0)
    dn = (((1,), (1,)), ((), ()))
    for g in range(N_KV_HEADS):
        colblk = slice((g // 2) * LANES, (g // 2 + 1) * LANES)
        own = g % 2
        kc_pl = _place_halves(kc[:, colblk], own)
        vc_pl = _place_halves(vc[:, colblk], own)
        if window:
            kw_pl = _place_halves(kw[:, colblk], own)
            vw_pl = _place_halves(vw[:, colblk], own)
        q0 = q_ref[:, g * 2 * LANES:g * 2 * LANES + LANES]
        q1 = q_ref[:, g * 2 * LANES + LANES:(g + 1) * 2 * LANES]
        qg = jnp.concatenate([q0, q1], axis=0).astype(BF16)
        acc = jnp.zeros((2 * tq, LANES), F32)
        for a in range(2):
            sink = jnp.where(row < tq, sink_ref[4 * g + a], sink_ref[4 * g + 2 + a])
            s_c = lax.dot_general(qg, kc_pl[a], dn, preferred_element_type=F32)
            m = jnp.maximum(jnp.max(s_c, axis=-1, keepdims=True), sink)
            if window:
                s_w = lax.dot_general(qg, kw_pl[a], dn, preferred_element_type=F32) + bias
                m = jnp.maximum(m, jnp.max(s_w, axis=-1, keepdims=True))
            p_c = jnp.exp(s_c - m)
            l = jnp.sum(p_c, axis=-1, keepdims=True) + jnp.exp(sink - m)
            o = jnp.dot(p_c.astype(BF16), vc_pl[a], preferred_element_type=F32)
            if window:
                p_w = jnp.exp(s_w - m)
                l = l + jnp.sum(p_w, axis=-1, keepdims=True)
                o = o + jnp.dot(p_w.astype(BF16), vw_pl[a], preferred_element_type=F32)
            acc = acc + o * (1.0 / l)
        o_ref[:, g * 2 * LANES:g * 2 * LANES + LANES] = acc[:tq]
        o_ref[:, g * 2 * LANES + LANES:(g + 1) * 2 * LANES] = acc[tq:]


def _window_bias(tq):
    a = np.arange(WINDOW)[:, None]
    b = np.arange(WINDOW)[None, :]
    prev = np.where(b >= a, 0.0, NEG_INF)
    cur = np.zeros((WINDOW, WINDOW))
    nxt = np.where(b <= a, 0.0, NEG_INF)
    one = np.concatenate([prev, cur, nxt], axis=1).astype(np.float32)
    return jnp.asarray(np.concatenate([one, one], axis=0))


def _attention_ctx(proj, sink, n_batch, seq):
    t = proj.shape[0]
    tq = seq
    kblk = OFF_K // SUB
    vblk = OFF_V // SUB
    return pl.pallas_call(
        functools.partial(_attn_kernel, window=False, tq=tq, n_qblocks=1),
        grid=(n_batch, 1),
        in_specs=[
            pl.BlockSpec(memory_space=pltpu.SMEM),
            pl.BlockSpec((tq, D_MODEL), lambda b, i: (b, 0)),
            pl.BlockSpec((seq, SUB), lambda b, i: (b, kblk)),
            pl.BlockSpec((seq, SUB), lambda b, i: (b, vblk)),
        ],
        out_specs=pl.BlockSpec((tq, D_MODEL), lambda b, i: (b, 0)),
        out_shape=jax.ShapeDtypeStruct((t, D_MODEL), F32),
        compiler_params=_cparams(("parallel", "arbitrary")),
        name="attn_ctx",
    )(sink, proj, proj, proj)


def _attention_lat(proj, ck, cv, sink, n_batch, seq, past):
    t = proj.shape[0]
    tq = WINDOW
    nqb = seq // tq
    kblk = OFF_K // SUB
    vblk = OFF_V // SUB

    def kv_spec(colblk, delta):
        def imap(b, i):
            return (b * nqb + jnp.clip(i + delta, 0, nqb - 1), colblk)
        return pl.BlockSpec((tq, SUB), imap)

    return pl.pallas_call(
        functools.partial(_attn_kernel, window=True, tq=tq, n_qblocks=nqb),
        grid=(n_batch, nqb),
        in_specs=[
            pl.BlockSpec(memory_space=pltpu.SMEM),
            pl.BlockSpec((tq, D_MODEL), lambda b, i: (b * nqb + i, 0)),
            pl.BlockSpec((past, SUB), lambda b, i: (b, 0)),
            pl.BlockSpec((past, SUB), lambda b, i: (b, 0)),
            kv_spec(kblk, -1), kv_spec(kblk, 0), kv_spec(kblk, 1),
            kv_spec(vblk, -1), kv_spec(vblk, 0), kv_spec(vblk, 1),
            pl.BlockSpec((2 * tq, 3 * WINDOW), lambda b, i: (0, 0)),
        ],
        out_specs=pl.BlockSpec((tq, D_MODEL), lambda b, i: (b * nqb + i, 0)),
        out_shape=jax.ShapeDtypeStruct((t, D_MODEL), F32),
        compiler_params=_cparams(("parallel", "arbitrary")),
        name="attn_lat",
    )(sink, proj, ck, cv, proj, proj, proj, proj, proj, proj, _window_bias(tq))


def _fgen_kernel(z_ref, w1_ref, b1_ref, w2_ref, b2_ref, fr_ref,
                 w3b_ref, b3b_ref, dcb_ref, w3f_ref, b3f_ref, dcf_ref, o_ref, a2_scr, *, n):
    @pl.when((pl.program_id(0) == 0) & (pl.program_id(1) == 0))
    def _():
        fr = fr_ref[...]
        a1 = jnp.sin(fr * (jnp.dot(z_ref[...], w1_ref[...], precision=HIGHEST,
                                   preferred_element_type=F32) + b1_ref[...]))
        a2_scr[...] = jnp.sin(fr * (jnp.dot(a1, w2_ref[...], precision=HIGHEST,
                                            preferred_element_type=F32) + b2_ref[...]))

    tc = o_ref.shape[1]
    tb = jnp.broadcast_to(z_ref[0:n, 0:1], (n, tc))
    tf = jnp.broadcast_to(z_ref[n:2 * n, 0:1], (n, tc))
    hb = (jnp.dot(a2_scr[0:n, :], w3b_ref[...], precision=HIGHEST, preferred_element_type=F32)
          + b3b_ref[...]) * jnp.exp(-tb * jnp.abs(dcb_ref[...]))
    hf = (jnp.dot(a2_scr[n:2 * n, :], w3f_ref[...], precision=HIGHEST, preferred_element_type=F32)
          + b3f_ref[...]) * jnp.exp(-tf * jnp.abs(dcf_ref[...]))
    tot = (jnp.sum(jnp.abs(hb), axis=0, keepdims=True)
           + jnp.sum(jnp.abs(hf), axis=0, keepdims=True))
    inv = 1.0 / (tot + EPS)
    rowid = lax.broadcasted_iota(jnp.int32, (n, tc), 0)
    o_ref[0:n, :] = jnp.where(rowid == 0, 0.0, hb * inv)
    o_ref[n:2 * n, :] = hf * inv


def _filter_gen(n, filt_w1, filt_b1, filt_w2, filt_b2, filt_w3, filt_b3, filt_freq, filt_decay, tc):
    t = jnp.arange(n, dtype=F32) / max(n - 1, 1)
    bands = jnp.arange(1, N_BANDS + 1, dtype=F32)
    ang = 2.0 * math.pi * t[:, None] * bands[None, :]
    z = jnp.concatenate([t[:, None], jnp.cos(ang), jnp.sin(ang)], axis=-1)
    zb = jnp.concatenate([z[0:1], z[1:][::-1]], axis=0)
    zfull = jnp.pad(jnp.concatenate([zb, z], axis=0), ((0, 0), (0, LANES - FILTER_EMB)))
    padw = LANES - FILTER_WIDTH
    w1 = jnp.pad(filt_w1.astype(F32), ((0, LANES - FILTER_EMB), (0, padw)))
    b1 = jnp.pad(filt_b1.astype(F32), (0, padw)).reshape(1, LANES)
    w2 = jnp.pad(filt_w2.astype(F32), ((0, padw), (0, padw)))
    b2 = jnp.pad(filt_b2.astype(F32), (0, padw)).reshape(1, LANES)
    fr = jnp.pad(filt_freq.astype(F32), (0, padw)).reshape(1, LANES)
    w3 = jnp.pad(filt_w3.astype(F32), ((0, padw), (0, 0)))
    ncol = w3.shape[1]
    b3 = filt_b3.astype(F32).reshape(1, ncol)
    dc = filt_decay.astype(F32).reshape(1, ncol)
    ct = D_HYENA // tc
    full = lambda shape: pl.BlockSpec(shape, lambda o, c: (0, 0))
    bwd = lambda rows: pl.BlockSpec((rows, tc), lambda o, c: (0, (2 * o + 1) * ct + c))
    fwd = lambda rows: pl.BlockSpec((rows, tc), lambda o, c: (0, (2 * o) * ct + c))
    return pl.pallas_call(
        functools.partial(_fgen_kernel, n=n),
        grid=(2, ct),
        in_specs=[
            full((2 * n, LANES)), full((LANES, LANES)), full((1, LANES)),
            full((LANES, LANES)), full((1, LANES)), full((1, LANES)),
            bwd(LANES), bwd(1), bwd(1), fwd(LANES), fwd(1), fwd(1),
        ],
        out_specs=pl.BlockSpec((2 * n, tc), lambda o, c: (0, o * ct + c)),
        out_shape=jax.ShapeDtypeStruct((2 * n, 2 * D_HYENA), F32),
        scratch_shapes=[pltpu.VMEM((2 * n, LANES), F32)],
        compiler_params=_cparams(("arbitrary", "arbitrary")),
        name="hyena_filter_gen",
    )(zfull, w1, b1, w2, b2, fr, w3, b3, dc, w3, b3, dc)


def _dft_mats(blk):
    f = np.arange(blk, dtype=np.int64)[:, None]
    s = np.arange(blk, dtype=np.int64)[None, :]
    theta = (np.pi / (2 * blk)) * (((2 * f + 1) * s) % (4 * blk)).astype(np.float64)
    fwd = np.concatenate([np.cos(theta), -np.sin(theta)], axis=0)
    inv = np.concatenate([np.cos(theta).T, -np.sin(theta).T], axis=1) / blk
    return (jnp.asarray(fwd, dtype=F32).astype(BF16), jnp.asarray(inv, dtype=F32).astype(BF16))


def _ftf_kernel(k_ref, f_ref, o_ref, *, nblk2, blk):
    tc = o_ref.shape[2]
    fmat = f_ref[...]
    fidx = lax.broadcasted_iota(jnp.int32, (blk, tc), 0)
    sgn = jnp.where(fidx % 2 == 0, 1.0, -1.0)
    prev = None
    for e in range(nblk2):
        p = jnp.dot(fmat, k_ref[e * blk:(e + 1) * blk, :].astype(BF16), preferred_element_type=F32)
        if e >= 1:
            o_ref[e - 1, 0:blk, :] = p[0:blk] - sgn * prev[blk:2 * blk]
            o_ref[e - 1, blk:2 * blk, :] = p[blk:2 * blk] + sgn * prev[0:blk]
        prev = p


def _filter_transform(kfull, fmat, n, blk, tc):
    nblk2 = 2 * n // blk
    nd = nblk2 - 1
    ct = D_HYENA // tc
    return pl.pallas_call(
        functools.partial(_ftf_kernel, nblk2=nblk2, blk=blk),
        grid=(2, ct),
        in_specs=[
            pl.BlockSpec((2 * n, tc), lambda o, c: (0, o * ct + c)),
            pl.BlockSpec((2 * blk, blk), lambda o, c: (0, 0)),
        ],
        out_specs=pl.BlockSpec((nd, 2 * blk, tc), lambda o, c: (o, 0, c)),
        out_shape=jax.ShapeDtypeStruct((2 * nd, 2 * blk, D_HYENA), F32),
        compiler_params=_cparams(("parallel", "parallel")),
        name="hyena_filter_transform",
    )(kfull, fmat)


CHUNK = 64


def _short_conv(u, w_ref, b_ref):
    n, tc = u.shape
    rowid = lax.broadcasted_iota(jnp.int32, (n, tc), 0)
    prev = jnp.where(rowid == 0, 0.0, pltpu.roll(u, 1, axis=0))
    nxt = jnp.where(rowid == n - 1, 0.0, pltpu.roll(u, n - 1, axis=0))
    return prev * w_ref[0:1, :] + u * w_ref[1:2, :] + nxt * w_ref[2:3, :] + b_ref[...]


def _conv_kernel(zin_ref, gin_ref, wz_ref, bz_ref, wg_ref, bg_ref, g_ref, skip_ref, f_ref, fi_ref,
                 o_ref, z_scr, gate_scr, zf_scr, yf_scr, *, n, blk, conv_z):
    nblk = n // blk
    z = zin_ref[...]
    if conv_z:
        z = _short_conv(z, wz_ref, bz_ref)
    z_scr[...] = z
    gate_scr[...] = _short_conv(gin_ref[...], wg_ref, bg_ref)
    fmat = f_ref[...]
    for j in range(nblk):
        zf_scr[j] = jnp.dot(fmat, z_scr[j * blk:(j + 1) * blk, :].astype(BF16),
                            preferred_element_type=F32)
    skip = skip_ref[...]

    def out_block(i, carry):
        def chunk(r, c2):
            r0 = pl.multiple_of(r * CHUNK, CHUNK)
            re_rows = pl.ds(r0, CHUNK)
            im_rows = pl.ds(blk + r0, CHUNK)
            acc_re = jnp.zeros((CHUNK, z_scr.shape[1]), F32)
            acc_im = jnp.zeros((CHUNK, z_scr.shape[1]), F32)
            for j in range(nblk):
                d = i - j + (nblk - 1)
                g_re = g_ref[d, re_rows, :]
                g_im = g_ref[d, im_rows, :]
                z_re = zf_scr[j, re_rows, :]
                z_im = zf_scr[j, im_rows, :]
                acc_re = acc_re + (g_re * z_re - g_im * z_im)
                acc_im = acc_im + (g_re * z_im + g_im * z_re)
            yf_scr[re_rows, :] = acc_re.astype(BF16)
            yf_scr[im_rows, :] = acc_im.astype(BF16)
            return c2

        lax.fori_loop(0, blk // CHUNK, chunk, 0)
        y = jnp.dot(fi_ref[...], yf_scr[...], preferred_element_type=F32)
        rows = pl.ds(pl.multiple_of(i * blk, blk), blk)
        o_ref[rows, :] = gate_scr[rows, :] * (y + skip * z_scr[rows, :])
        return carry

    lax.fori_loop(0, nblk, out_block, 0)


def _long_conv(zsrc, zcol, gsrc, gcol, conv_w, conv_b, gspec, order, skip, fmat, fimat,
               n_batch, n, blk, tc, conv_z):
    nblk = n // blk
    nd = 2 * nblk - 1
    ct = D_HYENA // tc
    hy0 = OFF_X1 // tc
    wz = (zcol - hy0) if conv_z else 0
    wg = gcol - hy0
    return pl.pallas_call(
        functools.partial(_conv_kernel, n=n, blk=blk, conv_z=conv_z),
        grid=(ct, n_batch),
        in_specs=[
            pl.BlockSpec((n, tc), lambda c, b: (b, zcol + c)),
            pl.BlockSpec((n, tc), lambda c, b: (b, gcol + c)),
            pl.BlockSpec((3, tc), lambda c, b: (0, wz + c)),
            pl.BlockSpec((1, tc), lambda c, b: (0, wz + c)),
            pl.BlockSpec((3, tc), lambda c, b: (0, wg + c)),
            pl.BlockSpec((1, tc), lambda c, b: (0, wg + c)),
            pl.BlockSpec((nd, 2 * blk, tc), lambda c, b: (order, 0, c)),
            pl.BlockSpec((None, 1, tc), lambda c, b: (order, 0, c)),
            pl.BlockSpec((2 * blk, blk), lambda c, b: (0, 0)),
            pl.BlockSpec((blk, 2 * blk), lambda c, b: (0, 0)),
        ],
        out_specs=pl.BlockSpec((n, tc), lambda c, b: (b, c)),
        out_shape=jax.ShapeDtypeStruct((n_batch * n, D_HYENA), F32),
        scratch_shapes=[
            pltpu.VMEM((n, tc), F32), pltpu.VMEM((n, tc), F32),
            pltpu.VMEM((nblk, 2 * blk, tc), F32), pltpu.VMEM((2 * blk, tc), BF16),
        ],
        compiler_params=_cparams(("parallel", "arbitrary")),
        name="hyena_conv",
    )(zsrc, gsrc, conv_w, conv_b, conv_w, conv_b, gspec, skip, fmat, fimat)


def _hyena(proj, n_batch, n, blk, tc, conv_w, conv_b, skip, filt):
    fmat, fimat = _dft_mats(blk)
    kfull = _filter_gen(n, *filt, tc=128)
    gspec = _filter_transform(kfull, fmat, n, blk, tc=128)
    cw = conv_w.astype(F32)
    cb = conv_b.astype(F32).reshape(1, -1)
    sk = skip.astype(F32).reshape(2, 1, D_HYENA)
    z2 = _long_conv(proj, OFF_HV // tc, proj, OFF_X1 // tc, cw, cb, gspec, 0, sk, fmat, fimat,
                    n_batch, n, blk, tc, conv_z=True)
    return _long_conv(z2, 0, proj, OFF_X2 // tc, cw, cb, gspec, 1, sk, fmat, fimat,
                      n_batch, n, blk, tc, conv_z=False)


def _outproj_kernel(x_ref, attn_ref, hy_ref, ga_ref, gh_ref, mod_ref, wa_ref, wh_ref, wo_ref, o_ref):
    a = jnp.dot(attn_ref[...].astype(BF16), wa_ref[...], preferred_element_type=F32)
    h = jnp.dot(hy_ref[...].astype(BF16), wh_ref[...], preferred_element_type=F32)
    merged = ga_ref[...] * a + gh_ref[...] * h
    out = jnp.dot(merged.astype(BF16), wo_ref[...], preferred_element_type=F32)
    o_ref[...] = x_ref[...] + mod_ref[0, 5:6, :] * out


def _outproj(x, attn, hy, proj, mod3, tiles_per_mod, wa, wh, wo, tm):
    t = x.shape[0]
    mod_map = (lambda i: (i // tiles_per_mod, 0, 0)) if tiles_per_mod else (lambda i: (0, 0, 0))
    row = lambda i: (i, 0)
    wspec = pl.BlockSpec((D_MODEL, D_MODEL), lambda i: (0, 0))
    return pl.pallas_call(
        _outproj_kernel,
        grid=(t // tm,),
        in_specs=[
            pl.BlockSpec((tm, D_MODEL), row),
            pl.BlockSpec((tm, D_MODEL), row),
            pl.BlockSpec((tm, D_MODEL), row),
            pl.BlockSpec((tm, D_MODEL), lambda i: (i, OFF_GA // D_MODEL)),
            pl.BlockSpec((tm, D_MODEL), lambda i: (i, OFF_GH // D_MODEL)),
            pl.BlockSpec((1, N_MOD, D_MODEL), mod_map),
            wspec, wspec, wspec,
        ],
        out_specs=pl.BlockSpec((tm, D_MODEL), row),
        out_shape=jax.ShapeDtypeStruct((t, D_MODEL), F32),
        compiler_params=_cparams(("parallel",)),
        name="mixer_out",
    )(x, attn, hy, proj, proj, mod3, wa, wh, wo)


def _rope_tables(n):
    pos = jnp.arange(n)
    row = (pos // GRID_W).astype(F32)
    col = (pos % GRID_W).astype(F32)
    n_freq = HEAD_DIM // 4
    inv = ROPE_BASE ** (-jnp.arange(n_freq, dtype=F32) / n_freq)
    ar = row[:, None] * inv[None, :]
    ac = col[:, None] * inv[None, :]
    cos = jnp.concatenate([jnp.cos(ar), jnp.cos(ar), jnp.cos(ac), jnp.cos(ac)], axis=-1)
    sin = jnp.concatenate([-jnp.sin(ar), jnp.sin(ar), -jnp.sin(ac), jnp.sin(ac)], axis=-1)
    reps = SUB // HEAD_DIM
    return jnp.tile(cos, (1, reps)), jnp.tile(sin, (1, reps))


def _layer(x, mod3, per_batch_mod, n_batch, seq, w, ctx_kv, hy_blk, hy_tc):
    t = x.shape[0]
    tm = 1024
    tiles_per_mod = (seq // tm) if per_batch_mod else 0
    x = _ffn(x, mod3, tiles_per_mod, w["norm_ffn1"], w["ffn1_wi"], w["ffn1_wo"], 0, tm, 256)
    rope_tabs = _rope_tables(seq) if ctx_kv is not None else None
    proj = _inproj(x, mod3, tiles_per_mod, w["norm_mix"], w["w_in"], w["qg"], w["kg"], w["bd"],
                   rope_tabs, tm, 512)
    if ctx_kv is None:
        attn = _attention_ctx(proj, w["sink"], n_batch, seq)
    else:
        attn = _attention_lat(proj, ctx_kv[0], ctx_kv[1], w["sink"], n_batch, seq,
                              ctx_kv[0].shape[0] // n_batch)
    hy = _hyena(proj, n_batch, seq, hy_blk, hy_tc, w["conv_w"], w["conv_b"], w["hyena_skip"], w["filt"])
    tmo = 512
    x = _outproj(x, attn, hy, proj, mod3, (seq // tmo) if per_batch_mod else 0,
                 w["wa"], w["wh"], w["wo"], tmo)
    x = _ffn(x, mod3, tiles_per_mod, w["norm_ffn2"], w["ffn2_wi"], w["ffn2_wo"], 6, tm, 256)
    return x, proj


def _permute_w_in(w_in):
    q = w_in[:, 0:1024]
    k = w_in[:, 1024:1280]
    v = w_in[:, 1280:1536]
    hy = w_in[:, 1536:4608]
    gates = w_in[:, 4608:6656]
    return jnp.concatenate([q, gates, hy, k, v], axis=1)


def kernel(x_prompt, x_sample, cache_k, cache_v, c, c_ctx, w_mod, b_mod, norm_ffn1, ffn1_wi, ffn1_wo, norm_mix, w_in, q_norm, k_norm, attn_sink, conv_w, conv_b, filt_w1, filt_b1, filt_w2, filt_b2, filt_w3, filt_b3, filt_freq, filt_decay, hyena_skip, w_attn_branch, w_hyena_branch, w_out, norm_ffn2, ffn2_wi, ffn2_wo):
    batch, seq, _ = x_prompt.shape
    dec_batch, dec_seq, _ = x_sample.shape
    depth = w_mod.shape[0]
    past = cache_k.shape[2]

    yp = x_prompt.reshape(batch * seq, D_MODEL)
    ys = x_sample.reshape(dec_batch * dec_seq, D_MODEL)
    bd = jnp.asarray(np.kron(np.eye(SUB // HEAD_DIM), np.ones((HEAD_DIM, HEAD_DIM))), dtype=F32).astype(BF16)
    new_ks, new_vs = [], []
    for l in range(depth):
        c_rows = jnp.zeros((16, D_MODEL), F32).at[0:dec_batch].set(c).at[dec_batch].set(c_ctx)
        mod = _modulation(c_rows, w_mod[l], b_mod[l]).reshape(16, N_MOD, D_MODEL)
        mod_lat = mod[0:dec_batch]
        mod_ctx = mod[dec_batch:dec_batch + 1]
        w = {
            "norm_ffn1": norm_ffn1[l], "ffn1_wi": ffn1_wi[l].astype(BF16), "ffn1_wo": ffn1_wo[l].astype(BF16),
            "norm_mix": norm_mix[l], "w_in": _permute_w_in(w_in[l]).astype(BF16),
            "qg": jnp.tile(q_norm[l], SUB // HEAD_DIM).reshape(1, SUB),
            "kg": jnp.tile(k_norm[l], SUB // HEAD_DIM).reshape(1, SUB),
            "bd": bd, "sink": attn_sink[l],
            "conv_w": conv_w[l], "conv_b": conv_b[l], "hyena_skip": hyena_skip[l],
            "filt": (filt_w1[l], filt_b1[l], filt_w2[l], filt_b2[l], filt_w3[l], filt_b3[l],
                     filt_freq[l], filt_decay[l]),
            "wa": w_attn_branch[l].astype(BF16), "wh": w_hyena_branch[l].astype(BF16),
            "wo": w_out[l].astype(BF16),
            "norm_ffn2": norm_ffn2[l], "ffn2_wi": ffn2_wi[l].astype(BF16), "ffn2_wo": ffn2_wo[l].astype(BF16),
        }
        yp, proj_p = _layer(yp, mod_ctx, False, batch, seq, w, None, hy_blk=256, hy_tc=128)
        ck = cache_k[:, l].reshape(dec_batch * past, N_KV_HEADS * HEAD_DIM)
        cv = cache_v[:, l].reshape(dec_batch * past, N_KV_HEADS * HEAD_DIM)
        ys, _ = _layer(ys, mod_lat, True, dec_batch, dec_seq, w, (ck, cv), hy_blk=512, hy_tc=128)
        new_ks.append(proj_p[:, OFF_K:OFF_K + SUB].reshape(batch, seq, N_KV_HEADS, HEAD_DIM))
        new_vs.append(proj_p[:, OFF_V:OFF_V + SUB].reshape(batch, seq, N_KV_HEADS, HEAD_DIM))
    new_k = jnp.stack(new_ks, axis=1)
    new_v = jnp.stack(new_vs, axis=1)
    return (yp.reshape(batch, seq, D_MODEL), ys.reshape(dec_batch, dec_seq, D_MODEL), new_k, new_v)
```

```python
import functools
import math

import numpy as np
import jax
import jax.numpy as jnp
from jax import lax
from jax.experimental import pallas as pl
from jax.experimental.pallas import tpu as pltpu

F32 = jnp.float32
BF16 = jnp.bfloat16
HIGHEST = lax.Precision.HIGHEST

D_MODEL = 1024
N_HEADS = 16
N_KV_HEADS = 4
HEAD_DIM = 64
GRID_W = 64
WINDOW = 128
ROPE_BASE = 10000.0
D_HYENA = 1024
N_BANDS = 16
FILTER_EMB = 1 + 2 * N_BANDS
FILTER_WIDTH = 64
D_FF = 2816
N_MOD = 9
EPS = 1e-6
NEG_INF = -1e30
LOG2E = math.log2(math.e)

QKV_COLS = 1536
QKV_K = 1024
QKV_V = 1280
GH_COLS = 5120
GH_GA = 0
GH_GH = 1024
GH_X1 = 2048
GH_X2 = 3072
GH_HV = 4096
SUB = 256

LANES = 128
VMEM_LIMIT = 56 * 1024 * 1024


def _cparams(sem):
    return pltpu.CompilerParams(dimension_semantics=sem, vmem_limit_bytes=VMEM_LIMIT)


def _resident(shape, index_map):
    return pl.BlockSpec(shape, index_map, pipeline_mode=pl.Buffered(1))


def _mod_kernel(c_ref, w_ref, b_ref, o_ref):
    c = c_ref[...]
    s = c * jax.nn.sigmoid(c)
    o_ref[...] = jnp.dot(s, w_ref[...], precision=HIGHEST, preferred_element_type=F32) + b_ref[...]


def _modulation(c_rows, w_mod, b_mod):
    rows = c_rows.shape[0]
    n_out = w_mod.shape[1]
    tn = 1024
    return pl.pallas_call(
        _mod_kernel,
        grid=(n_out // tn,),
        in_specs=[
            pl.BlockSpec((rows, D_MODEL), lambda j: (0, 0)),
            pl.BlockSpec((D_MODEL, tn), lambda j: (0, j)),
            pl.BlockSpec((1, tn), lambda j: (0, j)),
        ],
        out_specs=pl.BlockSpec((rows, tn), lambda j: (0, j)),
        out_shape=jax.ShapeDtypeStruct((rows, n_out), F32),
        compiler_params=_cparams(("arbitrary",)),
        name="modulation",
    )(c_rows, w_mod, b_mod.reshape(1, n_out))


def _norm_modulate(x, g, shift, scale):
    ms = jnp.mean(x * x, axis=-1, keepdims=True)
    y = x * lax.rsqrt(ms + EPS) * g
    return y * (1.0 + scale) + shift


def _ffn_kernel(x_ref, mod_ref, g_ref, wg_ref, wu_ref, wo_ref, o_ref, h_scr, acc_scr, *, mod_base):
    j = pl.program_id(1)

    @pl.when(j == 0)
    def _():
        shift = mod_ref[0, mod_base:mod_base + 1, :]
        scale = mod_ref[0, mod_base + 1:mod_base + 2, :]
        h_scr[...] = _norm_modulate(x_ref[...], g_ref[...], shift, scale).astype(BF16)
        acc_scr[...] = jnp.zeros_like(acc_scr)

    h = h_scr[...]
    gate = jnp.dot(h, wg_ref[...], preferred_element_type=F32)
    up = jnp.dot(h, wu_ref[...], preferred_element_type=F32)
    a = (gate * jax.nn.sigmoid(gate) * up).astype(BF16)
    acc_scr[...] += jnp.dot(a, wo_ref[...], preferred_element_type=F32)

    @pl.when(j == pl.num_programs(1) - 1)
    def _():
        gmod = mod_ref[0, mod_base + 2:mod_base + 3, :]
        o_ref[...] = x_ref[...] + (0.5 * gmod) * acc_scr[...]


def _ffn(x, mod3, tiles_per_mod, g, wi, wo, mod_base, tm, tf):
    t = x.shape[0]
    nj = D_FF // tf
    mod_map = (lambda i, j: (i // tiles_per_mod, 0, 0)) if tiles_per_mod else (lambda i, j: (0, 0, 0))
    return pl.pallas_call(
        functools.partial(_ffn_kernel, mod_base=mod_base),
        grid=(t // tm, nj),
        in_specs=[
            pl.BlockSpec((tm, D_MODEL), lambda i, j: (i, 0)),
            pl.BlockSpec((1, N_MOD, D_MODEL), mod_map),
            pl.BlockSpec((1, D_MODEL), lambda i, j: (0, 0)),
            pl.BlockSpec((D_MODEL, tf), lambda i, j: (0, j)),
            pl.BlockSpec((D_MODEL, tf), lambda i, j: (0, j + nj)),
            pl.BlockSpec((tf, D_MODEL), lambda i, j: (j, 0)),
        ],
        out_specs=pl.BlockSpec((tm, D_MODEL), lambda i, j: (i, 0)),
        out_shape=jax.ShapeDtypeStruct((t, D_MODEL), F32),
        scratch_shapes=[pltpu.VMEM((tm, D_MODEL), BF16), pltpu.VMEM((tm, D_MODEL), F32)],
        compiler_params=_cparams(("parallel", "arbitrary")),
        name="ffn",
    )(x, mod3, g.reshape(1, D_MODEL), wi, wi, wo)


def _rope_partner(y):
    lane = lax.broadcasted_iota(jnp.int32, (1, LANES), 1)
    first = (lane % 32) < 16
    parts = []
    for c in range(y.shape[1] // LANES):
        yc = y[:, c * LANES:(c + 1) * LANES]
        fwd = pltpu.roll(yc, LANES - 16, axis=1)
        bwd = pltpu.roll(yc, 16, axis=1)
        parts.append(jnp.where(first, fwd, bwd))
    return jnp.concatenate(parts, axis=1)


def _qkv_kernel(*refs, rope):
    if rope:
        x_ref, mod_ref, g_ref, w_ref, qg_ref, kg_ref, bd_ref, cos_ref, sin_ref, o_ref, h_ref = refs
    else:
        x_ref, mod_ref, g_ref, w_ref, qg_ref, kg_ref, bd_ref, o_ref, h_ref = refs
    j = pl.program_id(1)

    @pl.when(j == 0)
    def _():
        shift = mod_ref[0, 3:4, :]
        scale = mod_ref[0, 4:5, :]
        h_ref[...] = _norm_modulate(x_ref[...], g_ref[...], shift, scale).astype(BF16)

    def head_norm(t, gain):
        sq = t * t
        hi = sq.astype(BF16)
        lo = (sq - hi.astype(F32)).astype(BF16)
        bd = bd_ref[...]
        ss = (jnp.dot(hi, bd, preferred_element_type=F32)
              + jnp.dot(lo, bd, preferred_element_type=F32))
        y = t * lax.rsqrt(ss * (1.0 / HEAD_DIM) + EPS) * gain
        if rope:
            y = y * cos_ref[...] + _rope_partner(y) * sin_ref[...]
        return y

    @pl.when(j < QKV_K // (2 * SUB))
    def _():
        acc = jnp.dot(h_ref[...], w_ref[...], preferred_element_type=F32)
        gain = qg_ref[...] * (LOG2E / math.sqrt(HEAD_DIM))
        for c in range(2):
            cols = slice(c * SUB, (c + 1) * SUB)
            o_ref[:, cols] = head_norm(acc[:, cols], gain)

    @pl.when(j == QKV_K // (2 * SUB))
    def _():
        acc = jnp.dot(h_ref[...], w_ref[...], preferred_element_type=F32)
        o_ref[:, 0:SUB] = head_norm(acc[:, 0:SUB], kg_ref[...])
        o_ref[:, SUB:2 * SUB] = acc[:, SUB:2 * SUB]


def _qkv_proj(x, mod3, tiles_per_mod, g, w, qg, kg, bd, rope_tabs, tm):
    t = x.shape[0]
    tn = 2 * SUB
    rope = rope_tabs is not None
    mod_map = (lambda i, j: (i // tiles_per_mod, 0, 0)) if tiles_per_mod else (lambda i, j: (0, 0, 0))
    in_specs = [
        pl.BlockSpec((tm, D_MODEL), lambda i, j: (i, 0)),
        pl.BlockSpec((1, N_MOD, D_MODEL), mod_map),
        pl.BlockSpec((1, D_MODEL), lambda i, j: (0, 0)),
        pl.BlockSpec((D_MODEL, tn), lambda i, j: (0, j)),
        pl.BlockSpec((1, SUB), lambda i, j: (0, 0)),
        pl.BlockSpec((1, SUB), lambda i, j: (0, 0)),
        pl.BlockSpec((SUB, SUB), lambda i, j: (0, 0)),
    ]
    args = [x, mod3, g.reshape(1, D_MODEL), w, qg, kg, bd]
    if rope:
        seq_tiles = rope_tabs[0].shape[0] // tm
        for tab in rope_tabs:
            in_specs.append(pl.BlockSpec((tm, SUB), lambda i, j: (i % seq_tiles, 0)))
            args.append(tab)
    return pl.pallas_call(
        functools.partial(_qkv_kernel, rope=rope),
        grid=(t // tm, QKV_COLS // tn),
        in_specs=in_specs,
        out_specs=[pl.BlockSpec((tm, tn), lambda i, j: (i, j)),
                   pl.BlockSpec((tm, D_MODEL), lambda i, j: (i, 0))],
        out_shape=[jax.ShapeDtypeStruct((t, QKV_COLS), F32),
                   jax.ShapeDtypeStruct((t, D_MODEL), BF16)],
        compiler_params=_cparams(("parallel", "arbitrary")),
        name="qkv_proj",
    )(*args)


def _gh_kernel(h_ref, w_ref, o_ref, *, gate_tiles):
    j = pl.program_id(1)

    @pl.when(j < gate_tiles)
    def _():
        o_ref[...] = jax.nn.sigmoid(jnp.dot(h_ref[...], w_ref[...], preferred_element_type=F32))

    @pl.when(j >= gate_tiles)
    def _():
        o_ref[...] = jnp.dot(h_ref[...], w_ref[...], preferred_element_type=F32)


def _gh_proj(h, w, tm, tn):
    t = h.shape[0]
    return pl.pallas_call(
        functools.partial(_gh_kernel, gate_tiles=GH_X1 // tn),
        grid=(t // tm, GH_COLS // tn),
        in_specs=[
            pl.BlockSpec((tm, D_MODEL), lambda i, j: (i, 0)),
            pl.BlockSpec((D_MODEL, tn), lambda i, j: (0, j)),
        ],
        out_specs=pl.BlockSpec((tm, tn), lambda i, j: (i, j)),
        out_shape=jax.ShapeDtypeStruct((t, GH_COLS), F32),
        compiler_params=_cparams(("parallel", "arbitrary")),
        name="gate_hyena_proj",
    )(h, w)


def _place_halves(x, own_half):
    lane = lax.broadcasted_iota(jnp.int32, (1, LANES), 1)
    in_half = [lane < 64, lane >= 64]
    swapped = pltpu.roll(x, 64, axis=1)
    out = [None, None]
    out[own_half] = jnp.where(in_half[own_half], x, 0.0).astype(BF16)
    out[1 - own_half] = jnp.where(in_half[1 - own_half], swapped, 0.0).astype(BF16)
    return out


def _attn_kernel(*refs, window, tq, n_qblocks):
    if window:
        (sink_ref, q_ref, kc_ref, vc_ref, kp_ref, kcur_ref, kn_ref,
         vp_ref, vcur_ref, vn_ref, bias_ref, o_ref) = refs
    else:
        sink_ref, q_ref, kc_ref, vc_ref, o_ref = refs
    kc = kc_ref[...]
    vc = vc_ref[...]
    if window:
        i = pl.program_id(1)
        kw = jnp.concatenate([kp_ref[...], kcur_ref[...], kn_ref[...]], axis=0)
        vw = jnp.concatenate([vp_ref[...], vcur_ref[...], vn_ref[...]], axis=0)
        col = lax.broadcasted_iota(jnp.int32, (1, 3 * WINDOW), 1)
        edge = (jnp.where((col < WINDOW) & (i == 0), NEG_INF, 0.0)
                + jnp.where((col >= 2 * WINDOW) & (i == n_qblocks - 1), NEG_INF, 0.0))
        bias = bias_ref[...] + edge
    row = lax.broadcasted_iota(jnp.int32, (2 * tq, 1), 0)
    dn = (((1,), (1,)), ((), ()))
    for g in range(N_KV_HEADS):
        colblk = slice((g // 2) * LANES, (g // 2 + 1) * LANES)
        own = g % 2
        kc_pl = _place_halves(kc[:, colblk], own)
        vc_pl = _place_halves(vc[:, colblk], own)
        if window:
            kw_pl = _place_halves(kw[:, colblk], own)
            vw_pl = _place_halves(vw[:, colblk], own)
        q0 = q_ref[:, g * 2 * LANES:g * 2 * LANES + LANES]
        q1 = q_ref[:, g * 2 * LANES + LANES:(g + 1) * 2 * LANES]
        qg = jnp.concatenate([q0, q1], axis=0).astype(BF16)
        acc = jnp.zeros((2 * tq, LANES), F32)
        for a in range(2):
            sink = jnp.where(row < tq, sink_ref[4 * g + a], sink_ref[4 * g + 2 + a]) * LOG2E
            s_c = lax.dot_general(qg, kc_pl[a], dn, preferred_element_type=F32)
            m = jnp.maximum(jnp.max(s_c, axis=-1, keepdims=True), sink)
            if window:
                s_w = lax.dot_general(qg, kw_pl[a], dn, preferred_element_type=F32) + bias
                m = jnp.maximum(m, jnp.max(s_w, axis=-1, keepdims=True))
            p_c = jnp.exp2(s_c - m)
            l = jnp.sum(p_c, axis=-1, keepdims=True) + jnp.exp2(sink - m)
            o = jnp.dot(p_c.astype(BF16), vc_pl[a], preferred_element_type=F32)
            if window:
                p_w = jnp.exp2(s_w - m)
                l = l + jnp.sum(p_w, axis=-1, keepdims=True)
                o = o + jnp.dot(p_w.astype(BF16), vw_pl[a], preferred_element_type=F32)
            acc = acc + o * (1.0 / l)
        o_ref[:, g * 2 * LANES:g * 2 * LANES + LANES] = acc[:tq]
        o_ref[:, g * 2 * LANES + LANES:(g + 1) * 2 * LANES] = acc[tq:]


def _window_bias(tq):
    a = np.arange(WINDOW)[:, None]
    b = np.arange(WINDOW)[None, :]
    prev = np.where(b >= a, 0.0, NEG_INF)
    cur = np.zeros((WINDOW, WINDOW))
    nxt = np.where(b <= a, 0.0, NEG_INF)
    one = np.concatenate([prev, cur, nxt], axis=1).astype(np.float32)
    return jnp.asarray(np.concatenate([one, one], axis=0))


def _attention_ctx(qkv, sink, n_batch, seq):
    t = qkv.shape[0]
    tq = seq
    kblk = QKV_K // SUB
    vblk = QKV_V // SUB
    return pl.pallas_call(
        functools.partial(_attn_kernel, window=False, tq=tq, n_qblocks=1),
        grid=(n_batch, 1),
        in_specs=[
            pl.BlockSpec(memory_space=pltpu.SMEM),
            pl.BlockSpec((tq, D_MODEL), lambda b, i: (b, 0)),
            pl.BlockSpec((seq, SUB), lambda b, i: (b, kblk)),
            pl.BlockSpec((seq, SUB), lambda b, i: (b, vblk)),
        ],
        out_specs=pl.BlockSpec((tq, D_MODEL), lambda b, i: (b, 0)),
        out_shape=jax.ShapeDtypeStruct((t, D_MODEL), F32),
        compiler_params=_cparams(("parallel", "arbitrary")),
        name="attn_ctx",
    )(sink, qkv, qkv, qkv)


def _attention_lat(qkv, ck, cv, sink, n_batch, seq, past):
    t = qkv.shape[0]
    tq = WINDOW
    nqb = seq // tq
    kblk = QKV_K // SUB
    vblk = QKV_V // SUB

    def kv_spec(colblk, delta):
        def imap(b, i):
            return (b * nqb + jnp.clip(i + delta, 0, nqb - 1), colblk)
        return pl.BlockSpec((tq, SUB), imap)

    return pl.pallas_call(
        functools.partial(_attn_kernel, window=True, tq=tq, n_qblocks=nqb),
        grid=(n_batch, nqb),
        in_specs=[
            pl.BlockSpec(memory_space=pltpu.SMEM),
            pl.BlockSpec((tq, D_MODEL), lambda b, i: (b * nqb + i, 0)),
            pl.BlockSpec((past, SUB), lambda b, i: (b, 0)),
            pl.BlockSpec((past, SUB), lambda b, i: (b, 0)),
            kv_spec(kblk, -1), kv_spec(kblk, 0), kv_spec(kblk, 1),
            kv_spec(vblk, -1), kv_spec(vblk, 0), kv_spec(vblk, 1),
            pl.BlockSpec((2 * tq, 3 * WINDOW), lambda b, i: (0, 0)),
        ],
        out_specs=pl.BlockSpec((tq, D_MODEL), lambda b, i: (b * nqb + i, 0)),
        out_shape=jax.ShapeDtypeStruct((t, D_MODEL), F32),
        compiler_params=_cparams(("parallel", "arbitrary")),
        name="attn_lat",
    )(sink, qkv, ck, cv, qkv, qkv, qkv, qkv, qkv, qkv, _window_bias(tq))


def _fgen_kernel(z_ref, w1_ref, b1_ref, w2_ref, b2_ref, fr_ref,
                 w3b_ref, b3b_ref, dcb_ref, w3f_ref, b3f_ref, dcf_ref, o_ref, a2_scr, *, n):
    @pl.when((pl.program_id(0) == 0) & (pl.program_id(1) == 0))
    def _():
        fr = fr_ref[...]
        a1 = jnp.sin(fr * (jnp.dot(z_ref[...], w1_ref[...], precision=HIGHEST,
                                   preferred_element_type=F32) + b1_ref[...]))
        a2_scr[...] = jnp.sin(fr * (jnp.dot(a1, w2_ref[...], precision=HIGHEST,
                                            preferred_element_type=F32) + b2_ref[...]))

    tc = o_ref.shape[1]
    tb = jnp.broadcast_to(z_ref[0:n, 0:1], (n, tc))
    tf = jnp.broadcast_to(z_ref[n:2 * n, 0:1], (n, tc))
    hb = (jnp.dot(a2_scr[0:n, :], w3b_ref[...], precision=HIGHEST, preferred_element_type=F32)
          + b3b_ref[...]) * jnp.exp(-tb * jnp.abs(dcb_ref[...]))
    hf = (jnp.dot(a2_scr[n:2 * n, :], w3f_ref[...], precision=HIGHEST, preferred_element_type=F32)
          + b3f_ref[...]) * jnp.exp(-tf * jnp.abs(dcf_ref[...]))
    tot = (jnp.sum(jnp.abs(hb), axis=0, keepdims=True)
           + jnp.sum(jnp.abs(hf), axis=0, keepdims=True))
    inv = 1.0 / (tot + EPS)
    rowid = lax.broadcasted_iota(jnp.int32, (n, tc), 0)
    o_ref[0:n, :] = jnp.where(rowid == 0, 0.0, hb * inv)
    o_ref[n:2 * n, :] = hf * inv


def _filter_gen(n, filt_w1, filt_b1, filt_w2, filt_b2, filt_w3, filt_b3, filt_freq, filt_decay, tc):
    t = jnp.arange(n, dtype=F32) / max(n - 1, 1)
    bands = jnp.arange(1, N_BANDS + 1, dtype=F32)
    ang = 2.0 * math.pi * t[:, None] * bands[None, :]
    z = jnp.concatenate([t[:, None], jnp.cos(ang), jnp.sin(ang)], axis=-1)
    zb = jnp.concatenate([z[0:1], z[1:][::-1]], axis=0)
    zfull = jnp.pad(jnp.concatenate([zb, z], axis=0), ((0, 0), (0, LANES - FILTER_EMB)))
    padw = LANES - FILTER_WIDTH
    w1 = jnp.pad(filt_w1.astype(F32), ((0, LANES - FILTER_EMB), (0, padw)))
    b1 = jnp.pad(filt_b1.astype(F32), (0, padw)).reshape(1, LANES)
    w2 = jnp.pad(filt_w2.astype(F32), ((0, padw), (0, padw)))
    b2 = jnp.pad(filt_b2.astype(F32), (0, padw)).reshape(1, LANES)
    fr = jnp.pad(filt_freq.astype(F32), (0, padw)).reshape(1, LANES)
    w3 = jnp.pad(filt_w3.astype(F32), ((0, padw), (0, 0)))
    ncol = w3.shape[1]
    b3 = filt_b3.astype(F32).reshape(1, ncol)
    dc = filt_decay.astype(F32).reshape(1, ncol)
    ct = D_HYENA // tc
    full = lambda shape: pl.BlockSpec(shape, lambda o, c: (0, 0))
    bwd = lambda rows: pl.BlockSpec((rows, tc), lambda o, c: (0, (2 * o + 1) * ct + c))
    fwd = lambda rows: pl.BlockSpec((rows, tc), lambda o, c: (0, (2 * o) * ct + c))
    return pl.pallas_call(
        functools.partial(_fgen_kernel, n=n),
        grid=(2, ct),
        in_specs=[
            full((2 * n, LANES)), full((LANES, LANES)), full((1, LANES)),
            full((LANES, LANES)), full((1, LANES)), full((1, LANES)),
            bwd(LANES), bwd(1), bwd(1), fwd(LANES), fwd(1), fwd(1),
        ],
        out_specs=pl.BlockSpec((2 * n, tc), lambda o, c: (0, o * ct + c)),
        out_shape=jax.ShapeDtypeStruct((2 * n, 2 * D_HYENA), F32),
        scratch_shapes=[pltpu.VMEM((2 * n, LANES), F32)],
        compiler_params=_cparams(("arbitrary", "arbitrary")),
        name="hyena_filter_gen",
    )(zfull, w1, b1, w2, b2, fr, w3, b3, dc, w3, b3, dc)


def _dft_mats(blk):
    f = np.arange(blk, dtype=np.int64)[:, None]
    s = np.arange(blk, dtype=np.int64)[None, :]
    theta = (np.pi / (2 * blk)) * (((2 * f + 1) * s) % (4 * blk)).astype(np.float64)
    fwd = np.concatenate([np.cos(theta), -np.sin(theta)], axis=0)
    inv = np.concatenate([np.cos(theta).T, -np.sin(theta).T], axis=1) / blk
    return (jnp.asarray(fwd, dtype=F32).astype(BF16), jnp.asarray(inv, dtype=F32).astype(BF16))


def _ftf_kernel(k_ref, skip_ref, f_ref, o_ref, *, nblk2, blk):
    tc = o_ref.shape[2]
    fmat = f_ref[...]
    fidx = lax.broadcasted_iota(jnp.int32, (blk, tc), 0)
    sgn = jnp.where(fidx % 2 == 0, 1.0, -1.0)
    prev = None
    for e in range(nblk2):
        p = jnp.dot(fmat, k_ref[e * blk:(e + 1) * blk, :].astype(BF16), preferred_element_type=F32)
        if e >= 1:
            re = p[0:blk] - sgn * prev[blk:2 * blk]
            if e == nblk2 // 2:
                re = re + skip_ref[...]
            o_ref[e - 1, 0:blk, :] = re
            o_ref[e - 1, blk:2 * blk, :] = p[blk:2 * blk] + sgn * prev[0:blk]
        prev = p


def _filter_transform(kfull, skip, fmat, n, blk, tc):
    nblk2 = 2 * n // blk
    nd = nblk2 - 1
    ct = D_HYENA // tc
    return pl.pallas_call(
        functools.partial(_ftf_kernel, nblk2=nblk2, blk=blk),
        grid=(2, ct),
        in_specs=[
            pl.BlockSpec((2 * n, tc), lambda o, c: (0, o * ct + c)),
            pl.BlockSpec((None, 1, tc), lambda o, c: (o, 0, c)),
            pl.BlockSpec((2 * blk, blk), lambda o, c: (0, 0)),
        ],
        out_specs=pl.BlockSpec((nd, 2 * blk, tc), lambda o, c: (o, 0, c)),
        out_shape=jax.ShapeDtypeStruct((2 * nd, 2 * blk, D_HYENA), F32),
        compiler_params=_cparams(("parallel", "parallel")),
        name="hyena_filter_transform",
    )(kfull, skip, fmat)


CHUNK = 32
HALO = 8


def _short_conv_block(src_ref, b, j, n, blk, w_ref, b_ref):
    nblk = n // blk
    lo = max(j * blk - HALO, 0)
    hi = min((j + 1) * blk + HALO, n)
    off = j * blk - lo
    win = src_ref[b, lo:hi, :]
    prev = pltpu.roll(win, 1, axis=0)[off:off + blk]
    nxt = pltpu.roll(win, hi - lo - 1, axis=0)[off:off + blk]
    u = win[off:off + blk]
    rowid = lax.broadcasted_iota(jnp.int32, u.shape, 0)
    if j == 0:
        prev = jnp.where(rowid == 0, 0.0, prev)
    if j == nblk - 1:
        nxt = jnp.where(rowid == blk - 1, 0.0, nxt)
    return prev * w_ref[0:1, :] + u * w_ref[1:2, :] + nxt * w_ref[2:3, :] + b_ref[...]


def _conv_kernel(zin_ref, gin_ref, wz_ref, bz_ref, wg_ref, bg_ref, g_ref, f_ref, fi_ref,
                 o_ref, zf_scr, yf_scr, *, n, blk, bb, conv_z):
    nblk = n // blk
    tc = o_ref.shape[2]
    fmat = f_ref[...]
    for j in range(nblk):
        rows = slice(j * blk, (j + 1) * blk)
        zs = []
        for b in range(bb):
            if conv_z:
                z = _short_conv_block(zin_ref, b, j, n, blk, wz_ref, bz_ref)
            else:
                z = zin_ref[b, rows, :]
            zs.append(z.astype(BF16))
            o_ref[b, rows, :] = _short_conv_block(gin_ref, b, j, n, blk, wg_ref, bg_ref)
        zf_scr[j] = jnp.dot(fmat, jnp.concatenate(zs, axis=1), preferred_element_type=F32)

    def pairs(i, slot):
        for r in range(blk // CHUNK):
            re_rows = slice(r * CHUNK, (r + 1) * CHUNK)
            im_rows = slice(blk + r * CHUNK, blk + (r + 1) * CHUNK)
            acc_re = [jnp.zeros((CHUNK, tc), F32) for _ in range(bb)]
            acc_im = [jnp.zeros((CHUNK, tc), F32) for _ in range(bb)]
            for j in range(nblk):
                d = i - j + (nblk - 1)
                g_re = g_ref[d, re_rows, :]
                g_im = g_ref[d, im_rows, :]
                for b in range(bb):
                    cols = slice(b * tc, (b + 1) * tc)
                    z_re = zf_scr[j, re_rows, cols]
                    z_im = zf_scr[j, im_rows, cols]
                    acc_re[b] = acc_re[b] + (g_re * z_re - g_im * z_im)
                    acc_im[b] = acc_im[b] + (g_re * z_im + g_im * z_re)
            for b in range(bb):
                cols = slice(b * tc, (b + 1) * tc)
                yf_scr[slot, re_rows, cols] = acc_re[b].astype(BF16)
                yf_scr[slot, im_rows, cols] = acc_im[b].astype(BF16)

    def finish(i, slot):
        y = jnp.dot(fi_ref[...], yf_scr[slot], preferred_element_type=F32)
        rows = pl.ds(pl.multiple_of(i * blk, blk), blk)
        for b in range(bb):
            o_ref[b, rows, :] = o_ref[b, rows, :] * y[:, b * tc:(b + 1) * tc]

    pairs(0, 0)
    if nblk > 1:
        def body(i, carry):
            pairs(i, i & 1)
            finish(i - 1, (i - 1) & 1)
            return carry
        lax.fori_loop(1, nblk, body, 0)
    finish(nblk - 1, (nblk - 1) & 1)


def _long_conv(zsrc, zcol, gsrc, gcol, conv_w, conv_b, gspec, order, fmat, fimat,
               n_batch, n, blk, tc, bb, conv_z):
    nblk = n // blk
    nd = 2 * nblk - 1
    ct = D_HYENA // tc
    hy0 = GH_X1 // tc
    wz = (zcol - hy0) if conv_z else 0
    wg = gcol - hy0
    return pl.pallas_call(
        functools.partial(_conv_kernel, n=n, blk=blk, bb=bb, conv_z=conv_z),
        grid=(ct, n_batch // bb),
        in_specs=[
            pl.BlockSpec((bb, n, tc), lambda c, b: (b, 0, zcol + c)),
            pl.BlockSpec((bb, n, tc), lambda c, b: (b, 0, gcol + c)),
            pl.BlockSpec((3, tc), lambda c, b: (0, wz + c)),
            pl.BlockSpec((1, tc), lambda c, b: (0, wz + c)),
            pl.BlockSpec((3, tc), lambda c, b: (0, wg + c)),
            pl.BlockSpec((1, tc), lambda c, b: (0, wg + c)),
            _resident((nd, 2 * blk, tc), lambda c, b: (order, 0, c)),
            _resident((2 * blk, blk), lambda c, b: (0, 0)),
            _resident((blk, 2 * blk), lambda c, b: (0, 0)),
        ],
        out_specs=pl.BlockSpec((bb, n, tc), lambda c, b: (b, 0, c)),
        out_shape=jax.ShapeDtypeStruct((n_batch, n, D_HYENA), F32),
        scratch_shapes=[
            pltpu.VMEM((nblk, 2 * blk, bb * tc), F32),
            pltpu.VMEM((2, 2 * blk, bb * tc), BF16),
        ],
        compiler_params=_cparams(("parallel", "arbitrary")),
        name="hyena_conv",
    )(zsrc, gsrc, conv_w, conv_b, conv_w, conv_b, gspec, fmat, fimat)


def _hyena(gh, n_batch, n, blk, tc, bb, conv_w, conv_b, skip, filt):
    fmat, fimat = _dft_mats(blk)
    kfull = _filter_gen(n, *filt, tc=128)
    sk = skip.astype(F32).reshape(2, 1, D_HYENA)
    gspec = _filter_transform(kfull, sk, fmat, n, blk, tc=128)
    cw = conv_w.astype(F32)
    cb = conv_b.astype(F32).reshape(1, -1)
    gh3 = gh.reshape(n_batch, n, GH_COLS)
    z2 = _long_conv(gh3, GH_HV // tc, gh3, GH_X1 // tc, cw, cb, gspec, 0, fmat, fimat,
                    n_batch, n, blk, tc, bb, conv_z=True)
    y = _long_conv(z2, 0, gh3, GH_X2 // tc, cw, cb, gspec, 1, fmat, fimat,
                   n_batch, n, blk, tc, bb, conv_z=False)
    return y.reshape(n_batch * n, D_HYENA)


def _outproj_kernel(x_ref, attn_ref, hy_ref, ga_ref, gh_ref, mod_ref, wa_ref, wh_ref, wo_ref, o_ref):
    a = jnp.dot(attn_ref[...].astype(BF16), wa_ref[...], preferred_element_type=F32)
    h = jnp.dot(hy_ref[...].astype(BF16), wh_ref[...], preferred_element_type=F32)
    merged = ga_ref[...] * a + gh_ref[...] * h
    out = jnp.dot(merged.astype(BF16), wo_ref[...], preferred_element_type=F32)
    o_ref[...] = x_ref[...] + mod_ref[0, 5:6, :] * out


def _outproj(x, attn, hy, gh, mod3, tiles_per_mod, wa, wh, wo, tm):
    t = x.shape[0]
    mod_map = (lambda i: (i // tiles_per_mod, 0, 0)) if tiles_per_mod else (lambda i: (0, 0, 0))
    row = lambda i: (i, 0)
    wspec = _resident((D_MODEL, D_MODEL), lambda i: (0, 0))
    return pl.pallas_call(
        _outproj_kernel,
        grid=(t // tm,),
        in_specs=[
            pl.BlockSpec((tm, D_MODEL), row),
            pl.BlockSpec((tm, D_MODEL), row),
            pl.BlockSpec((tm, D_MODEL), row),
            pl.BlockSpec((tm, D_MODEL), lambda i: (i, GH_GA // D_MODEL)),
            pl.BlockSpec((tm, D_MODEL), lambda i: (i, GH_GH // D_MODEL)),
            pl.BlockSpec((1, N_MOD, D_MODEL), mod_map),
            wspec, wspec, wspec,
        ],
        out_specs=pl.BlockSpec((tm, D_MODEL), row),
        out_shape=jax.ShapeDtypeStruct((t, D_MODEL), F32),
        compiler_params=_cparams(("parallel",)),
        name="mixer_out",
    )(x, attn, hy, gh, gh, mod3, wa, wh, wo)


def _rope_tables(n):
    pos = jnp.arange(n)
    row = (pos // GRID_W).astype(F32)
    col = (pos % GRID_W).astype(F32)
    n_freq = HEAD_DIM // 4
    inv = ROPE_BASE ** (-jnp.arange(n_freq, dtype=F32) / n_freq)
    ar = row[:, None] * inv[None, :]
    ac = col[:, None] * inv[None, :]
    cos = jnp.concatenate([jnp.cos(ar), jnp.cos(ar), jnp.cos(ac), jnp.cos(ac)], axis=-1)
    sin = jnp.concatenate([-jnp.sin(ar), jnp.sin(ar), -jnp.sin(ac), jnp.sin(ac)], axis=-1)
    reps = SUB // HEAD_DIM
    return jnp.tile(cos, (1, reps)), jnp.tile(sin, (1, reps))


def _layer(x, mod3, per_batch_mod, n_batch, seq, w, ctx_kv, hy_blk, hy_tc, hy_bb):
    tm = 1024
    tiles_per_mod = (seq // tm) if per_batch_mod else 0
    x = _ffn(x, mod3, tiles_per_mod, w["norm_ffn1"], w["ffn1_wi"], w["ffn1_wo"], 0, tm, 256)
    rope_tabs = _rope_tables(seq) if ctx_kv is not None else None
    qkv, h = _qkv_proj(x, mod3, tiles_per_mod, w["norm_mix"], w["w_qkv"], w["qg"], w["kg"], w["bd"],
                       rope_tabs, tm)
    gh = _gh_proj(h, w["w_gh"], tm, 1024)
    if ctx_kv is None:
        attn = _attention_ctx(qkv, w["sink"], n_batch, seq)
    else:
        attn = _attention_lat(qkv, ctx_kv[0], ctx_kv[1], w["sink"], n_batch, seq,
                              ctx_kv[0].shape[0] // n_batch)
    hy = _hyena(gh, n_batch, seq, hy_blk, hy_tc, hy_bb, w["conv_w"], w["conv_b"], w["hyena_skip"], w["filt"])
    tmo = 512
    x = _outproj(x, attn, hy, gh, mod3, (seq // tmo) if per_batch_mod else 0,
                 w["wa"], w["wh"], w["wo"], tmo)
    x = _ffn(x, mod3, tiles_per_mod, w["norm_ffn2"], w["ffn2_wi"], w["ffn2_wo"], 6, tm, 256)
    return x, qkv


def kernel(x_prompt, x_sample, cache_k, cache_v, c, c_ctx, w_mod, b_mod, norm_ffn1, ffn1_wi, ffn1_wo, norm_mix, w_in, q_norm, k_norm, attn_sink, conv_w, conv_b, filt_w1, filt_b1, filt_w2, filt_b2, filt_w3, filt_b3, filt_freq, filt_decay, hyena_skip, w_attn_branch, w_hyena_branch, w_out, norm_ffn2, ffn2_wi, ffn2_wo):
    batch, seq, _ = x_prompt.shape
    dec_batch, dec_seq, _ = x_sample.shape
    depth = w_mod.shape[0]
    past = cache_k.shape[2]

    yp = x_prompt.reshape(batch * seq, D_MODEL)
    ys = x_sample.reshape(dec_batch * dec_seq, D_MODEL)
    bd = jnp.asarray(np.kron(np.eye(SUB // HEAD_DIM), np.ones((HEAD_DIM, HEAD_DIM))), dtype=F32).astype(BF16)
    new_ks, new_vs = [], []
    for l in range(depth):
        c_rows = jnp.zeros((16, D_MODEL), F32).at[0:dec_batch].set(c).at[dec_batch].set(c_ctx)
        mod = _modulation(c_rows, w_mod[l], b_mod[l]).reshape(16, N_MOD, D_MODEL)
        mod_lat = mod[0:dec_batch]
        mod_ctx = mod[dec_batch:dec_batch + 1]
        wl = w_in[l]
        hy_end = QKV_COLS + 3 * D_HYENA
        w = {
            "norm_ffn1": norm_ffn1[l], "ffn1_wi": ffn1_wi[l].astype(BF16), "ffn1_wo": ffn1_wo[l].astype(BF16),
            "norm_mix": norm_mix[l],
            "w_qkv": wl[:, 0:QKV_COLS].astype(BF16),
            "w_gh": jnp.concatenate([wl[:, hy_end:], wl[:, QKV_COLS:hy_end]], axis=1).astype(BF16),
            "qg": jnp.tile(q_norm[l], SUB // HEAD_DIM).reshape(1, SUB),
            "kg": jnp.tile(k_norm[l], SUB // HEAD_DIM).reshape(1, SUB),
            "bd": bd, "sink": attn_sink[l],
            "conv_w": conv_w[l], "conv_b": conv_b[l], "hyena_skip": hyena_skip[l],
            "filt": (filt_w1[l], filt_b1[l], filt_w2[l], filt_b2[l], filt_w3[l], filt_b3[l],
                     filt_freq[l], filt_decay[l]),
            "wa": w_attn_branch[l].astype(BF16), "wh": w_hyena_branch[l].astype(BF16),
            "wo": w_out[l].astype(BF16),
            "norm_ffn2": norm_ffn2[l], "ffn2_wi": ffn2_wi[l].astype(BF16), "ffn2_wo": ffn2_wo[l].astype(BF16),
        }
        yp, qkv_p = _layer(yp, mod_ctx, False, batch, seq, w, None, hy_blk=256, hy_tc=128, hy_bb=8)
        ck = cache_k[:, l].reshape(dec_batch * past, N_KV_HEADS * HEAD_DIM)
        cv = cache_v[:, l].reshape(dec_batch * past, N_KV_HEADS * HEAD_DIM)
        ys, _ = _layer(ys, mod_lat, True, dec_batch, dec_seq, w, (ck, cv), hy_blk=512, hy_tc=128, hy_bb=2)
        new_ks.append(qkv_p[:, QKV_K:QKV_K + SUB].reshape(batch, seq, N_KV_HEADS, HEAD_DIM))
        new_vs.append(qkv_p[:, QKV_V:QKV_V + SUB].reshape(batch, seq, N_KV_HEADS, HEAD_DIM))
    new_k = jnp.stack(new_ks, axis=1)
    new_v = jnp.stack(new_vs, axis=1)
    return (yp.reshape(batch, seq, D_MODEL), ys.reshape(dec_batch, dec_seq, D_MODEL), new_k, new_v)
```

```python
import functools
import math

import numpy as np
import jax
import jax.numpy as jnp
from jax import lax
from jax.experimental import pallas as pl
from jax.experimental.pallas import tpu as pltpu

F32 = jnp.float32
BF16 = jnp.bfloat16
HIGHEST = lax.Precision.HIGHEST

D_MODEL = 1024
N_HEADS = 16
N_KV_HEADS = 4
HEAD_DIM = 64
GRID_W = 64
WINDOW = 128
ROPE_BASE = 10000.0
D_HYENA = 1024
N_BANDS = 16
FILTER_EMB = 1 + 2 * N_BANDS
FILTER_WIDTH = 64
D_FF = 2816
N_MOD = 9
EPS = 1e-6
NEG_INF = -1e30
LOG2E = math.log2(math.e)

QKV_COLS = 1536
QKV_K = 1024
QKV_V = 1280
GH_COLS = 5120
GH_GA = 0
GH_GH = 1024
GH_X1 = 2048
GH_X2 = 3072
GH_HV = 4096
SUB = 256

LANES = 128
VMEM_LIMIT = 61 * 1024 * 1024


def _cparams(sem):
    return pltpu.CompilerParams(dimension_semantics=sem, vmem_limit_bytes=VMEM_LIMIT)


def _resident(shape, index_map):
    return pl.BlockSpec(shape, index_map, pipeline_mode=pl.Buffered(1))


def _mod_kernel(c_ref, w_ref, b_ref, o_ref):
    c = c_ref[...]
    s = c * jax.nn.sigmoid(c)
    o_ref[...] = jnp.dot(s, w_ref[...], precision=HIGHEST, preferred_element_type=F32) + b_ref[...]


def _modulation(c_rows, w_mod, b_mod):
    rows = c_rows.shape[0]
    n_out = w_mod.shape[1]
    tn = 1024
    return pl.pallas_call(
        _mod_kernel,
        grid=(n_out // tn,),
        in_specs=[
            pl.BlockSpec((rows, D_MODEL), lambda j: (0, 0)),
            pl.BlockSpec((D_MODEL, tn), lambda j: (0, j)),
            pl.BlockSpec((1, tn), lambda j: (0, j)),
        ],
        out_specs=pl.BlockSpec((rows, tn), lambda j: (0, j)),
        out_shape=jax.ShapeDtypeStruct((rows, n_out), F32),
        compiler_params=_cparams(("arbitrary",)),
        name="modulation",
    )(c_rows, w_mod, b_mod.reshape(1, n_out))


def _norm_modulate(x, g, shift, scale):
    ms = jnp.mean(x * x, axis=-1, keepdims=True)
    y = x * lax.rsqrt(ms + EPS) * g
    return y * (1.0 + scale) + shift


def _ffn_kernel(x_ref, mod_ref, g_ref, wg_ref, wu_ref, wo_ref, o_ref, h_scr, acc_scr, *, mod_base):
    j = pl.program_id(1)

    @pl.when(j == 0)
    def _():
        shift = mod_ref[0, mod_base:mod_base + 1, :]
        scale = mod_ref[0, mod_base + 1:mod_base + 2, :]
        h_scr[...] = _norm_modulate(x_ref[...], g_ref[...], shift, scale).astype(BF16)
        acc_scr[...] = jnp.zeros_like(acc_scr)

    h = h_scr[...]
    gate = jnp.dot(h, wg_ref[...], preferred_element_type=F32)
    up = jnp.dot(h, wu_ref[...], preferred_element_type=F32)
    a = (gate * jax.nn.sigmoid(gate) * up).astype(BF16)
    acc_scr[...] += jnp.dot(a, wo_ref[...], preferred_element_type=F32)

    @pl.when(j == pl.num_programs(1) - 1)
    def _():
        gmod = mod_ref[0, mod_base + 2:mod_base + 3, :]
        o_ref[...] = x_ref[...] + (0.5 * gmod) * acc_scr[...]


def _ffn(x, mod3, tiles_per_mod, g, wi, wo, mod_base, tm, tf):
    t = x.shape[0]
    nj = D_FF // tf
    mod_map = (lambda i, j: (i // tiles_per_mod, 0, 0)) if tiles_per_mod else (lambda i, j: (0, 0, 0))
    return pl.pallas_call(
        functools.partial(_ffn_kernel, mod_base=mod_base),
        grid=(t // tm, nj),
        in_specs=[
            pl.BlockSpec((tm, D_MODEL), lambda i, j: (i, 0)),
            pl.BlockSpec((1, N_MOD, D_MODEL), mod_map),
            pl.BlockSpec((1, D_MODEL), lambda i, j: (0, 0)),
            pl.BlockSpec((D_MODEL, tf), lambda i, j: (0, j)),
            pl.BlockSpec((D_MODEL, tf), lambda i, j: (0, j + nj)),
            pl.BlockSpec((tf, D_MODEL), lambda i, j: (j, 0)),
        ],
        out_specs=pl.BlockSpec((tm, D_MODEL), lambda i, j: (i, 0)),
        out_shape=jax.ShapeDtypeStruct((t, D_MODEL), F32),
        scratch_shapes=[pltpu.VMEM((tm, D_MODEL), BF16), pltpu.VMEM((tm, D_MODEL), F32)],
        compiler_params=_cparams(("parallel", "arbitrary")),
        name="ffn",
    )(x, mod3, g.reshape(1, D_MODEL), wi, wi, wo)


def _rope_partner(y):
    lane = lax.broadcasted_iota(jnp.int32, (1, LANES), 1)
    first = (lane % 32) < 16
    parts = []
    for c in range(y.shape[1] // LANES):
        yc = y[:, c * LANES:(c + 1) * LANES]
        fwd = pltpu.roll(yc, LANES - 16, axis=1)
        bwd = pltpu.roll(yc, 16, axis=1)
        parts.append(jnp.where(first, fwd, bwd))
    return jnp.concatenate(parts, axis=1)


def _place_halves(x, own_half):
    lane = lax.broadcasted_iota(jnp.int32, (1, LANES), 1)
    in_half = [lane < 64, lane >= 64]
    swapped = pltpu.roll(x, 64, axis=1)
    out = [None, None]
    out[own_half] = jnp.where(in_half[own_half], x, 0.0).astype(BF16)
    out[1 - own_half] = jnp.where(in_half[1 - own_half], swapped, 0.0).astype(BF16)
    return out


def _store_placed_keys(k, dst_ref):
    for g in range(N_KV_HEADS):
        halves = _place_halves(k[:, (g // 2) * LANES:(g // 2 + 1) * LANES], g % 2)
        for a in range(2):
            dst_ref[:, (2 * g + a) * LANES:(2 * g + a + 1) * LANES] = halves[a]


def _qkv_kernel(*refs, rope, attn_layout):
    refs = list(refs)
    x_ref, mod_ref, g_ref, w_ref, qg_ref, kg_ref, bd_ref = refs[:7]
    refs = refs[7:]
    if rope:
        cos_ref, sin_ref = refs[:2]
        refs = refs[2:]
    if attn_layout:
        q_ref, kp_ref, vt_ref, h_ref = refs
    else:
        o_ref, h_ref = refs
    j = pl.program_id(1)

    @pl.when(j == 0)
    def _():
        shift = mod_ref[0, 3:4, :]
        scale = mod_ref[0, 4:5, :]
        h_ref[...] = _norm_modulate(x_ref[...], g_ref[...], shift, scale).astype(BF16)

    def head_norm(t, gain):
        ss = jnp.dot((t * t).astype(BF16), bd_ref[...], preferred_element_type=F32)
        y = t * lax.rsqrt(ss * (1.0 / HEAD_DIM) + EPS) * gain
        if rope:
            y = y * cos_ref[...] + _rope_partner(y) * sin_ref[...]
        return y

    @pl.when(j < QKV_K // (2 * SUB))
    def _():
        acc = jnp.dot(h_ref[...], w_ref[...], preferred_element_type=F32)
        gain = qg_ref[...] * (LOG2E / math.sqrt(HEAD_DIM))
        for c in range(2):
            cols = slice(c * SUB, (c + 1) * SUB)
            y = head_norm(acc[:, cols], gain)
            if attn_layout:
                q_ref[:, cols] = y.astype(BF16)
            else:
                o_ref[:, cols] = y

    @pl.when(j == QKV_K // (2 * SUB))
    def _():
        acc = jnp.dot(h_ref[...], w_ref[...], preferred_element_type=F32)
        k = head_norm(acc[:, 0:SUB], kg_ref[...])
        v = acc[:, SUB:2 * SUB]
        if attn_layout:
            _store_placed_keys(k, kp_ref)
            vt_ref[...] = v.T.astype(BF16)
        else:
            o_ref[:, 0:SUB] = k
            o_ref[:, SUB:2 * SUB] = v


def _qkv_proj(x, mod3, tiles_per_mod, g, w, qg, kg, bd, rope_tabs, tm, attn_layout):
    t = x.shape[0]
    tn = 2 * SUB
    n_q_tiles = QKV_K // tn
    rope = rope_tabs is not None
    mod_map = (lambda i, j: (i // tiles_per_mod, 0, 0)) if tiles_per_mod else (lambda i, j: (0, 0, 0))
    in_specs = [
        pl.BlockSpec((tm, D_MODEL), lambda i, j: (i, 0)),
        pl.BlockSpec((1, N_MOD, D_MODEL), mod_map),
        pl.BlockSpec((1, D_MODEL), lambda i, j: (0, 0)),
        pl.BlockSpec((D_MODEL, tn), lambda i, j: (0, j)),
        pl.BlockSpec((1, SUB), lambda i, j: (0, 0)),
        pl.BlockSpec((1, SUB), lambda i, j: (0, 0)),
        pl.BlockSpec((SUB, SUB), lambda i, j: (0, 0)),
    ]
    args = [x, mod3, g.reshape(1, D_MODEL), w, qg, kg, bd]
    if rope:
        seq_tiles = rope_tabs[0].shape[0] // tm
        for tab in rope_tabs:
            in_specs.append(pl.BlockSpec((tm, SUB), lambda i, j: (i % seq_tiles, 0)))
            args.append(tab)
    h_spec = pl.BlockSpec((tm, D_MODEL), lambda i, j: (i, 0))
    h_shape = jax.ShapeDtypeStruct((t, D_MODEL), BF16)
    if attn_layout:
        out_specs = [pl.BlockSpec((tm, tn), lambda i, j: (i, jnp.minimum(j, n_q_tiles - 1))),
                     pl.BlockSpec((tm, 2 * N_KV_HEADS * LANES), lambda i, j: (i, 0)),
                     pl.BlockSpec((SUB, tm), lambda i, j: (0, i)),
                     h_spec]
        out_shape = [jax.ShapeDtypeStruct((t, QKV_K), BF16),
                     jax.ShapeDtypeStruct((t, 2 * N_KV_HEADS * LANES), BF16),
                     jax.ShapeDtypeStruct((SUB, t), BF16),
                     h_shape]
    else:
        out_specs = [pl.BlockSpec((tm, tn), lambda i, j: (i, j)), h_spec]
        out_shape = [jax.ShapeDtypeStruct((t, QKV_COLS), F32), h_shape]
    return pl.pallas_call(
        functools.partial(_qkv_kernel, rope=rope, attn_layout=attn_layout),
        grid=(t // tm, QKV_COLS // tn),
        in_specs=in_specs,
        out_specs=out_specs,
        out_shape=out_shape,
        compiler_params=_cparams(("parallel", "arbitrary")),
        name="qkv_proj",
    )(*args)


def _gh_kernel(h_ref, w_ref, o_ref, *, gate_tiles):
    j = pl.program_id(1)

    @pl.when(j < gate_tiles)
    def _():
        o_ref[...] = jax.nn.sigmoid(jnp.dot(h_ref[...], w_ref[...], preferred_element_type=F32))

    @pl.when(j >= gate_tiles)
    def _():
        o_ref[...] = jnp.dot(h_ref[...], w_ref[...], preferred_element_type=F32)


def _gh_proj(h, w, tm, tn):
    t = h.shape[0]
    return pl.pallas_call(
        functools.partial(_gh_kernel, gate_tiles=GH_X1 // tn),
        grid=(t // tm, GH_COLS // tn),
        in_specs=[
            pl.BlockSpec((tm, D_MODEL), lambda i, j: (i, 0)),
            pl.BlockSpec((D_MODEL, tn), lambda i, j: (0, j)),
        ],
        out_specs=pl.BlockSpec((tm, tn), lambda i, j: (i, j)),
        out_shape=jax.ShapeDtypeStruct((t, GH_COLS), F32),
        compiler_params=_cparams(("parallel", "arbitrary")),
        name="gate_hyena_proj",
    )(h, w)


def _col_reduce(x, op):
    rows, cols = x.shape
    part = op(x.reshape(8, rows // 8, cols), axis=0)
    return op(part, axis=0, keepdims=True)


def _attn_kernel(*refs, window, tq, n_qblocks):
    if window:
        (sink_ref, q_ref, kc_ref, vc_ref, kp_ref, kcur_ref, kn_ref,
         vp_ref, vcur_ref, vn_ref, bias_ref, o_ref, kcp_scr, vct_scr) = refs
    else:
        sink_ref, q_ref, kc_ref, vc_ref, o_ref, kcp_scr, vct_scr = refs
    i = pl.program_id(1)

    @pl.when(i == 0)
    def _():
        _store_placed_keys(kc_ref[...], kcp_scr)
        vct_scr[...] = vc_ref[...].T.astype(BF16)

    if window:
        krow = lax.broadcasted_iota(jnp.int32, (3 * WINDOW, 2 * tq), 0)
        edge = (jnp.where((krow < WINDOW) & (i == 0), NEG_INF, 0.0)
                + jnp.where((krow >= 2 * WINDOW) & (i == n_qblocks - 1), NEG_INF, 0.0))
        bias = bias_ref[...] + edge
    lane = lax.broadcasted_iota(jnp.int32, (1, 2 * tq), 1)
    dn = (((1,), (1,)), ((), ()))
    for g in range(N_KV_HEADS):
        q0 = q_ref[:, g * 2 * LANES:g * 2 * LANES + LANES]
        q1 = q_ref[:, g * 2 * LANES + LANES:(g + 1) * 2 * LANES]
        qg = jnp.concatenate([q0, q1], axis=0).astype(BF16)
        vrows = slice(g * HEAD_DIM, (g + 1) * HEAD_DIM)
        if window:
            vw_t = jnp.concatenate([vp_ref[vrows, :], vcur_ref[vrows, :], vn_ref[vrows, :]], axis=1)
        outs = []
        for a in range(2):
            kcols = slice((2 * g + a) * LANES, (2 * g + a + 1) * LANES)
            sink = jnp.where(lane < tq, sink_ref[4 * g + a], sink_ref[4 * g + 2 + a]) * LOG2E
            s_c = lax.dot_general(kcp_scr[:, kcols], qg, dn, preferred_element_type=F32)
            m = jnp.maximum(_col_reduce(s_c, jnp.max), sink)
            if window:
                kw = jnp.concatenate([kp_ref[:, kcols], kcur_ref[:, kcols], kn_ref[:, kcols]], axis=0)
                s_w = lax.dot_general(kw, qg, dn, preferred_element_type=F32) + bias
                m = jnp.maximum(m, _col_reduce(s_w, jnp.max))
            p_c = jnp.exp2(s_c - m)
            l = _col_reduce(p_c, jnp.sum) + jnp.exp2(sink - m)
            o = jnp.dot(vct_scr[vrows, :], p_c.astype(BF16), preferred_element_type=F32)
            if window:
                p_w = jnp.exp2(s_w - m)
                l = l + _col_reduce(p_w, jnp.sum)
                o = o + jnp.dot(vw_t, p_w.astype(BF16), preferred_element_type=F32)
            outs.append(o * (1.0 / l))
        o_t = jnp.concatenate(outs, axis=0)
        o_ref[:, g * 2 * LANES:g * 2 * LANES + LANES] = o_t[:, 0:tq].T
        o_ref[:, g * 2 * LANES + LANES:(g + 1) * 2 * LANES] = o_t[:, tq:2 * tq].T


def _window_bias_t(tq):
    a = np.arange(WINDOW)[None, :]
    b = np.arange(WINDOW)[:, None]
    prev = np.where(b >= a, 0.0, NEG_INF)
    cur = np.zeros((WINDOW, WINDOW))
    nxt = np.where(b <= a, 0.0, NEG_INF)
    one = np.concatenate([prev, cur, nxt], axis=0).astype(np.float32)
    return jnp.asarray(np.concatenate([one, one], axis=1))


def _attention_ctx(qkv, sink, n_batch, seq):
    t = qkv.shape[0]
    tq = seq
    kblk = QKV_K // SUB
    vblk = QKV_V // SUB
    return pl.pallas_call(
        functools.partial(_attn_kernel, window=False, tq=tq, n_qblocks=1),
        grid=(n_batch, 1),
        in_specs=[
            pl.BlockSpec(memory_space=pltpu.SMEM),
            pl.BlockSpec((tq, D_MODEL), lambda b, i: (b, 0)),
            pl.BlockSpec((seq, SUB), lambda b, i: (b, kblk)),
            pl.BlockSpec((seq, SUB), lambda b, i: (b, vblk)),
        ],
        out_specs=pl.BlockSpec((tq, D_MODEL), lambda b, i: (b, 0)),
        out_shape=jax.ShapeDtypeStruct((t, D_MODEL), F32),
        scratch_shapes=[pltpu.VMEM((seq, 2 * N_KV_HEADS * LANES), BF16), pltpu.VMEM((SUB, seq), BF16)],
        compiler_params=_cparams(("parallel", "arbitrary")),
        name="attn_ctx",
    )(sink, qkv, qkv, qkv)


def _attention_lat(q, kp, vt, ck, cv, sink, n_batch, seq, past):
    t = q.shape[0]
    tq = WINDOW
    nqb = seq // tq
    kwidth = 2 * N_KV_HEADS * LANES

    def kblock(delta):
        return pl.BlockSpec((tq, kwidth), lambda b, i: (b * nqb + jnp.clip(i + delta, 0, nqb - 1), 0))

    def vblock(delta):
        return pl.BlockSpec((SUB, tq), lambda b, i: (0, b * nqb + jnp.clip(i + delta, 0, nqb - 1)))

    return pl.pallas_call(
        functools.partial(_attn_kernel, window=True, tq=tq, n_qblocks=nqb),
        grid=(n_batch, nqb),
        in_specs=[
            pl.BlockSpec(memory_space=pltpu.SMEM),
            pl.BlockSpec((tq, D_MODEL), lambda b, i: (b * nqb + i, 0)),
            pl.BlockSpec((past, SUB), lambda b, i: (b, 0)),
            pl.BlockSpec((past, SUB), lambda b, i: (b, 0)),
            kblock(-1), kblock(0), kblock(1),
            vblock(-1), vblock(0), vblock(1),
            pl.BlockSpec((3 * WINDOW, 2 * tq), lambda b, i: (0, 0)),
        ],
        out_specs=pl.BlockSpec((tq, D_MODEL), lambda b, i: (b * nqb + i, 0)),
        out_shape=jax.ShapeDtypeStruct((t, D_MODEL), F32),
        scratch_shapes=[pltpu.VMEM((past, kwidth), BF16), pltpu.VMEM((SUB, past), BF16)],
        compiler_params=_cparams(("parallel", "arbitrary")),
        name="attn_lat",
    )(sink, q, ck, cv, kp, kp, kp, vt, vt, vt, _window_bias_t(tq))


def _fgen_kernel(z_ref, w1_ref, b1_ref, w2_ref, b2_ref, fr_ref,
                 w3b_ref, b3b_ref, dcb_ref, w3f_ref, b3f_ref, dcf_ref, o_ref, a2_scr, *, n):
    @pl.when((pl.program_id(0) == 0) & (pl.program_id(1) == 0))
    def _():
        fr = fr_ref[...]
        a1 = jnp.sin(fr * (jnp.dot(z_ref[...], w1_ref[...], precision=HIGHEST,
                                   preferred_element_type=F32) + b1_ref[...]))
        a2_scr[...] = jnp.sin(fr * (jnp.dot(a1, w2_ref[...], precision=HIGHEST,
                                            preferred_element_type=F32) + b2_ref[...]))

    tc = o_ref.shape[1]
    tb = jnp.broadcast_to(z_ref[0:n, 0:1], (n, tc))
    tf = jnp.broadcast_to(z_ref[n:2 * n, 0:1], (n, tc))
    hb = (jnp.dot(a2_scr[0:n, :], w3b_ref[...], precision=HIGHEST, preferred_element_type=F32)
          + b3b_ref[...]) * jnp.exp(-tb * jnp.abs(dcb_ref[...]))
    hf = (jnp.dot(a2_scr[n:2 * n, :], w3f_ref[...], precision=HIGHEST, preferred_element_type=F32)
          + b3f_ref[...]) * jnp.exp(-tf * jnp.abs(dcf_ref[...]))
    tot = (jnp.sum(jnp.abs(hb), axis=0, keepdims=True)
           + jnp.sum(jnp.abs(hf), axis=0, keepdims=True))
    inv = 1.0 / (tot + EPS)
    rowid = lax.broadcasted_iota(jnp.int32, (n, tc), 0)
    o_ref[0:n, :] = jnp.where(rowid == 0, 0.0, hb * inv)
    o_ref[n:2 * n, :] = hf * inv


def _filter_gen(n, filt_w1, filt_b1, filt_w2, filt_b2, filt_w3, filt_b3, filt_freq, filt_decay, tc):
    t = jnp.arange(n, dtype=F32) / max(n - 1, 1)
    bands = jnp.arange(1, N_BANDS + 1, dtype=F32)
    ang = 2.0 * math.pi * t[:, None] * bands[None, :]
    z = jnp.concatenate([t[:, None], jnp.cos(ang), jnp.sin(ang)], axis=-1)
    zb = jnp.concatenate([z[0:1], z[1:][::-1]], axis=0)
    zfull = jnp.pad(jnp.concatenate([zb, z], axis=0), ((0, 0), (0, LANES - FILTER_EMB)))
    padw = LANES - FILTER_WIDTH
    w1 = jnp.pad(filt_w1.astype(F32), ((0, LANES - FILTER_EMB), (0, padw)))
    b1 = jnp.pad(filt_b1.astype(F32), (0, padw)).reshape(1, LANES)
    w2 = jnp.pad(filt_w2.astype(F32), ((0, padw), (0, padw)))
    b2 = jnp.pad(filt_b2.astype(F32), (0, padw)).reshape(1, LANES)
    fr = jnp.pad(filt_freq.astype(F32), (0, padw)).reshape(1, LANES)
    w3 = jnp.pad(filt_w3.astype(F32), ((0, padw), (0, 0)))
    ncol = w3.shape[1]
    b3 = filt_b3.astype(F32).reshape(1, ncol)
    dc = filt_decay.astype(F32).reshape(1, ncol)
    ct = D_HYENA // tc
    full = lambda shape: pl.BlockSpec(shape, lambda o, c: (0, 0))
    bwd = lambda rows: pl.BlockSpec((rows, tc), lambda o, c: (0, (2 * o + 1) * ct + c))
    fwd = lambda rows: pl.BlockSpec((rows, tc), lambda o, c: (0, (2 * o) * ct + c))
    return pl.pallas_call(
        functools.partial(_fgen_kernel, n=n),
        grid=(2, ct),
        in_specs=[
            full((2 * n, LANES)), full((LANES, LANES)), full((1, LANES)),
            full((LANES, LANES)), full((1, LANES)), full((1, LANES)),
            bwd(LANES), bwd(1), bwd(1), fwd(LANES), fwd(1), fwd(1),
        ],
        out_specs=pl.BlockSpec((2 * n, tc), lambda o, c: (0, o * ct + c)),
        out_shape=jax.ShapeDtypeStruct((2 * n, 2 * D_HYENA), F32),
        scratch_shapes=[pltpu.VMEM((2 * n, LANES), F32)],
        compiler_params=_cparams(("arbitrary", "arbitrary")),
        name="hyena_filter_gen",
    )(zfull, w1, b1, w2, b2, fr, w3, b3, dc, w3, b3, dc)


def _dft_mats(blk):
    f = np.arange(blk, dtype=np.int64)[:, None]
    s = np.arange(blk, dtype=np.int64)[None, :]
    theta = (np.pi / (2 * blk)) * (((2 * f + 1) * s) % (4 * blk)).astype(np.float64)
    fwd = np.concatenate([np.cos(theta), -np.sin(theta)], axis=0)
    inv = np.concatenate([np.cos(theta).T, -np.sin(theta).T], axis=1) / blk
    return (jnp.asarray(fwd, dtype=F32).astype(BF16), jnp.asarray(inv, dtype=F32).astype(BF16))


def _ftf_kernel(k_ref, skip_ref, f_ref, o_ref, *, nblk2, blk):
    tc = o_ref.shape[2]
    fmat = f_ref[...]
    fidx = lax.broadcasted_iota(jnp.int32, (blk, tc), 0)
    sgn = jnp.where(fidx % 2 == 0, 1.0, -1.0)
    prev = None
    for e in range(nblk2):
        p = jnp.dot(fmat, k_ref[e * blk:(e + 1) * blk, :].astype(BF16), preferred_element_type=F32)
        if e >= 1:
            re = p[0:blk] - sgn * prev[blk:2 * blk]
            im = p[blk:2 * blk] + sgn * prev[0:blk]
            if e == nblk2 // 2:
                re = re + skip_ref[...]
            o_ref[e - 1, 0:blk, :] = re
            o_ref[e - 1, blk:2 * blk, :] = im - re
            o_ref[e - 1, 2 * blk:3 * blk, :] = re + im
        prev = p


def _filter_transform(kfull, skip, fmat, n, blk, tc):
    nblk2 = 2 * n // blk
    nd = nblk2 - 1
    ct = D_HYENA // tc
    return pl.pallas_call(
        functools.partial(_ftf_kernel, nblk2=nblk2, blk=blk),
        grid=(2, ct),
        in_specs=[
            pl.BlockSpec((2 * n, tc), lambda o, c: (0, o * ct + c)),
            pl.BlockSpec((None, 1, tc), lambda o, c: (o, 0, c)),
            pl.BlockSpec((2 * blk, blk), lambda o, c: (0, 0)),
        ],
        out_specs=pl.BlockSpec((nd, 3 * blk, tc), lambda o, c: (o, 0, c)),
        out_shape=jax.ShapeDtypeStruct((2 * nd, 3 * blk, D_HYENA), F32),
        compiler_params=_cparams(("parallel", "parallel")),
        name="hyena_filter_transform",
    )(kfull, skip, fmat)


CHUNK = 32
HALO = 8


def _short_conv_block(src_ref, b, j, n, blk, w_ref, b_ref):
    nblk = n // blk
    lo = max(j * blk - HALO, 0)
    hi = min((j + 1) * blk + HALO, n)
    off = j * blk - lo
    win = src_ref[b, lo:hi, :]
    prev = pltpu.roll(win, 1, axis=0)[off:off + blk]
    nxt = pltpu.roll(win, hi - lo - 1, axis=0)[off:off + blk]
    u = win[off:off + blk]
    rowid = lax.broadcasted_iota(jnp.int32, u.shape, 0)
    if j == 0:
        prev = jnp.where(rowid == 0, 0.0, prev)
    if j == nblk - 1:
        nxt = jnp.where(rowid == blk - 1, 0.0, nxt)
    return prev * w_ref[0:1, :] + u * w_ref[1:2, :] + nxt * w_ref[2:3, :] + b_ref[...]


def _conv_kernel(zin_ref, gin_ref, wz_ref, bz_ref, wg_ref, bg_ref, g_ref, f_ref, fi_ref,
                 o_ref, zf_scr, yf_scr, *, n, blk, bb, conv_z):
    nblk = n // blk
    tc = o_ref.shape[2]
    fmat = f_ref[...]
    for j in range(nblk):
        rows = slice(j * blk, (j + 1) * blk)
        zs = []
        for b in range(bb):
            if conv_z:
                z = _short_conv_block(zin_ref, b, j, n, blk, wz_ref, bz_ref)
            else:
                z = zin_ref[b, rows, :]
            zs.append(z.astype(BF16))
            o_ref[b, rows, :] = _short_conv_block(gin_ref, b, j, n, blk, wg_ref, bg_ref)
        zf = jnp.dot(fmat, jnp.concatenate(zs, axis=1), preferred_element_type=F32)
        zf_scr[j, 0:2 * blk, :] = zf
        zf_scr[j, 2 * blk:3 * blk, :] = zf[0:blk] + zf[blk:2 * blk]

    def pairs(i, slot):
        for r in range(blk // CHUNK):
            rows = [slice(p * blk + r * CHUNK, p * blk + (r + 1) * CHUNK) for p in range(3)]
            k1 = [jnp.zeros((CHUNK, tc), F32) for _ in range(bb)]
            k2 = [jnp.zeros((CHUNK, tc), F32) for _ in range(bb)]
            k3 = [jnp.zeros((CHUNK, tc), F32) for _ in range(bb)]
            for j in range(nblk):
                d = i - j + (nblk - 1)
                g_c = g_ref[d, rows[0], :]
                g_dmc = g_ref[d, rows[1], :]
                g_cpd = g_ref[d, rows[2], :]
                for b in range(bb):
                    cols = slice(b * tc, (b + 1) * tc)
                    k1[b] = k1[b] + g_c * zf_scr[j, rows[2], cols]
                    k2[b] = k2[b] + g_dmc * zf_scr[j, rows[0], cols]
                    k3[b] = k3[b] + g_cpd * zf_scr[j, rows[1], cols]
            for b in range(bb):
                cols = slice(b * tc, (b + 1) * tc)
                yf_scr[slot, rows[0], cols] = (k1[b] - k3[b]).astype(BF16)
                yf_scr[slot, rows[1], cols] = (k1[b] + k2[b]).astype(BF16)

    def finish(i, slot):
        y = jnp.dot(fi_ref[...], yf_scr[slot], preferred_element_type=F32)
        rows = pl.ds(pl.multiple_of(i * blk, blk), blk)
        for b in range(bb):
            o_ref[b, rows, :] = o_ref[b, rows, :] * y[:, b * tc:(b + 1) * tc]

    pairs(0, 0)
    if nblk > 1:
        def body(i, carry):
            pairs(i, i & 1)
            finish(i - 1, (i - 1) & 1)
            return carry
        lax.fori_loop(1, nblk, body, 0)
    finish(nblk - 1, (nblk - 1) & 1)


def _long_conv(zsrc, zcol, gsrc, gcol, conv_w, conv_b, gspec, order, fmat, fimat,
               n_batch, n, blk, tc, bb, conv_z):
    nblk = n // blk
    nd = 2 * nblk - 1
    ct = D_HYENA // tc
    hy0 = GH_X1 // tc
    wz = (zcol - hy0) if conv_z else 0
    wg = gcol - hy0
    return pl.pallas_call(
        functools.partial(_conv_kernel, n=n, blk=blk, bb=bb, conv_z=conv_z),
        grid=(ct, n_batch // bb),
        in_specs=[
            pl.BlockSpec((bb, n, tc), lambda c, b: (b, 0, zcol + c)),
            pl.BlockSpec((bb, n, tc), lambda c, b: (b, 0, gcol + c)),
            pl.BlockSpec((3, tc), lambda c, b: (0, wz + c)),
            pl.BlockSpec((1, tc), lambda c, b: (0, wz + c)),
            pl.BlockSpec((3, tc), lambda c, b: (0, wg + c)),
            pl.BlockSpec((1, tc), lambda c, b: (0, wg + c)),
            _resident((nd, 3 * blk, tc), lambda c, b: (order, 0, c)),
            _resident((2 * blk, blk), lambda c, b: (0, 0)),
            _resident((blk, 2 * blk), lambda c, b: (0, 0)),
        ],
        out_specs=pl.BlockSpec((bb, n, tc), lambda c, b: (b, 0, c)),
        out_shape=jax.ShapeDtypeStruct((n_batch, n, D_HYENA), F32),
        scratch_shapes=[
            pltpu.VMEM((nblk, 3 * blk, bb * tc), F32),
            pltpu.VMEM((2, 2 * blk, bb * tc), BF16),
        ],
        compiler_params=_cparams(("parallel", "arbitrary")),
        name="hyena_conv",
    )(zsrc, gsrc, conv_w, conv_b, conv_w, conv_b, gspec, fmat, fimat)


def _hyena(gh, n_batch, n, blk, tc, bb, conv_w, conv_b, skip, filt):
    fmat, fimat = _dft_mats(blk)
    kfull = _filter_gen(n, *filt, tc=128)
    sk = skip.astype(F32).reshape(2, 1, D_HYENA)
    gspec = _filter_transform(kfull, sk, fmat, n, blk, tc=128)
    cw = conv_w.astype(F32)
    cb = conv_b.astype(F32).reshape(1, -1)
    gh3 = gh.reshape(n_batch, n, GH_COLS)
    z2 = _long_conv(gh3, GH_HV // tc, gh3, GH_X1 // tc, cw, cb, gspec, 0, fmat, fimat,
                    n_batch, n, blk, tc, bb, conv_z=True)
    y = _long_conv(z2, 0, gh3, GH_X2 // tc, cw, cb, gspec, 1, fmat, fimat,
                   n_batch, n, blk, tc, bb, conv_z=False)
    return y.reshape(n_batch * n, D_HYENA)


def _outproj_kernel(x_ref, attn_ref, hy_ref, ga_ref, gh_ref, mod_ref, wa_ref, wh_ref, wo_ref, o_ref):
    a = jnp.dot(attn_ref[...].astype(BF16), wa_ref[...], preferred_element_type=F32)
    h = jnp.dot(hy_ref[...].astype(BF16), wh_ref[...], preferred_element_type=F32)
    merged = ga_ref[...] * a + gh_ref[...] * h
    out = jnp.dot(merged.astype(BF16), wo_ref[...], preferred_element_type=F32)
    o_ref[...] = x_ref[...] + mod_ref[0, 5:6, :] * out


def _outproj(x, attn, hy, gh, mod3, tiles_per_mod, wa, wh, wo, tm):
    t = x.shape[0]
    mod_map = (lambda i: (i // tiles_per_mod, 0, 0)) if tiles_per_mod else (lambda i: (0, 0, 0))
    row = lambda i: (i, 0)
    wspec = _resident((D_MODEL, D_MODEL), lambda i: (0, 0))
    return pl.pallas_call(
        _outproj_kernel,
        grid=(t // tm,),
        in_specs=[
            pl.BlockSpec((tm, D_MODEL), row),
            pl.BlockSpec((tm, D_MODEL), row),
            pl.BlockSpec((tm, D_MODEL), row),
            pl.BlockSpec((tm, D_MODEL), lambda i: (i, GH_GA // D_MODEL)),
            pl.BlockSpec((tm, D_MODEL), lambda i: (i, GH_GH // D_MODEL)),
            pl.BlockSpec((1, N_MOD, D_MODEL), mod_map),
            wspec, wspec, wspec,
        ],
        out_specs=pl.BlockSpec((tm, D_MODEL), row),
        out_shape=jax.ShapeDtypeStruct((t, D_MODEL), F32),
        compiler_params=_cparams(("parallel",)),
        name="mixer_out",
    )(x, attn, hy, gh, gh, mod3, wa, wh, wo)


def _rope_tables(n):
    pos = jnp.arange(n)
    row = (pos // GRID_W).astype(F32)
    col = (pos % GRID_W).astype(F32)
    n_freq = HEAD_DIM // 4
    inv = ROPE_BASE ** (-jnp.arange(n_freq, dtype=F32) / n_freq)
    ar = row[:, None] * inv[None, :]
    ac = col[:, None] * inv[None, :]
    cos = jnp.concatenate([jnp.cos(ar), jnp.cos(ar), jnp.cos(ac), jnp.cos(ac)], axis=-1)
    sin = jnp.concatenate([-jnp.sin(ar), jnp.sin(ar), -jnp.sin(ac), jnp.sin(ac)], axis=-1)
    reps = SUB // HEAD_DIM
    return jnp.tile(cos, (1, reps)), jnp.tile(sin, (1, reps))


def _layer(x, mod3, per_batch_mod, n_batch, seq, w, ctx_kv, hy_blk, hy_tc, hy_bb):
    tm = 1024
    tiles_per_mod = (seq // tm) if per_batch_mod else 0
    tmf = 2048
    ffn_tiles_per_mod = (seq // tmf) if per_batch_mod else 0
    x = _ffn(x, mod3, ffn_tiles_per_mod, w["norm_ffn1"], w["ffn1_wi"], w["ffn1_wo"], 0, tmf, 256)
    qkv = None
    if ctx_kv is None:
        qkv, h = _qkv_proj(x, mod3, tiles_per_mod, w["norm_mix"], w["w_qkv"], w["qg"], w["kg"], w["bd"],
                           None, tm, attn_layout=False)
        attn = _attention_ctx(qkv, w["sink"], n_batch, seq)
    else:
        q, kp, vt, h = _qkv_proj(x, mod3, tiles_per_mod, w["norm_mix"], w["w_qkv"], w["qg"], w["kg"],
                                 w["bd"], _rope_tables(seq), tm, attn_layout=True)
        attn = _attention_lat(q, kp, vt, ctx_kv[0], ctx_kv[1], w["sink"], n_batch, seq,
                              ctx_kv[0].shape[0] // n_batch)
    gh = _gh_proj(h, w["w_gh"], tm, 1024)
    hy = _hyena(gh, n_batch, seq, hy_blk, hy_tc, hy_bb, w["conv_w"], w["conv_b"], w["hyena_skip"], w["filt"])
    tmo = 512
    x = _outproj(x, attn, hy, gh, mod3, (seq // tmo) if per_batch_mod else 0,
                 w["wa"], w["wh"], w["wo"], tmo)
    x = _ffn(x, mod3, ffn_tiles_per_mod, w["norm_ffn2"], w["ffn2_wi"], w["ffn2_wo"], 6, tmf, 256)
    return x, qkv


def kernel(x_prompt, x_sample, cache_k, cache_v, c, c_ctx, w_mod, b_mod, norm_ffn1, ffn1_wi, ffn1_wo, norm_mix, w_in, q_norm, k_norm, attn_sink, conv_w, conv_b, filt_w1, filt_b1, filt_w2, filt_b2, filt_w3, filt_b3, filt_freq, filt_decay, hyena_skip, w_attn_branch, w_hyena_branch, w_out, norm_ffn2, ffn2_wi, ffn2_wo):
    batch, seq, _ = x_prompt.shape
    dec_batch, dec_seq, _ = x_sample.shape
    depth = w_mod.shape[0]
    past = cache_k.shape[2]

    yp = x_prompt.reshape(batch * seq, D_MODEL)
    ys = x_sample.reshape(dec_batch * dec_seq, D_MODEL)
    bd = jnp.asarray(np.kron(np.eye(SUB // HEAD_DIM), np.ones((HEAD_DIM, HEAD_DIM))), dtype=F32).astype(BF16)
    new_ks, new_vs = [], []
    for l in range(depth):
        c_rows = jnp.zeros((16, D_MODEL), F32).at[0:dec_batch].set(c).at[dec_batch].set(c_ctx)
        mod = _modulation(c_rows, w_mod[l], b_mod[l]).reshape(16, N_MOD, D_MODEL)
        mod_lat = mod[0:dec_batch]
        mod_ctx = mod[dec_batch:dec_batch + 1]
        wl = w_in[l]
        hy_end = QKV_COLS + 3 * D_HYENA
        w = {
            "norm_ffn1": norm_ffn1[l], "ffn1_wi": ffn1_wi[l].astype(BF16), "ffn1_wo": ffn1_wo[l].astype(BF16),
            "norm_mix": norm_mix[l],
            "w_qkv": wl[:, 0:QKV_COLS].astype(BF16),
            "w_gh": jnp.concatenate([wl[:, hy_end:], wl[:, QKV_COLS:hy_end]], axis=1).astype(BF16),
            "qg": jnp.tile(q_norm[l], SUB // HEAD_DIM).reshape(1, SUB),
            "kg": jnp.tile(k_norm[l], SUB // HEAD_DIM).reshape(1, SUB),
            "bd": bd, "sink": attn_sink[l],
            "conv_w": conv_w[l], "conv_b": conv_b[l], "hyena_skip": hyena_skip[l],
            "filt": (filt_w1[l], filt_b1[l], filt_w2[l], filt_b2[l], filt_w3[l], filt_b3[l],
                     filt_freq[l], filt_decay[l]),
            "wa": w_attn_branch[l].astype(BF16), "wh": w_hyena_branch[l].astype(BF16),
            "wo": w_out[l].astype(BF16),
            "norm_ffn2": norm_ffn2[l], "ffn2_wi": ffn2_wi[l].astype(BF16), "ffn2_wo": ffn2_wo[l].astype(BF16),
        }
        yp, qkv_p = _layer(yp, mod_ctx, False, batch, seq, w, None, hy_blk=256, hy_tc=128, hy_bb=8)
        ck = cache_k[:, l].reshape(dec_batch * past, N_KV_HEADS * HEAD_DIM)
        cv = cache_v[:, l].reshape(dec_batch * past, N_KV_HEADS * HEAD_DIM)
        ys, _ = _layer(ys, mod_lat, True, dec_batch, dec_seq, w, (ck, cv), hy_blk=512, hy_tc=128, hy_bb=2)
        new_ks.append(qkv_p[:, QKV_K:QKV_K + SUB].reshape(batch, seq, N_KV_HEADS, HEAD_DIM))
        new_vs.append(qkv_p[:, QKV_V:QKV_V + SUB].reshape(batch, seq, N_KV_HEADS, HEAD_DIM))
    new_k = jnp.stack(new_ks, axis=1)
    new_v = jnp.stack(new_vs, axis=1)
    return (yp.reshape(batch, seq, D_MODEL), ys.reshape(dec_batch, dec_seq, D_MODEL), new_k, new_v)
```

```python
import functools
import math

import numpy as np
import jax
import jax.numpy as jnp
from jax import lax
from jax.experimental import pallas as pl
from jax.experimental.pallas import tpu as pltpu

F32 = jnp.float32
BF16 = jnp.bfloat16
HIGHEST = lax.Precision.HIGHEST

D_MODEL = 1024
N_HEADS = 16
N_KV_HEADS = 4
HEAD_DIM = 64
GRID_W = 64
WINDOW = 128
ROPE_BASE = 10000.0
D_HYENA = 1024
N_BANDS = 16
FILTER_EMB = 1 + 2 * N_BANDS
FILTER_WIDTH = 64
D_FF = 2816
N_MOD = 9
EPS = 1e-6
NEG_INF = -1e30
LOG2E = math.log2(math.e)

QKV_COLS = 1536
QKV_K = 1024
QKV_V = 1280
GH_COLS = 5120
GH_GA = 0
GH_GH = 1024
GH_X1 = 2048
GH_X2 = 3072
GH_HV = 4096
SUB = 256

LANES = 128
VMEM_LIMIT = 61 * 1024 * 1024


def _cparams(sem):
    return pltpu.CompilerParams(dimension_semantics=sem, vmem_limit_bytes=VMEM_LIMIT)


def _resident(shape, index_map):
    return pl.BlockSpec(shape, index_map, pipeline_mode=pl.Buffered(1))


def _mod_kernel(c_ref, w_ref, b_ref, o_ref):
    c = c_ref[...]
    s = c * jax.nn.sigmoid(c)
    o_ref[...] = jnp.dot(s, w_ref[...], precision=HIGHEST, preferred_element_type=F32) + b_ref[...]


def _modulation(c_rows, w_mod, b_mod):
    rows = c_rows.shape[0]
    n_out = w_mod.shape[1]
    tn = 1024
    return pl.pallas_call(
        _mod_kernel,
        grid=(n_out // tn,),
        in_specs=[
            pl.BlockSpec((rows, D_MODEL), lambda j: (0, 0)),
            pl.BlockSpec((D_MODEL, tn), lambda j: (0, j)),
            pl.BlockSpec((1, tn), lambda j: (0, j)),
        ],
        out_specs=pl.BlockSpec((rows, tn), lambda j: (0, j)),
        out_shape=jax.ShapeDtypeStruct((rows, n_out), F32),
        compiler_params=_cparams(("arbitrary",)),
        name="modulation",
    )(c_rows, w_mod, b_mod.reshape(1, n_out))


def _norm_modulate(x, g, shift, scale):
    ms = jnp.mean(x * x, axis=-1, keepdims=True)
    y = x * lax.rsqrt(ms + EPS) * g
    return y * (1.0 + scale) + shift


def _ffn_kernel(x_ref, mod_ref, g_ref, wg_ref, wu_ref, wo_ref, o_ref, h_scr, acc_scr, *, mod_base):
    j = pl.program_id(1)

    @pl.when(j == 0)
    def _():
        shift = mod_ref[0, mod_base:mod_base + 1, :]
        scale = mod_ref[0, mod_base + 1:mod_base + 2, :]
        h_scr[...] = _norm_modulate(x_ref[...], g_ref[...], shift, scale).astype(BF16)
        acc_scr[...] = jnp.zeros_like(acc_scr)

    h = h_scr[...]
    gate = jnp.dot(h, wg_ref[...], preferred_element_type=F32)
    up = jnp.dot(h, wu_ref[...], preferred_element_type=F32)
    a = (gate * jax.nn.sigmoid(gate) * up).astype(BF16)
    acc_scr[...] += jnp.dot(a, wo_ref[...], preferred_element_type=F32)

    @pl.when(j == pl.num_programs(1) - 1)
    def _():
        gmod = mod_ref[0, mod_base + 2:mod_base + 3, :]
        o_ref[...] = x_ref[...] + (0.5 * gmod) * acc_scr[...]


def _ffn(x, mod3, tiles_per_mod, g, wi, wo, mod_base, tm, tf):
    t = x.shape[0]
    nj = D_FF // tf
    mod_map = (lambda i, j: (i // tiles_per_mod, 0, 0)) if tiles_per_mod else (lambda i, j: (0, 0, 0))
    return pl.pallas_call(
        functools.partial(_ffn_kernel, mod_base=mod_base),
        grid=(t // tm, nj),
        in_specs=[
            pl.BlockSpec((tm, D_MODEL), lambda i, j: (i, 0)),
            pl.BlockSpec((1, N_MOD, D_MODEL), mod_map),
            pl.BlockSpec((1, D_MODEL), lambda i, j: (0, 0)),
            pl.BlockSpec((D_MODEL, tf), lambda i, j: (0, j)),
            pl.BlockSpec((D_MODEL, tf), lambda i, j: (0, j + nj)),
            pl.BlockSpec((tf, D_MODEL), lambda i, j: (j, 0)),
        ],
        out_specs=pl.BlockSpec((tm, D_MODEL), lambda i, j: (i, 0)),
        out_shape=jax.ShapeDtypeStruct((t, D_MODEL), F32),
        scratch_shapes=[pltpu.VMEM((tm, D_MODEL), BF16), pltpu.VMEM((tm, D_MODEL), F32)],
        compiler_params=_cparams(("parallel", "arbitrary")),
        name="ffn",
    )(x, mod3, g.reshape(1, D_MODEL), wi, wi, wo)


def _rope_partner(y):
    lane = lax.broadcasted_iota(jnp.int32, (1, LANES), 1)
    first = (lane % 32) < 16
    parts = []
    for c in range(y.shape[1] // LANES):
        yc = y[:, c * LANES:(c + 1) * LANES]
        fwd = pltpu.roll(yc, LANES - 16, axis=1)
        bwd = pltpu.roll(yc, 16, axis=1)
        parts.append(jnp.where(first, fwd, bwd))
    return jnp.concatenate(parts, axis=1)


def _place_halves(x, own_half):
    lane = lax.broadcasted_iota(jnp.int32, (1, LANES), 1)
    in_half = [lane < 64, lane >= 64]
    swapped = pltpu.roll(x, 64, axis=1)
    out = [None, None]
    out[own_half] = jnp.where(in_half[own_half], x, 0.0).astype(BF16)
    out[1 - own_half] = jnp.where(in_half[1 - own_half], swapped, 0.0).astype(BF16)
    return out


def _store_placed_keys(k, dst_ref):
    for g in range(N_KV_HEADS):
        halves = _place_halves(k[:, (g // 2) * LANES:(g // 2 + 1) * LANES], g % 2)
        for a in range(2):
            dst_ref[:, (2 * g + a) * LANES:(2 * g + a + 1) * LANES] = halves[a]


def _qkv_kernel(*refs, rope, attn_layout):
    refs = list(refs)
    x_ref, mod_ref, g_ref, w_ref, qg_ref, kg_ref, bd_ref = refs[:7]
    refs = refs[7:]
    if rope:
        cos_ref, sin_ref = refs[:2]
        refs = refs[2:]
    if attn_layout:
        q_ref, kp_ref, vt_ref, h_ref = refs
    else:
        o_ref, h_ref = refs
    j = pl.program_id(1)

    @pl.when(j == 0)
    def _():
        shift = mod_ref[0, 3:4, :]
        scale = mod_ref[0, 4:5, :]
        h_ref[...] = _norm_modulate(x_ref[...], g_ref[...], shift, scale).astype(BF16)

    def head_norm(t, gain):
        ss = jnp.dot((t * t).astype(BF16), bd_ref[...], preferred_element_type=F32)
        y = t * lax.rsqrt(ss * (1.0 / HEAD_DIM) + EPS) * gain
        if rope:
            y = y * cos_ref[...] + _rope_partner(y) * sin_ref[...]
        return y

    @pl.when(j < QKV_K // (2 * SUB))
    def _():
        acc = jnp.dot(h_ref[...], w_ref[...], preferred_element_type=F32)
        gain = qg_ref[...] * (LOG2E / math.sqrt(HEAD_DIM))
        for c in range(2):
            cols = slice(c * SUB, (c + 1) * SUB)
            y = head_norm(acc[:, cols], gain)
            if attn_layout:
                q_ref[:, cols] = y.astype(BF16)
            else:
                o_ref[:, cols] = y

    @pl.when(j == QKV_K // (2 * SUB))
    def _():
        acc = jnp.dot(h_ref[...], w_ref[...], preferred_element_type=F32)
        k = head_norm(acc[:, 0:SUB], kg_ref[...])
        v = acc[:, SUB:2 * SUB]
        if attn_layout:
            _store_placed_keys(k, kp_ref)
            vt_ref[...] = v.T.astype(BF16)
        else:
            o_ref[:, 0:SUB] = k
            o_ref[:, SUB:2 * SUB] = v


def _qkv_proj(x, mod3, tiles_per_mod, g, w, qg, kg, bd, rope_tabs, tm, attn_layout):
    t = x.shape[0]
    tn = 2 * SUB
    n_q_tiles = QKV_K // tn
    rope = rope_tabs is not None
    mod_map = (lambda i, j: (i // tiles_per_mod, 0, 0)) if tiles_per_mod else (lambda i, j: (0, 0, 0))
    in_specs = [
        pl.BlockSpec((tm, D_MODEL), lambda i, j: (i, 0)),
        pl.BlockSpec((1, N_MOD, D_MODEL), mod_map),
        pl.BlockSpec((1, D_MODEL), lambda i, j: (0, 0)),
        pl.BlockSpec((D_MODEL, tn), lambda i, j: (0, j)),
        pl.BlockSpec((1, SUB), lambda i, j: (0, 0)),
        pl.BlockSpec((1, SUB), lambda i, j: (0, 0)),
        pl.BlockSpec((SUB, SUB), lambda i, j: (0, 0)),
    ]
    args = [x, mod3, g.reshape(1, D_MODEL), w, qg, kg, bd]
    if rope:
        seq_tiles = rope_tabs[0].shape[0] // tm
        for tab in rope_tabs:
            in_specs.append(pl.BlockSpec((tm, SUB), lambda i, j: (i % seq_tiles, 0)))
            args.append(tab)
    h_spec = pl.BlockSpec((tm, D_MODEL), lambda i, j: (i, 0))
    h_shape = jax.ShapeDtypeStruct((t, D_MODEL), BF16)
    if attn_layout:
        out_specs = [pl.BlockSpec((tm, tn), lambda i, j: (i, jnp.minimum(j, n_q_tiles - 1))),
                     pl.BlockSpec((tm, 2 * N_KV_HEADS * LANES), lambda i, j: (i, 0)),
                     pl.BlockSpec((SUB, tm), lambda i, j: (0, i)),
                     h_spec]
        out_shape = [jax.ShapeDtypeStruct((t, QKV_K), BF16),
                     jax.ShapeDtypeStruct((t, 2 * N_KV_HEADS * LANES), BF16),
                     jax.ShapeDtypeStruct((SUB, t), BF16),
                     h_shape]
    else:
        out_specs = [pl.BlockSpec((tm, tn), lambda i, j: (i, j)), h_spec]
        out_shape = [jax.ShapeDtypeStruct((t, QKV_COLS), F32), h_shape]
    return pl.pallas_call(
        functools.partial(_qkv_kernel, rope=rope, attn_layout=attn_layout),
        grid=(t // tm, QKV_COLS // tn),
        in_specs=in_specs,
        out_specs=out_specs,
        out_shape=out_shape,
        compiler_params=_cparams(("parallel", "arbitrary")),
        name="qkv_proj",
    )(*args)


def _gh_kernel(h_ref, w_ref, o_ref, *, gate_tiles):
    j = pl.program_id(1)

    @pl.when(j < gate_tiles)
    def _():
        o_ref[...] = jax.nn.sigmoid(jnp.dot(h_ref[...], w_ref[...], preferred_element_type=F32))

    @pl.when(j >= gate_tiles)
    def _():
        o_ref[...] = jnp.dot(h_ref[...], w_ref[...], preferred_element_type=F32)


def _gh_proj(h, w, tm, tn):
    t = h.shape[0]
    return pl.pallas_call(
        functools.partial(_gh_kernel, gate_tiles=GH_X1 // tn),
        grid=(t // tm, GH_COLS // tn),
        in_specs=[
            pl.BlockSpec((tm, D_MODEL), lambda i, j: (i, 0)),
            pl.BlockSpec((D_MODEL, tn), lambda i, j: (0, j)),
        ],
        out_specs=pl.BlockSpec((tm, tn), lambda i, j: (i, j)),
        out_shape=jax.ShapeDtypeStruct((t, GH_COLS), F32),
        compiler_params=_cparams(("parallel", "arbitrary")),
        name="gate_hyena_proj",
    )(h, w)


def _col_reduce(x, op):
    rows, cols = x.shape
    part = op(x.reshape(8, rows // 8, cols), axis=0)
    return op(part, axis=0, keepdims=True)


def _attn_kernel(*refs, window, tq, n_qblocks):
    if window:
        (sink_ref, q_ref, kc_ref, vc_ref, kp_ref, kcur_ref, kn_ref,
         vp_ref, vcur_ref, vn_ref, bias_ref, o_ref, kcp_scr, vct_scr) = refs
    else:
        sink_ref, q_ref, kc_ref, vc_ref, o_ref, kcp_scr, vct_scr = refs
    i = pl.program_id(1)

    @pl.when(i == 0)
    def _():
        _store_placed_keys(kc_ref[...], kcp_scr)
        vct_scr[...] = vc_ref[...].T.astype(BF16)

    if window:
        krow = lax.broadcasted_iota(jnp.int32, (3 * WINDOW, 2 * tq), 0)
        edge = (jnp.where((krow < WINDOW) & (i == 0), NEG_INF, 0.0)
                + jnp.where((krow >= 2 * WINDOW) & (i == n_qblocks - 1), NEG_INF, 0.0))
        bias = bias_ref[...] + edge
    lane = lax.broadcasted_iota(jnp.int32, (1, 2 * tq), 1)
    dn = (((1,), (1,)), ((), ()))
    nkc = kcp_scr.shape[0]
    n_units = 2 * N_KV_HEADS

    def scores(u):
        g, a = divmod(u, 2)
        q0 = q_ref[:, g * 2 * LANES:g * 2 * LANES + LANES]
        q1 = q_ref[:, g * 2 * LANES + LANES:(g + 1) * 2 * LANES]
        qg = jnp.concatenate([q0, q1], axis=0).astype(BF16)
        kcols = slice(u * LANES, (u + 1) * LANES)
        keys = [kcp_scr[:, kcols]]
        if window:
            keys += [kp_ref[:, kcols], kcur_ref[:, kcols], kn_ref[:, kcols]]
        return lax.dot_general(jnp.concatenate(keys, axis=0), qg, dn, preferred_element_type=F32)

    def softmax(u, s):
        g, a = divmod(u, 2)
        sink = jnp.where(lane < tq, sink_ref[4 * g + a], sink_ref[4 * g + 2 + a]) * LOG2E
        s_c = s[0:nkc]
        m = jnp.maximum(_col_reduce(s_c, jnp.max), sink)
        if window:
            s_w = s[nkc:] + bias
            m = jnp.maximum(m, _col_reduce(s_w, jnp.max))
        p_c = jnp.exp2(s_c - m)
        l = _col_reduce(p_c, jnp.sum) + jnp.exp2(sink - m)
        probs = [p_c.astype(BF16)]
        if window:
            p_w = jnp.exp2(s_w - m)
            l = l + _col_reduce(p_w, jnp.sum)
            probs.append(p_w.astype(BF16))
        return probs, 1.0 / l

    def values(u, probs, inv_l):
        g = u // 2
        vrows = slice(g * HEAD_DIM, (g + 1) * HEAD_DIM)
        o = jnp.dot(vct_scr[vrows, :], probs[0], preferred_element_type=F32)
        if window:
            vw_t = jnp.concatenate([vp_ref[vrows, :], vcur_ref[vrows, :], vn_ref[vrows, :]], axis=1)
            o = o + jnp.dot(vw_t, probs[1], preferred_element_type=F32)
        return o * inv_l

    outs = []
    s_q = {0: scores(0), 1: scores(1)}
    p_q = {}
    for step in range(n_units + 1):
        if step + 2 < n_units:
            s_q[step + 2] = scores(step + 2)
        if 0 <= step < n_units:
            p_q[step] = softmax(step, s_q.pop(step))
        u = step - 1
        if 0 <= u < n_units:
            outs.append(values(u, *p_q.pop(u)))
            if u % 2 == 1:
                g = u // 2
                o_t = jnp.concatenate(outs[-2:], axis=0)
                o_ref[:, g * 2 * LANES:g * 2 * LANES + LANES] = o_t[:, 0:tq].T
                o_ref[:, g * 2 * LANES + LANES:(g + 1) * 2 * LANES] = o_t[:, tq:2 * tq].T


def _window_bias_t(tq):
    a = np.arange(WINDOW)[None, :]
    b = np.arange(WINDOW)[:, None]
    prev = np.where(b >= a, 0.0, NEG_INF)
    cur = np.zeros((WINDOW, WINDOW))
    nxt = np.where(b <= a, 0.0, NEG_INF)
    one = np.concatenate([prev, cur, nxt], axis=0).astype(np.float32)
    return jnp.asarray(np.concatenate([one, one], axis=1))


def _attention_ctx(qkv, sink, n_batch, seq):
    t = qkv.shape[0]
    tq = seq
    kblk = QKV_K // SUB
    vblk = QKV_V // SUB
    return pl.pallas_call(
        functools.partial(_attn_kernel, window=False, tq=tq, n_qblocks=1),
        grid=(n_batch, 1),
        in_specs=[
            pl.BlockSpec(memory_space=pltpu.SMEM),
            pl.BlockSpec((tq, D_MODEL), lambda b, i: (b, 0)),
            pl.BlockSpec((seq, SUB), lambda b, i: (b, kblk)),
            pl.BlockSpec((seq, SUB), lambda b, i: (b, vblk)),
        ],
        out_specs=pl.BlockSpec((tq, D_MODEL), lambda b, i: (b, 0)),
        out_shape=jax.ShapeDtypeStruct((t, D_MODEL), F32),
        scratch_shapes=[pltpu.VMEM((seq, 2 * N_KV_HEADS * LANES), BF16), pltpu.VMEM((SUB, seq), BF16)],
        compiler_params=_cparams(("parallel", "arbitrary")),
        name="attn_ctx",
    )(sink, qkv, qkv, qkv)


def _attention_lat(q, kp, vt, ck, cv, sink, n_batch, seq, past):
    t = q.shape[0]
    tq = WINDOW
    nqb = seq // tq
    kwidth = 2 * N_KV_HEADS * LANES

    def kblock(delta):
        return pl.BlockSpec((tq, kwidth), lambda b, i: (b * nqb + jnp.clip(i + delta, 0, nqb - 1), 0))

    def vblock(delta):
        return pl.BlockSpec((SUB, tq), lambda b, i: (0, b * nqb + jnp.clip(i + delta, 0, nqb - 1)))

    return pl.pallas_call(
        functools.partial(_attn_kernel, window=True, tq=tq, n_qblocks=nqb),
        grid=(n_batch, nqb),
        in_specs=[
            pl.BlockSpec(memory_space=pltpu.SMEM),
            pl.BlockSpec((tq, D_MODEL), lambda b, i: (b * nqb + i, 0)),
            pl.BlockSpec((past, SUB), lambda b, i: (b, 0)),
            pl.BlockSpec((past, SUB), lambda b, i: (b, 0)),
            kblock(-1), kblock(0), kblock(1),
            vblock(-1), vblock(0), vblock(1),
            pl.BlockSpec((3 * WINDOW, 2 * tq), lambda b, i: (0, 0)),
        ],
        out_specs=pl.BlockSpec((tq, D_MODEL), lambda b, i: (b * nqb + i, 0)),
        out_shape=jax.ShapeDtypeStruct((t, D_MODEL), F32),
        scratch_shapes=[pltpu.VMEM((past, kwidth), BF16), pltpu.VMEM((SUB, past), BF16)],
        compiler_params=_cparams(("parallel", "arbitrary")),
        name="attn_lat",
    )(sink, q, ck, cv, kp, kp, kp, vt, vt, vt, _window_bias_t(tq))


def _fgen_kernel(z_ref, w1_ref, b1_ref, w2_ref, b2_ref, fr_ref,
                 w3b_ref, b3b_ref, dcb_ref, w3f_ref, b3f_ref, dcf_ref, o_ref, a2_scr, *, n):
    @pl.when((pl.program_id(0) == 0) & (pl.program_id(1) == 0))
    def _():
        fr = fr_ref[...]
        a1 = jnp.sin(fr * (jnp.dot(z_ref[...], w1_ref[...], precision=HIGHEST,
                                   preferred_element_type=F32) + b1_ref[...]))
        a2_scr[...] = jnp.sin(fr * (jnp.dot(a1, w2_ref[...], precision=HIGHEST,
                                            preferred_element_type=F32) + b2_ref[...]))

    tc = o_ref.shape[1]
    tb = jnp.broadcast_to(z_ref[0:n, 0:1], (n, tc))
    tf = jnp.broadcast_to(z_ref[n:2 * n, 0:1], (n, tc))
    hb = (jnp.dot(a2_scr[0:n, :], w3b_ref[...], precision=HIGHEST, preferred_element_type=F32)
          + b3b_ref[...]) * jnp.exp(-tb * jnp.abs(dcb_ref[...]))
    hf = (jnp.dot(a2_scr[n:2 * n, :], w3f_ref[...], precision=HIGHEST, preferred_element_type=F32)
          + b3f_ref[...]) * jnp.exp(-tf * jnp.abs(dcf_ref[...]))
    tot = (jnp.sum(jnp.abs(hb), axis=0, keepdims=True)
           + jnp.sum(jnp.abs(hf), axis=0, keepdims=True))
    inv = 1.0 / (tot + EPS)
    rowid = lax.broadcasted_iota(jnp.int32, (n, tc), 0)
    o_ref[0:n, :] = jnp.where(rowid == 0, 0.0, hb * inv)
    o_ref[n:2 * n, :] = hf * inv


def _filter_gen(n, filt_w1, filt_b1, filt_w2, filt_b2, filt_w3, filt_b3, filt_freq, filt_decay, tc):
    t = jnp.arange(n, dtype=F32) / max(n - 1, 1)
    bands = jnp.arange(1, N_BANDS + 1, dtype=F32)
    ang = 2.0 * math.pi * t[:, None] * bands[None, :]
    z = jnp.concatenate([t[:, None], jnp.cos(ang), jnp.sin(ang)], axis=-1)
    zb = jnp.concatenate([z[0:1], z[1:][::-1]], axis=0)
    zfull = jnp.pad(jnp.concatenate([zb, z], axis=0), ((0, 0), (0, LANES - FILTER_EMB)))
    padw = LANES - FILTER_WIDTH
    w1 = jnp.pad(filt_w1.astype(F32), ((0, LANES - FILTER_EMB), (0, padw)))
    b1 = jnp.pad(filt_b1.astype(F32), (0, padw)).reshape(1, LANES)
    w2 = jnp.pad(filt_w2.astype(F32), ((0, padw), (0, padw)))
    b2 = jnp.pad(filt_b2.astype(F32), (0, padw)).reshape(1, LANES)
    fr = jnp.pad(filt_freq.astype(F32), (0, padw)).reshape(1, LANES)
    w3 = jnp.pad(filt_w3.astype(F32), ((0, padw), (0, 0)))
    ncol = w3.shape[1]
    b3 = filt_b3.astype(F32).reshape(1, ncol)
    dc = filt_decay.astype(F32).reshape(1, ncol)
    ct = D_HYENA // tc
    full = lambda shape: pl.BlockSpec(shape, lambda o, c: (0, 0))
    bwd = lambda rows: pl.BlockSpec((rows, tc), lambda o, c: (0, (2 * o + 1) * ct + c))
    fwd = lambda rows: pl.BlockSpec((rows, tc), lambda o, c: (0, (2 * o) * ct + c))
    return pl.pallas_call(
        functools.partial(_fgen_kernel, n=n),
        grid=(2, ct),
        in_specs=[
            full((2 * n, LANES)), full((LANES, LANES)), full((1, LANES)),
            full((LANES, LANES)), full((1, LANES)), full((1, LANES)),
            bwd(LANES), bwd(1), bwd(1), fwd(LANES), fwd(1), fwd(1),
        ],
        out_specs=pl.BlockSpec((2 * n, tc), lambda o, c: (0, o * ct + c)),
        out_shape=jax.ShapeDtypeStruct((2 * n, 2 * D_HYENA), F32),
        scratch_shapes=[pltpu.VMEM((2 * n, LANES), F32)],
        compiler_params=_cparams(("arbitrary", "arbitrary")),
        name="hyena_filter_gen",
    )(zfull, w1, b1, w2, b2, fr, w3, b3, dc, w3, b3, dc)


def _dft_mats(blk):
    f = np.arange(blk, dtype=np.int64)[:, None]
    s = np.arange(blk, dtype=np.int64)[None, :]
    theta = (np.pi / (2 * blk)) * (((2 * f + 1) * s) % (4 * blk)).astype(np.float64)
    fwd = np.concatenate([np.cos(theta), -np.sin(theta)], axis=0)
    inv = np.concatenate([np.cos(theta).T, -np.sin(theta).T], axis=1) / blk
    return (jnp.asarray(fwd, dtype=F32).astype(BF16), jnp.asarray(inv, dtype=F32).astype(BF16))


def _ftf_kernel(k_ref, skip_ref, f_ref, o_ref, *, nblk2, blk):
    tc = o_ref.shape[2]
    fmat = f_ref[...]
    fidx = lax.broadcasted_iota(jnp.int32, (blk, tc), 0)
    sgn = jnp.where(fidx % 2 == 0, 1.0, -1.0)
    prev = None
    for e in range(nblk2):
        p = jnp.dot(fmat, k_ref[e * blk:(e + 1) * blk, :].astype(BF16), preferred_element_type=F32)
        if e >= 1:
            re = p[0:blk] - sgn * prev[blk:2 * blk]
            im = p[blk:2 * blk] + sgn * prev[0:blk]
            if e == nblk2 // 2:
                re = re + skip_ref[...]
            o_ref[e - 1, 0:blk, :] = re
            o_ref[e - 1, blk:2 * blk, :] = im
        prev = p


def _filter_transform(kfull, skip, fmat, n, blk, tc):
    nblk2 = 2 * n // blk
    nd = nblk2 - 1
    ct = D_HYENA // tc
    return pl.pallas_call(
        functools.partial(_ftf_kernel, nblk2=nblk2, blk=blk),
        grid=(2, ct),
        in_specs=[
            pl.BlockSpec((2 * n, tc), lambda o, c: (0, o * ct + c)),
            pl.BlockSpec((None, 1, tc), lambda o, c: (o, 0, c)),
            pl.BlockSpec((2 * blk, blk), lambda o, c: (0, 0)),
        ],
        out_specs=pl.BlockSpec((nd, 2 * blk, tc), lambda o, c: (o, 0, c)),
        out_shape=jax.ShapeDtypeStruct((2 * nd, 2 * blk, D_HYENA), F32),
        compiler_params=_cparams(("parallel", "parallel")),
        name="hyena_filter_transform",
    )(kfull, skip, fmat)


CHUNK = 32
HALO = 8


def _short_conv_block(src_ref, b, j, n, blk, w_ref, b_ref):
    nblk = n // blk
    lo = max(j * blk - HALO, 0)
    hi = min((j + 1) * blk + HALO, n)
    off = j * blk - lo
    win = src_ref[b, lo:hi, :]
    prev = pltpu.roll(win, 1, axis=0)[off:off + blk]
    nxt = pltpu.roll(win, hi - lo - 1, axis=0)[off:off + blk]
    u = win[off:off + blk]
    rowid = lax.broadcasted_iota(jnp.int32, u.shape, 0)
    if j == 0:
        prev = jnp.where(rowid == 0, 0.0, prev)
    if j == nblk - 1:
        nxt = jnp.where(rowid == blk - 1, 0.0, nxt)
    return prev * w_ref[0:1, :] + u * w_ref[1:2, :] + nxt * w_ref[2:3, :] + b_ref[...]


def _conv_kernel(zin_ref, gin_ref, wz_ref, bz_ref, wg_ref, bg_ref, g_ref, f_ref, fi_ref,
                 o_ref, zf_scr, yf_scr, *, n, blk, bb, conv_z):
    nblk = n // blk
    tc = o_ref.shape[2]
    fmat = f_ref[...]
    for j in range(nblk):
        rows = slice(j * blk, (j + 1) * blk)
        zs = []
        for b in range(bb):
            if conv_z:
                z = _short_conv_block(zin_ref, b, j, n, blk, wz_ref, bz_ref)
            else:
                z = zin_ref[b, rows, :]
            zs.append(z.astype(BF16))
        zf_scr[j] = jnp.dot(fmat, jnp.concatenate(zs, axis=1), preferred_element_type=F32)
        for b in range(bb):
            o_ref[b, rows, :] = _short_conv_block(gin_ref, b, j, n, blk, wg_ref, bg_ref)

    def pairs(i, slot):
        for r in range(blk // CHUNK):
            re_rows = slice(r * CHUNK, (r + 1) * CHUNK)
            im_rows = slice(blk + r * CHUNK, blk + (r + 1) * CHUNK)
            acc_re = [jnp.zeros((CHUNK, tc), F32) for _ in range(bb)]
            acc_im = [jnp.zeros((CHUNK, tc), F32) for _ in range(bb)]
            for j in range(nblk):
                d = i - j + (nblk - 1)
                g_re = g_ref[d, re_rows, :]
                g_im = g_ref[d, im_rows, :]
                for b in range(bb):
                    cols = slice(b * tc, (b + 1) * tc)
                    z_re = zf_scr[j, re_rows, cols]
                    z_im = zf_scr[j, im_rows, cols]
                    acc_re[b] = acc_re[b] + (g_re * z_re - g_im * z_im)
                    acc_im[b] = acc_im[b] + (g_re * z_im + g_im * z_re)
            for b in range(bb):
                cols = slice(b * tc, (b + 1) * tc)
                yf_scr[slot, re_rows, cols] = acc_re[b].astype(BF16)
                yf_scr[slot, im_rows, cols] = acc_im[b].astype(BF16)

    def inverse(slot):
        return jnp.dot(fi_ref[...], yf_scr[slot], preferred_element_type=F32)

    def gate_out(i, y):
        rows = pl.ds(pl.multiple_of(i * blk, blk), blk)
        for b in range(bb):
            o_ref[b, rows, :] = o_ref[b, rows, :] * y[:, b * tc:(b + 1) * tc]

    pairs(0, 0)
    if nblk > 1:
        def body(i, carry):
            y = inverse((i - 1) & 1)
            pairs(i, i & 1)
            gate_out(i - 1, y)
            return carry
        lax.fori_loop(1, nblk, body, 0)
    gate_out(nblk - 1, inverse((nblk - 1) & 1))


def _long_conv(zsrc, zcol, gsrc, gcol, conv_w, conv_b, gspec, order, fmat, fimat,
               n_batch, n, blk, tc, bb, conv_z):
    nblk = n // blk
    nd = 2 * nblk - 1
    ct = D_HYENA // tc
    hy0 = GH_X1 // tc
    wz = (zcol - hy0) if conv_z else 0
    wg = gcol - hy0
    return pl.pallas_call(
        functools.partial(_conv_kernel, n=n, blk=blk, bb=bb, conv_z=conv_z),
        grid=(ct, n_batch // bb),
        in_specs=[
            pl.BlockSpec((bb, n, tc), lambda c, b: (b, 0, zcol + c)),
            pl.BlockSpec((bb, n, tc), lambda c, b: (b, 0, gcol + c)),
            pl.BlockSpec((3, tc), lambda c, b: (0, wz + c)),
            pl.BlockSpec((1, tc), lambda c, b: (0, wz + c)),
            pl.BlockSpec((3, tc), lambda c, b: (0, wg + c)),
            pl.BlockSpec((1, tc), lambda c, b: (0, wg + c)),
            _resident((nd, 2 * blk, tc), lambda c, b: (order, 0, c)),
            _resident((2 * blk, blk), lambda c, b: (0, 0)),
            _resident((blk, 2 * blk), lambda c, b: (0, 0)),
        ],
        out_specs=pl.BlockSpec((bb, n, tc), lambda c, b: (b, 0, c)),
        out_shape=jax.ShapeDtypeStruct((n_batch, n, D_HYENA), F32),
        scratch_shapes=[
            pltpu.VMEM((nblk, 2 * blk, bb * tc), F32),
            pltpu.VMEM((2, 2 * blk, bb * tc), BF16),
        ],
        compiler_params=_cparams(("parallel", "arbitrary")),
        name="hyena_conv",
    )(zsrc, gsrc, conv_w, conv_b, conv_w, conv_b, gspec, fmat, fimat)


def _hyena(gh, n_batch, n, blk, tc, bb, conv_w, conv_b, skip, filt):
    fmat, fimat = _dft_mats(blk)
    kfull = _filter_gen(n, *filt, tc=128)
    sk = skip.astype(F32).reshape(2, 1, D_HYENA)
    gspec = _filter_transform(kfull, sk, fmat, n, blk, tc=128)
    cw = conv_w.astype(F32)
    cb = conv_b.astype(F32).reshape(1, -1)
    gh3 = gh.reshape(n_batch, n, GH_COLS)
    z2 = _long_conv(gh3, GH_HV // tc, gh3, GH_X1 // tc, cw, cb, gspec, 0, fmat, fimat,
                    n_batch, n, blk, tc, bb, conv_z=True)
    y = _long_conv(z2, 0, gh3, GH_X2 // tc, cw, cb, gspec, 1, fmat, fimat,
                   n_batch, n, blk, tc, bb, conv_z=False)
    return y.reshape(n_batch * n, D_HYENA)


def _outproj_kernel(x_ref, attn_ref, hy_ref, ga_ref, gh_ref, mod_ref, wa_ref, wh_ref, wo_ref, o_ref):
    a = jnp.dot(attn_ref[...].astype(BF16), wa_ref[...], preferred_element_type=F32)
    h = jnp.dot(hy_ref[...].astype(BF16), wh_ref[...], preferred_element_type=F32)
    merged = ga_ref[...] * a + gh_ref[...] * h
    out = jnp.dot(merged.astype(BF16), wo_ref[...], preferred_element_type=F32)
    o_ref[...] = x_ref[...] + mod_ref[0, 5:6, :] * out


def _outproj(x, attn, hy, gh, mod3, tiles_per_mod, wa, wh, wo, tm):
    t = x.shape[0]
    mod_map = (lambda i: (i // tiles_per_mod, 0, 0)) if tiles_per_mod else (lambda i: (0, 0, 0))
    row = lambda i: (i, 0)
    wspec = _resident((D_MODEL, D_MODEL), lambda i: (0, 0))
    return pl.pallas_call(
        _outproj_kernel,
        grid=(t // tm,),
        in_specs=[
            pl.BlockSpec((tm, D_MODEL), row),
            pl.BlockSpec((tm, D_MODEL), row),
            pl.BlockSpec((tm, D_MODEL), row),
            pl.BlockSpec((tm, D_MODEL), lambda i: (i, GH_GA // D_MODEL)),
            pl.BlockSpec((tm, D_MODEL), lambda i: (i, GH_GH // D_MODEL)),
            pl.BlockSpec((1, N_MOD, D_MODEL), mod_map),
            wspec, wspec, wspec,
        ],
        out_specs=pl.BlockSpec((tm, D_MODEL), row),
        out_shape=jax.ShapeDtypeStruct((t, D_MODEL), F32),
        compiler_params=_cparams(("parallel",)),
        name="mixer_out",
    )(x, attn, hy, gh, gh, mod3, wa, wh, wo)


def _rope_tables(n):
    pos = jnp.arange(n)
    row = (pos // GRID_W).astype(F32)
    col = (pos % GRID_W).astype(F32)
    n_freq = HEAD_DIM // 4
    inv = ROPE_BASE ** (-jnp.arange(n_freq, dtype=F32) / n_freq)
    ar = row[:, None] * inv[None, :]
    ac = col[:, None] * inv[None, :]
    cos = jnp.concatenate([jnp.cos(ar), jnp.cos(ar), jnp.cos(ac), jnp.cos(ac)], axis=-1)
    sin = jnp.concatenate([-jnp.sin(ar), jnp.sin(ar), -jnp.sin(ac), jnp.sin(ac)], axis=-1)
    reps = SUB // HEAD_DIM
    return jnp.tile(cos, (1, reps)), jnp.tile(sin, (1, reps))


def _layer(x, mod3, per_batch_mod, n_batch, seq, w, ctx_kv, hy_blk, hy_tc, hy_bb):
    tm = 1024
    tiles_per_mod = (seq // tm) if per_batch_mod else 0
    tmf = 2048
    ffn_tiles_per_mod = (seq // tmf) if per_batch_mod else 0
    x = _ffn(x, mod3, ffn_tiles_per_mod, w["norm_ffn1"], w["ffn1_wi"], w["ffn1_wo"], 0, tmf, 256)
    qkv = None
    if ctx_kv is None:
        qkv, h = _qkv_proj(x, mod3, tiles_per_mod, w["norm_mix"], w["w_qkv"], w["qg"], w["kg"], w["bd"],
                           None, tm, attn_layout=False)
        attn = _attention_ctx(qkv, w["sink"], n_batch, seq)
    else:
        q, kp, vt, h = _qkv_proj(x, mod3, tiles_per_mod, w["norm_mix"], w["w_qkv"], w["qg"], w["kg"],
                                 w["bd"], _rope_tables(seq), tm, attn_layout=True)
        attn = _attention_lat(q, kp, vt, ctx_kv[0], ctx_kv[1], w["sink"], n_batch, seq,
                              ctx_kv[0].shape[0] // n_batch)
    gh = _gh_proj(h, w["w_gh"], tm, 1024)
    hy = _hyena(gh, n_batch, seq, hy_blk, hy_tc, hy_bb, w["conv_w"], w["conv_b"], w["hyena_skip"], w["filt"])
    tmo = 512
    x = _outproj(x, attn, hy, gh, mod3, (seq // tmo) if per_batch_mod else 0,
                 w["wa"], w["wh"], w["wo"], tmo)
    x = _ffn(x, mod3, ffn_tiles_per_mod, w["norm_ffn2"], w["ffn2_wi"], w["ffn2_wo"], 6, tmf, 256)
    return x, qkv


def kernel(x_prompt, x_sample, cache_k, cache_v, c, c_ctx, w_mod, b_mod, norm_ffn1, ffn1_wi, ffn1_wo, norm_mix, w_in, q_norm, k_norm, attn_sink, conv_w, conv_b, filt_w1, filt_b1, filt_w2, filt_b2, filt_w3, filt_b3, filt_freq, filt_decay, hyena_skip, w_attn_branch, w_hyena_branch, w_out, norm_ffn2, ffn2_wi, ffn2_wo):
    batch, seq, _ = x_prompt.shape
    dec_batch, dec_seq, _ = x_sample.shape
    depth = w_mod.shape[0]
    past = cache_k.shape[2]

    yp = x_prompt.reshape(batch * seq, D_MODEL)
    ys = x_sample.reshape(dec_batch * dec_seq, D_MODEL)
    bd = jnp.asarray(np.kron(np.eye(SUB // HEAD_DIM), np.ones((HEAD_DIM, HEAD_DIM))), dtype=F32).astype(BF16)
    new_ks, new_vs = [], []
    for l in range(depth):
        c_rows = jnp.zeros((16, D_MODEL), F32).at[0:dec_batch].set(c).at[dec_batch].set(c_ctx)
        mod = _modulation(c_rows, w_mod[l], b_mod[l]).reshape(16, N_MOD, D_MODEL)
        mod_lat = mod[0:dec_batch]
        mod_ctx = mod[dec_batch:dec_batch + 1]
        wl = w_in[l]
        hy_end = QKV_COLS + 3 * D_HYENA
        w = {
            "norm_ffn1": norm_ffn1[l], "ffn1_wi": ffn1_wi[l].astype(BF16), "ffn1_wo": ffn1_wo[l].astype(BF16),
            "norm_mix": norm_mix[l],
            "w_qkv": wl[:, 0:QKV_COLS].astype(BF16),
            "w_gh": jnp.concatenate([wl[:, hy_end:], wl[:, QKV_COLS:hy_end]], axis=1).astype(BF16),
            "qg": jnp.tile(q_norm[l], SUB // HEAD_DIM).reshape(1, SUB),
            "kg": jnp.tile(k_norm[l], SUB // HEAD_DIM).reshape(1, SUB),
            "bd": bd, "sink": attn_sink[l],
            "conv_w": conv_w[l], "conv_b": conv_b[l], "hyena_skip": hyena_skip[l],
            "filt": (filt_w1[l], filt_b1[l], filt_w2[l], filt_b2[l], filt_w3[l], filt_b3[l],
                     filt_freq[l], filt_decay[l]),
            "wa": w_attn_branch[l].astype(BF16), "wh": w_hyena_branch[l].astype(BF16),
            "wo": w_out[l].astype(BF16),
            "norm_ffn2": norm_ffn2[l], "ffn2_wi": ffn2_wi[l].astype(BF16), "ffn2_wo": ffn2_wo[l].astype(BF16),
        }
        yp, qkv_p = _layer(yp, mod_ctx, False, batch, seq, w, None, hy_blk=256, hy_tc=128, hy_bb=8)
        ck = cache_k[:, l].reshape(dec_batch * past, N_KV_HEADS * HEAD_DIM)
        cv = cache_v[:, l].reshape(dec_batch * past, N_KV_HEADS * HEAD_DIM)
        ys, _ = _layer(ys, mod_lat, True, dec_batch, dec_seq, w, (ck, cv), hy_blk=512, hy_tc=128, hy_bb=2)
        new_ks.append(qkv_p[:, QKV_K:QKV_K + SUB].reshape(batch, seq, N_KV_HEADS, HEAD_DIM))
        new_vs.append(qkv_p[:, QKV_V:QKV_V + SUB].reshape(batch, seq, N_KV_HEADS, HEAD_DIM))
    new_k = jnp.stack(new_ks, axis=1)
    new_v = jnp.stack(new_vs, axis=1)
    return (yp.reshape(batch, seq, D_MODEL), ys.reshape(dec_batch, dec_seq, D_MODEL), new_k, new_v)
```

```python
import functools
import math

import numpy as np
import jax
import jax.numpy as jnp
from jax import lax
from jax.experimental import pallas as pl
from jax.experimental.pallas import tpu as pltpu

F32 = jnp.float32
BF16 = jnp.bfloat16
HIGHEST = lax.Precision.HIGHEST

D_MODEL = 1024
N_HEADS = 16
N_KV_HEADS = 4
HEAD_DIM = 64
GRID_W = 64
WINDOW = 128
ROPE_BASE = 10000.0
D_HYENA = 1024
N_BANDS = 16
FILTER_EMB = 1 + 2 * N_BANDS
FILTER_WIDTH = 64
D_FF = 2816
N_MOD = 9
EPS = 1e-6
NEG_INF = -1e30
LOG2E = math.log2(math.e)

QKV_COLS = 1536
QKV_K = 1024
QKV_V = 1280
GH_COLS = 5120
GH_GA = 0
GH_GH = 1024
GH_X1 = 2048
GH_X2 = 3072
GH_HV = 4096
SUB = 256

LANES = 128
VMEM_LIMIT = 61 * 1024 * 1024


def _cparams(sem):
    return pltpu.CompilerParams(dimension_semantics=sem, vmem_limit_bytes=VMEM_LIMIT)


def _resident(shape, index_map):
    return pl.BlockSpec(shape, index_map, pipeline_mode=pl.Buffered(1))


def _mod_kernel(c_ref, w_ref, b_ref, o_ref):
    c = c_ref[...]
    s = c * jax.nn.sigmoid(c)
    o_ref[...] = jnp.dot(s, w_ref[...], precision=HIGHEST, preferred_element_type=F32) + b_ref[...]


def _modulation(c_rows, w_mod, b_mod):
    rows = c_rows.shape[0]
    n_out = w_mod.shape[1]
    tn = 1024
    return pl.pallas_call(
        _mod_kernel,
        grid=(n_out // tn,),
        in_specs=[
            pl.BlockSpec((rows, D_MODEL), lambda j: (0, 0)),
            pl.BlockSpec((D_MODEL, tn), lambda j: (0, j)),
            pl.BlockSpec((1, tn), lambda j: (0, j)),
        ],
        out_specs=pl.BlockSpec((rows, tn), lambda j: (0, j)),
        out_shape=jax.ShapeDtypeStruct((rows, n_out), F32),
        compiler_params=_cparams(("arbitrary",)),
        name="modulation",
    )(c_rows, w_mod, b_mod.reshape(1, n_out))


def _norm_modulate(x, g, shift, scale):
    ms = jnp.mean(x * x, axis=-1, keepdims=True)
    y = x * lax.rsqrt(ms + EPS) * g
    return y * (1.0 + scale) + shift


def _ffn_kernel(x_ref, mod_ref, g_ref, wg_ref, wu_ref, wo_ref, o_ref, h_scr, acc_scr, *, mod_base):
    j = pl.program_id(1)

    @pl.when(j == 0)
    def _():
        shift = mod_ref[0, mod_base:mod_base + 1, :]
        scale = mod_ref[0, mod_base + 1:mod_base + 2, :]
        h_scr[...] = _norm_modulate(x_ref[...], g_ref[...], shift, scale).astype(BF16)
        acc_scr[...] = jnp.zeros_like(acc_scr)

    h = h_scr[...]
    gate = jnp.dot(h, wg_ref[...], preferred_element_type=F32)
    up = jnp.dot(h, wu_ref[...], preferred_element_type=F32)
    a = (gate * jax.nn.sigmoid(gate) * up).astype(BF16)
    acc_scr[...] += jnp.dot(a, wo_ref[...], preferred_element_type=F32)

    @pl.when(j == pl.num_programs(1) - 1)
    def _():
        gmod = mod_ref[0, mod_base + 2:mod_base + 3, :]
        o_ref[...] = x_ref[...] + (0.5 * gmod) * acc_scr[...]


def _ffn(x, mod3, tiles_per_mod, g, wi, wo, mod_base, tm, tf):
    t = x.shape[0]
    nj = D_FF // tf
    mod_map = (lambda i, j: (i // tiles_per_mod, 0, 0)) if tiles_per_mod else (lambda i, j: (0, 0, 0))
    return pl.pallas_call(
        functools.partial(_ffn_kernel, mod_base=mod_base),
        grid=(t // tm, nj),
        in_specs=[
            pl.BlockSpec((tm, D_MODEL), lambda i, j: (i, 0)),
            pl.BlockSpec((1, N_MOD, D_MODEL), mod_map),
            pl.BlockSpec((1, D_MODEL), lambda i, j: (0, 0)),
            pl.BlockSpec((D_MODEL, tf), lambda i, j: (0, j)),
            pl.BlockSpec((D_MODEL, tf), lambda i, j: (0, j + nj)),
            pl.BlockSpec((tf, D_MODEL), lambda i, j: (j, 0)),
        ],
        out_specs=pl.BlockSpec((tm, D_MODEL), lambda i, j: (i, 0)),
        out_shape=jax.ShapeDtypeStruct((t, D_MODEL), F32),
        scratch_shapes=[pltpu.VMEM((tm, D_MODEL), BF16), pltpu.VMEM((tm, D_MODEL), F32)],
        compiler_params=_cparams(("parallel", "arbitrary")),
        name="ffn",
    )(x, mod3, g.reshape(1, D_MODEL), wi, wi, wo)


def _rope_partner(y):
    lane = lax.broadcasted_iota(jnp.int32, (1, LANES), 1)
    first = (lane % 32) < 16
    parts = []
    for c in range(y.shape[1] // LANES):
        yc = y[:, c * LANES:(c + 1) * LANES]
        fwd = pltpu.roll(yc, LANES - 16, axis=1)
        bwd = pltpu.roll(yc, 16, axis=1)
        parts.append(jnp.where(first, fwd, bwd))
    return jnp.concatenate(parts, axis=1)


def _place_halves(x, own_half):
    lane = lax.broadcasted_iota(jnp.int32, (1, LANES), 1)
    in_half = [lane < 64, lane >= 64]
    swapped = pltpu.roll(x, 64, axis=1)
    out = [None, None]
    out[own_half] = jnp.where(in_half[own_half], x, 0.0).astype(BF16)
    out[1 - own_half] = jnp.where(in_half[1 - own_half], swapped, 0.0).astype(BF16)
    return out


def _store_placed_keys(k, dst_ref):
    for g in range(N_KV_HEADS):
        halves = _place_halves(k[:, (g // 2) * LANES:(g // 2 + 1) * LANES], g % 2)
        for a in range(2):
            dst_ref[:, (2 * g + a) * LANES:(2 * g + a + 1) * LANES] = halves[a]


def _inproj_kernel(*refs, rope, attn_layout):
    refs = list(refs)
    x_ref, mod_ref, g_ref, wq_ref, wg_ref, qg_ref, kg_ref, bd_ref = refs[:8]
    refs = refs[8:]
    if rope:
        cos_ref, sin_ref = refs[:2]
        refs = refs[2:]
    if attn_layout:
        q_ref, kp_ref, vt_ref, gh_ref = refs
    else:
        qkv_ref, gh_ref = refs

    h = _norm_modulate(x_ref[...], g_ref[...], mod_ref[0, 3:4, :], mod_ref[0, 4:5, :]).astype(BF16)

    def head_norm(t, gain):
        ss = jnp.dot((t * t).astype(BF16), bd_ref[...], preferred_element_type=F32)
        y = t * lax.rsqrt(ss * (1.0 / HEAD_DIM) + EPS) * gain
        if rope:
            y = y * cos_ref[...] + _rope_partner(y) * sin_ref[...]
        return y

    def q_epilogue(acc, c0):
        gain = qg_ref[...] * (LOG2E / math.sqrt(HEAD_DIM))
        for c in range(acc.shape[1] // SUB):
            y = head_norm(acc[:, c * SUB:(c + 1) * SUB], gain)
            cols = slice(c0 + c * SUB, c0 + (c + 1) * SUB)
            if attn_layout:
                q_ref[:, cols] = y.astype(BF16)
            else:
                qkv_ref[:, cols] = y

    def kv_epilogue(acc, c0):
        k = head_norm(acc[:, 0:SUB], kg_ref[...])
        v = acc[:, SUB:2 * SUB]
        if attn_layout:
            _store_placed_keys(k, kp_ref)
            vt_ref[...] = v.T.astype(BF16)
        else:
            qkv_ref[:, QKV_K:QKV_K + SUB] = k
            qkv_ref[:, QKV_V:QKV_V + SUB] = v

    def gate_epilogue(acc, c0):
        gh_ref[:, c0:c0 + acc.shape[1]] = jax.nn.sigmoid(acc)

    def raw_epilogue(acc, c0):
        gh_ref[:, c0:c0 + acc.shape[1]] = acc

    half = QKV_K // 2
    chunks = [(wq_ref, 0, half, q_epilogue), (wq_ref, half, half, q_epilogue),
              (wq_ref, QKV_K, 2 * SUB, kv_epilogue)]
    for c0 in range(0, GH_COLS, D_MODEL):
        chunks.append((wg_ref, c0, D_MODEL, gate_epilogue if c0 < GH_X1 else raw_epilogue))

    def matmul(idx):
        w_ref, c0, width, _ = chunks[idx]
        return jnp.dot(h, w_ref[:, c0:c0 + width], preferred_element_type=F32)

    ahead = 2
    pending = {idx: matmul(idx) for idx in range(ahead)}
    for idx in range(len(chunks)):
        if idx + ahead < len(chunks):
            pending[idx + ahead] = matmul(idx + ahead)
        chunks[idx][3](pending.pop(idx), chunks[idx][1])


def _inproj(x, mod3, tiles_per_mod, g, wq, wg, qg, kg, bd, rope_tabs, tm, attn_layout):
    t = x.shape[0]
    rope = rope_tabs is not None
    mod_map = (lambda i: (i // tiles_per_mod, 0, 0)) if tiles_per_mod else (lambda i: (0, 0, 0))
    row = lambda i: (i, 0)
    const = lambda i: (0, 0)
    in_specs = [
        pl.BlockSpec((tm, D_MODEL), row),
        pl.BlockSpec((1, N_MOD, D_MODEL), mod_map),
        pl.BlockSpec((1, D_MODEL), const),
        _resident((D_MODEL, QKV_COLS), const),
        _resident((D_MODEL, GH_COLS), const),
        pl.BlockSpec((1, SUB), const),
        pl.BlockSpec((1, SUB), const),
        pl.BlockSpec((SUB, SUB), const),
    ]
    args = [x, mod3, g.reshape(1, D_MODEL), wq, wg, qg, kg, bd]
    if rope:
        seq_tiles = rope_tabs[0].shape[0] // tm
        for tab in rope_tabs:
            in_specs.append(pl.BlockSpec((tm, SUB), lambda i: (i % seq_tiles, 0)))
            args.append(tab)
    gh_spec = pl.BlockSpec((tm, GH_COLS), row)
    gh_shape = jax.ShapeDtypeStruct((t, GH_COLS), F32)
    if attn_layout:
        kwidth = 2 * N_KV_HEADS * LANES
        out_specs = [pl.BlockSpec((tm, QKV_K), row), pl.BlockSpec((tm, kwidth), row),
                     pl.BlockSpec((SUB, tm), lambda i: (0, i)), gh_spec]
        out_shape = [jax.ShapeDtypeStruct((t, QKV_K), BF16), jax.ShapeDtypeStruct((t, kwidth), BF16),
                     jax.ShapeDtypeStruct((SUB, t), BF16), gh_shape]
    else:
        out_specs = [pl.BlockSpec((tm, QKV_COLS), row), gh_spec]
        out_shape = [jax.ShapeDtypeStruct((t, QKV_COLS), F32), gh_shape]
    return pl.pallas_call(
        functools.partial(_inproj_kernel, rope=rope, attn_layout=attn_layout),
        grid=(t // tm,),
        in_specs=in_specs,
        out_specs=out_specs,
        out_shape=out_shape,
        compiler_params=_cparams(("parallel",)),
        name="inproj",
    )(*args)


def _col_reduce(x, op):
    rows, cols = x.shape
    part = op(x.reshape(8, rows // 8, cols), axis=0)
    return op(part, axis=0, keepdims=True)


def _attn_kernel(*refs, window, tq, n_qblocks):
    if window:
        (sink_ref, q_ref, kc_ref, vc_ref, kp_ref, kcur_ref, kn_ref,
         vp_ref, vcur_ref, vn_ref, bias_ref, o_ref, kcp_scr, vct_scr) = refs
    else:
        sink_ref, q_ref, kc_ref, vc_ref, o_ref, kcp_scr, vct_scr = refs
    i = pl.program_id(1)

    @pl.when(i == 0)
    def _():
        _store_placed_keys(kc_ref[...], kcp_scr)
        vct_scr[...] = vc_ref[...].T.astype(BF16)

    if window:
        krow = lax.broadcasted_iota(jnp.int32, (3 * WINDOW, 2 * tq), 0)
        edge = (jnp.where((krow < WINDOW) & (i == 0), NEG_INF, 0.0)
                + jnp.where((krow >= 2 * WINDOW) & (i == n_qblocks - 1), NEG_INF, 0.0))
        bias = bias_ref[...] + edge
    lane = lax.broadcasted_iota(jnp.int32, (1, 2 * tq), 1)
    dn = (((1,), (1,)), ((), ()))
    nkc = kcp_scr.shape[0]
    n_units = 2 * N_KV_HEADS

    def scores(u):
        g, a = divmod(u, 2)
        q0 = q_ref[:, g * 2 * LANES:g * 2 * LANES + LANES]
        q1 = q_ref[:, g * 2 * LANES + LANES:(g + 1) * 2 * LANES]
        qg = jnp.concatenate([q0, q1], axis=0).astype(BF16)
        kcols = slice(u * LANES, (u + 1) * LANES)
        keys = [kcp_scr[:, kcols]]
        if window:
            keys += [kp_ref[:, kcols], kcur_ref[:, kcols], kn_ref[:, kcols]]
        return lax.dot_general(jnp.concatenate(keys, axis=0), qg, dn, preferred_element_type=F32)

    def softmax(u, s):
        g, a = divmod(u, 2)
        sink = jnp.where(lane < tq, sink_ref[4 * g + a], sink_ref[4 * g + 2 + a]) * LOG2E
        s_c = s[0:nkc]
        m = jnp.maximum(_col_reduce(s_c, jnp.max), sink)
        if window:
            s_w = s[nkc:] + bias
            m = jnp.maximum(m, _col_reduce(s_w, jnp.max))
        p_c = jnp.exp2(s_c - m)
        l = _col_reduce(p_c, jnp.sum) + jnp.exp2(sink - m)
        probs = [p_c.astype(BF16)]
        if window:
            p_w = jnp.exp2(s_w - m)
            l = l + _col_reduce(p_w, jnp.sum)
            probs.append(p_w.astype(BF16))
        return probs, 1.0 / l

    def values(u, probs, inv_l):
        g = u // 2
        vrows = slice(g * HEAD_DIM, (g + 1) * HEAD_DIM)
        o = jnp.dot(vct_scr[vrows, :], probs[0], preferred_element_type=F32)
        if window:
            vw_t = jnp.concatenate([vp_ref[vrows, :], vcur_ref[vrows, :], vn_ref[vrows, :]], axis=1)
            o = o + jnp.dot(vw_t, probs[1], preferred_element_type=F32)
        return o * inv_l

    outs = []
    s_q = {0: scores(0), 1: scores(1)}
    p_q = {}
    for step in range(n_units + 1):
        if step + 2 < n_units:
            s_q[step + 2] = scores(step + 2)
        if 0 <= step < n_units:
            p_q[step] = softmax(step, s_q.pop(step))
        u = step - 1
        if 0 <= u < n_units:
            outs.append(values(u, *p_q.pop(u)))
            if u % 2 == 1:
                g = u // 2
                o_t = jnp.concatenate(outs[-2:], axis=0)
                o_ref[:, g * 2 * LANES:g * 2 * LANES + LANES] = o_t[:, 0:tq].T
                o_ref[:, g * 2 * LANES + LANES:(g + 1) * 2 * LANES] = o_t[:, tq:2 * tq].T


def _window_bias_t(tq):
    a = np.arange(WINDOW)[None, :]
    b = np.arange(WINDOW)[:, None]
    prev = np.where(b >= a, 0.0, NEG_INF)
    cur = np.zeros((WINDOW, WINDOW))
    nxt = np.where(b <= a, 0.0, NEG_INF)
    one = np.concatenate([prev, cur, nxt], axis=0).astype(np.float32)
    return jnp.asarray(np.concatenate([one, one], axis=1))


def _attention_ctx(qkv, sink, n_batch, seq):
    t = qkv.shape[0]
    tq = seq
    kblk = QKV_K // SUB
    vblk = QKV_V // SUB
    return pl.pallas_call(
        functools.partial(_attn_kernel, window=False, tq=tq, n_qblocks=1),
        grid=(n_batch, 1),
        in_specs=[
            pl.BlockSpec(memory_space=pltpu.SMEM),
            pl.BlockSpec((tq, D_MODEL), lambda b, i: (b, 0)),
            pl.BlockSpec((seq, SUB), lambda b, i: (b, kblk)),
            pl.BlockSpec((seq, SUB), lambda b, i: (b, vblk)),
        ],
        out_specs=pl.BlockSpec((tq, D_MODEL), lambda b, i: (b, 0)),
        out_shape=jax.ShapeDtypeStruct((t, D_MODEL), F32),
        scratch_shapes=[pltpu.VMEM((seq, 2 * N_KV_HEADS * LANES), BF16), pltpu.VMEM((SUB, seq), BF16)],
        compiler_params=_cparams(("parallel", "arbitrary")),
        name="attn_ctx",
    )(sink, qkv, qkv, qkv)


def _attention_lat(q, kp, vt, ck, cv, sink, n_batch, seq, past):
    t = q.shape[0]
    tq = WINDOW
    nqb = seq // tq
    kwidth = 2 * N_KV_HEADS * LANES

    def kblock(delta):
        return pl.BlockSpec((tq, kwidth), lambda b, i: (b * nqb + jnp.clip(i + delta, 0, nqb - 1), 0))

    def vblock(delta):
        return pl.BlockSpec((SUB, tq), lambda b, i: (0, b * nqb + jnp.clip(i + delta, 0, nqb - 1)))

    return pl.pallas_call(
        functools.partial(_attn_kernel, window=True, tq=tq, n_qblocks=nqb),
        grid=(n_batch, nqb),
        in_specs=[
            pl.BlockSpec(memory_space=pltpu.SMEM),
            pl.BlockSpec((tq, D_MODEL), lambda b, i: (b * nqb + i, 0)),
            pl.BlockSpec((past, SUB), lambda b, i: (b, 0)),
            pl.BlockSpec((past, SUB), lambda b, i: (b, 0)),
            kblock(-1), kblock(0), kblock(1),
            vblock(-1), vblock(0), vblock(1),
            pl.BlockSpec((3 * WINDOW, 2 * tq), lambda b, i: (0, 0)),
        ],
        out_specs=pl.BlockSpec((tq, D_MODEL), lambda b, i: (b * nqb + i, 0)),
        out_shape=jax.ShapeDtypeStruct((t, D_MODEL), F32),
        scratch_shapes=[pltpu.VMEM((past, kwidth), BF16), pltpu.VMEM((SUB, past), BF16)],
        compiler_params=_cparams(("parallel", "arbitrary")),
        name="attn_lat",
    )(sink, q, ck, cv, kp, kp, kp, vt, vt, vt, _window_bias_t(tq))


def _fgen_kernel(z_ref, w1_ref, b1_ref, w2_ref, b2_ref, fr_ref,
                 w3b_ref, b3b_ref, dcb_ref, w3f_ref, b3f_ref, dcf_ref, o_ref, a2_scr, *, n):
    @pl.when((pl.program_id(0) == 0) & (pl.program_id(1) == 0))
    def _():
        fr = fr_ref[...]
        a1 = jnp.sin(fr * (jnp.dot(z_ref[...], w1_ref[...], precision=HIGHEST,
                                   preferred_element_type=F32) + b1_ref[...]))
        a2_scr[...] = jnp.sin(fr * (jnp.dot(a1, w2_ref[...], precision=HIGHEST,
                                            preferred_element_type=F32) + b2_ref[...]))

    tc = o_ref.shape[1]
    tb = jnp.broadcast_to(z_ref[0:n, 0:1], (n, tc))
    tf = jnp.broadcast_to(z_ref[n:2 * n, 0:1], (n, tc))
    hb = (jnp.dot(a2_scr[0:n, :], w3b_ref[...], precision=HIGHEST, preferred_element_type=F32)
          + b3b_ref[...]) * jnp.exp(-tb * jnp.abs(dcb_ref[...]))
    hf = (jnp.dot(a2_scr[n:2 * n, :], w3f_ref[...], precision=HIGHEST, preferred_element_type=F32)
          + b3f_ref[...]) * jnp.exp(-tf * jnp.abs(dcf_ref[...]))
    tot = (jnp.sum(jnp.abs(hb), axis=0, keepdims=True)
           + jnp.sum(jnp.abs(hf), axis=0, keepdims=True))
    inv = 1.0 / (tot + EPS)
    rowid = lax.broadcasted_iota(jnp.int32, (n, tc), 0)
    o_ref[0:n, :] = jnp.where(rowid == 0, 0.0, hb * inv)
    o_ref[n:2 * n, :] = hf * inv


def _filter_gen(n, filt_w1, filt_b1, filt_w2, filt_b2, filt_w3, filt_b3, filt_freq, filt_decay, tc):
    t = jnp.arange(n, dtype=F32) / max(n - 1, 1)
    bands = jnp.arange(1, N_BANDS + 1, dtype=F32)
    ang = 2.0 * math.pi * t[:, None] * bands[None, :]
    z = jnp.concatenate([t[:, None], jnp.cos(ang), jnp.sin(ang)], axis=-1)
    zb = jnp.concatenate([z[0:1], z[1:][::-1]], axis=0)
    zfull = jnp.pad(jnp.concatenate([zb, z], axis=0), ((0, 0), (0, LANES - FILTER_EMB)))
    padw = LANES - FILTER_WIDTH
    w1 = jnp.pad(filt_w1.astype(F32), ((0, LANES - FILTER_EMB), (0, padw)))
    b1 = jnp.pad(filt_b1.astype(F32), (0, padw)).reshape(1, LANES)
    w2 = jnp.pad(filt_w2.astype(F32), ((0, padw), (0, padw)))
    b2 = jnp.pad(filt_b2.astype(F32), (0, padw)).reshape(1, LANES)
    fr = jnp.pad(filt_freq.astype(F32), (0, padw)).reshape(1, LANES)
    w3 = jnp.pad(filt_w3.astype(F32), ((0, padw), (0, 0)))
    ncol = w3.shape[1]
    b3 = filt_b3.astype(F32).reshape(1, ncol)
    dc = filt_decay.astype(F32).reshape(1, ncol)
    ct = D_HYENA // tc
    full = lambda shape: pl.BlockSpec(shape, lambda o, c: (0, 0))
    bwd = lambda rows: pl.BlockSpec((rows, tc), lambda o, c: (0, (2 * o + 1) * ct + c))
    fwd = lambda rows: pl.BlockSpec((rows, tc), lambda o, c: (0, (2 * o) * ct + c))
    return pl.pallas_call(
        functools.partial(_fgen_kernel, n=n),
        grid=(2, ct),
        in_specs=[
            full((2 * n, LANES)), full((LANES, LANES)), full((1, LANES)),
            full((LANES, LANES)), full((1, LANES)), full((1, LANES)),
            bwd(LANES), bwd(1), bwd(1), fwd(LANES), fwd(1), fwd(1),
        ],
        out_specs=pl.BlockSpec((2 * n, tc), lambda o, c: (0, o * ct + c)),
        out_shape=jax.ShapeDtypeStruct((2 * n, 2 * D_HYENA), F32),
        scratch_shapes=[pltpu.VMEM((2 * n, LANES), F32)],
        compiler_params=_cparams(("arbitrary", "arbitrary")),
        name="hyena_filter_gen",
    )(zfull, w1, b1, w2, b2, fr, w3, b3, dc, w3, b3, dc)


def _dft_mats(blk):
    f = np.arange(blk, dtype=np.int64)[:, None]
    s = np.arange(blk, dtype=np.int64)[None, :]
    theta = (np.pi / (2 * blk)) * (((2 * f + 1) * s) % (4 * blk)).astype(np.float64)
    fwd = np.concatenate([np.cos(theta), -np.sin(theta)], axis=0)
    inv = np.concatenate([np.cos(theta).T, -np.sin(theta).T], axis=1) / blk
    return (jnp.asarray(fwd, dtype=F32).astype(BF16), jnp.asarray(inv, dtype=F32).astype(BF16))


def _ftf_kernel(k_ref, skip_ref, f_ref, o_ref, *, nblk2, blk):
    tc = o_ref.shape[2]
    fmat = f_ref[...]
    fidx = lax.broadcasted_iota(jnp.int32, (blk, tc), 0)
    sgn = jnp.where(fidx % 2 == 0, 1.0, -1.0)
    prev = None
    for e in range(nblk2):
        p = jnp.dot(fmat, k_ref[e * blk:(e + 1) * blk, :].astype(BF16), preferred_element_type=F32)
        if e >= 1:
            re = p[0:blk] - sgn * prev[blk:2 * blk]
            im = p[blk:2 * blk] + sgn * prev[0:blk]
            if e == nblk2 // 2:
                re = re + skip_ref[...]
            o_ref[e - 1, 0:blk, :] = re
            o_ref[e - 1, blk:2 * blk, :] = im
        prev = p


def _filter_transform(kfull, skip, fmat, n, blk, tc):
    nblk2 = 2 * n // blk
    nd = nblk2 - 1
    ct = D_HYENA // tc
    return pl.pallas_call(
        functools.partial(_ftf_kernel, nblk2=nblk2, blk=blk),
        grid=(2, ct),
        in_specs=[
            pl.BlockSpec((2 * n, tc), lambda o, c: (0, o * ct + c)),
            pl.BlockSpec((None, 1, tc), lambda o, c: (o, 0, c)),
            pl.BlockSpec((2 * blk, blk), lambda o, c: (0, 0)),
        ],
        out_specs=pl.BlockSpec((nd, 2 * blk, tc), lambda o, c: (o, 0, c)),
        out_shape=jax.ShapeDtypeStruct((2 * nd, 2 * blk, D_HYENA), F32),
        compiler_params=_cparams(("parallel", "parallel")),
        name="hyena_filter_transform",
    )(kfull, skip, fmat)


CHUNK = 32
HALO = 8


def _short_conv_block(src_ref, b, j, n, blk, w_ref, b_ref):
    nblk = n // blk
    lo = max(j * blk - HALO, 0)
    hi = min((j + 1) * blk + HALO, n)
    off = j * blk - lo
    win = src_ref[b, lo:hi, :]
    prev = pltpu.roll(win, 1, axis=0)[off:off + blk]
    nxt = pltpu.roll(win, hi - lo - 1, axis=0)[off:off + blk]
    u = win[off:off + blk]
    rowid = lax.broadcasted_iota(jnp.int32, u.shape, 0)
    if j == 0:
        prev = jnp.where(rowid == 0, 0.0, prev)
    if j == nblk - 1:
        nxt = jnp.where(rowid == blk - 1, 0.0, nxt)
    return prev * w_ref[0:1, :] + u * w_ref[1:2, :] + nxt * w_ref[2:3, :] + b_ref[...]


def _conv_kernel(zin_ref, gin_ref, wz_ref, bz_ref, wg_ref, bg_ref, g_ref, f_ref, fi_ref,
                 o_ref, zf_scr, yf_scr, *, n, blk, bb, conv_z):
    nblk = n // blk
    tc = o_ref.shape[2]
    fmat = f_ref[...]
    for j in range(nblk):
        rows = slice(j * blk, (j + 1) * blk)
        zs = []
        for b in range(bb):
            if conv_z:
                z = _short_conv_block(zin_ref, b, j, n, blk, wz_ref, bz_ref)
            else:
                z = zin_ref[b, rows, :]
            zs.append(z.astype(BF16))
        zf_scr[j] = jnp.dot(fmat, jnp.concatenate(zs, axis=1), preferred_element_type=F32)
        for b in range(bb):
            o_ref[b, rows, :] = _short_conv_block(gin_ref, b, j, n, blk, wg_ref, bg_ref)

    def pairs(i, slot):
        for r in range(blk // CHUNK):
            re_rows = slice(r * CHUNK, (r + 1) * CHUNK)
            im_rows = slice(blk + r * CHUNK, blk + (r + 1) * CHUNK)
            acc_re = [jnp.zeros((CHUNK, tc), F32) for _ in range(bb)]
            acc_im = [jnp.zeros((CHUNK, tc), F32) for _ in range(bb)]
            for j in range(nblk):
                d = i - j + (nblk - 1)
                g_re = g_ref[d, re_rows, :]
                g_im = g_ref[d, im_rows, :]
                for b in range(bb):
                    cols = slice(b * tc, (b + 1) * tc)
                    z_re = zf_scr[j, re_rows, cols]
                    z_im = zf_scr[j, im_rows, cols]
                    acc_re[b] = acc_re[b] + (g_re * z_re - g_im * z_im)
                    acc_im[b] = acc_im[b] + (g_re * z_im + g_im * z_re)
            for b in range(bb):
                cols = slice(b * tc, (b + 1) * tc)
                yf_scr[slot, re_rows, cols] = acc_re[b].astype(BF16)
                yf_scr[slot, im_rows, cols] = acc_im[b].astype(BF16)

    def inverse(slot):
        return jnp.dot(fi_ref[...], yf_scr[slot], preferred_element_type=F32)

    def gate_out(i, y):
        rows = pl.ds(pl.multiple_of(i * blk, blk), blk)
        for b in range(bb):
            o_ref[b, rows, :] = o_ref[b, rows, :] * y[:, b * tc:(b + 1) * tc]

    pairs(0, 0)
    if nblk > 1:
        def body(i, carry):
            y = inverse((i - 1) & 1)
            pairs(i, i & 1)
            gate_out(i - 1, y)
            return carry
        lax.fori_loop(1, nblk, body, 0)
    gate_out(nblk - 1, inverse((nblk - 1) & 1))


def _long_conv(zsrc, zcol, gsrc, gcol, conv_w, conv_b, gspec, order, fmat, fimat,
               n_batch, n, blk, tc, bb, conv_z):
    nblk = n // blk
    nd = 2 * nblk - 1
    ct = D_HYENA // tc
    hy0 = GH_X1 // tc
    wz = (zcol - hy0) if conv_z else 0
    wg = gcol - hy0
    return pl.pallas_call(
        functools.partial(_conv_kernel, n=n, blk=blk, bb=bb, conv_z=conv_z),
        grid=(ct, n_batch // bb),
        in_specs=[
            pl.BlockSpec((bb, n, tc), lambda c, b: (b, 0, zcol + c)),
            pl.BlockSpec((bb, n, tc), lambda c, b: (b, 0, gcol + c)),
            pl.BlockSpec((3, tc), lambda c, b: (0, wz + c)),
            pl.BlockSpec((1, tc), lambda c, b: (0, wz + c)),
            pl.BlockSpec((3, tc), lambda c, b: (0, wg + c)),
            pl.BlockSpec((1, tc), lambda c, b: (0, wg + c)),
            _resident((nd, 2 * blk, tc), lambda c, b: (order, 0, c)),
            _resident((2 * blk, blk), lambda c, b: (0, 0)),
            _resident((blk, 2 * blk), lambda c, b: (0, 0)),
        ],
        out_specs=pl.BlockSpec((bb, n, tc), lambda c, b: (b, 0, c)),
        out_shape=jax.ShapeDtypeStruct((n_batch, n, D_HYENA), F32),
        scratch_shapes=[
            pltpu.VMEM((nblk, 2 * blk, bb * tc), F32),
            pltpu.VMEM((2, 2 * blk, bb * tc), BF16),
        ],
        compiler_params=_cparams(("parallel", "arbitrary")),
        name="hyena_conv",
    )(zsrc, gsrc, conv_w, conv_b, conv_w, conv_b, gspec, fmat, fimat)


def _hyena(gh, n_batch, n, blk, tc, bb, conv_w, conv_b, skip, filt):
    fmat, fimat = _dft_mats(blk)
    kfull = _filter_gen(n, *filt, tc=128)
    sk = skip.astype(F32).reshape(2, 1, D_HYENA)
    gspec = _filter_transform(kfull, sk, fmat, n, blk, tc=128)
    cw = conv_w.astype(F32)
    cb = conv_b.astype(F32).reshape(1, -1)
    gh3 = gh.reshape(n_batch, n, GH_COLS)
    z2 = _long_conv(gh3, GH_HV // tc, gh3, GH_X1 // tc, cw, cb, gspec, 0, fmat, fimat,
                    n_batch, n, blk, tc, bb, conv_z=True)
    y = _long_conv(z2, 0, gh3, GH_X2 // tc, cw, cb, gspec, 1, fmat, fimat,
                   n_batch, n, blk, tc, bb, conv_z=False)
    return y.reshape(n_batch * n, D_HYENA)


def _outproj_kernel(x_ref, attn_ref, hy_ref, ga_ref, gh_ref, mod_ref, wa_ref, wh_ref, wo_ref, o_ref):
    a = jnp.dot(attn_ref[...].astype(BF16), wa_ref[...], preferred_element_type=F32)
    h = jnp.dot(hy_ref[...].astype(BF16), wh_ref[...], preferred_element_type=F32)
    merged = ga_ref[...] * a + gh_ref[...] * h
    out = jnp.dot(merged.astype(BF16), wo_ref[...], preferred_element_type=F32)
    o_ref[...] = x_ref[...] + mod_ref[0, 5:6, :] * out


def _outproj(x, attn, hy, gh, mod3, tiles_per_mod, wa, wh, wo, tm):
    t = x.shape[0]
    mod_map = (lambda i: (i // tiles_per_mod, 0, 0)) if tiles_per_mod else (lambda i: (0, 0, 0))
    row = lambda i: (i, 0)
    wspec = _resident((D_MODEL, D_MODEL), lambda i: (0, 0))
    return pl.pallas_call(
        _outproj_kernel,
        grid=(t // tm,),
        in_specs=[
            pl.BlockSpec((tm, D_MODEL), row),
            pl.BlockSpec((tm, D_MODEL), row),
            pl.BlockSpec((tm, D_MODEL), row),
            pl.BlockSpec((tm, D_MODEL), lambda i: (i, GH_GA // D_MODEL)),
            pl.BlockSpec((tm, D_MODEL), lambda i: (i, GH_GH // D_MODEL)),
            pl.BlockSpec((1, N_MOD, D_MODEL), mod_map),
            wspec, wspec, wspec,
        ],
        out_specs=pl.BlockSpec((tm, D_MODEL), row),
        out_shape=jax.ShapeDtypeStruct((t, D_MODEL), F32),
        compiler_params=_cparams(("parallel",)),
        name="mixer_out",
    )(x, attn, hy, gh, gh, mod3, wa, wh, wo)


def _rope_tables(n):
    pos = jnp.arange(n)
    row = (pos // GRID_W).astype(F32)
    col = (pos % GRID_W).astype(F32)
    n_freq = HEAD_DIM // 4
    inv = ROPE_BASE ** (-jnp.arange(n_freq, dtype=F32) / n_freq)
    ar = row[:, None] * inv[None, :]
    ac = col[:, None] * inv[None, :]
    cos = jnp.concatenate([jnp.cos(ar), jnp.cos(ar), jnp.cos(ac), jnp.cos(ac)], axis=-1)
    sin = jnp.concatenate([-jnp.sin(ar), jnp.sin(ar), -jnp.sin(ac), jnp.sin(ac)], axis=-1)
    reps = SUB // HEAD_DIM
    return jnp.tile(cos, (1, reps)), jnp.tile(sin, (1, reps))


def _layer(x, mod3, per_batch_mod, n_batch, seq, w, ctx_kv, hy_blk, hy_tc, hy_bb):
    tm = 1024
    tiles_per_mod = (seq // tm) if per_batch_mod else 0
    tmf = 2048
    ffn_tiles_per_mod = (seq // tmf) if per_batch_mod else 0
    x = _ffn(x, mod3, ffn_tiles_per_mod, w["norm_ffn1"], w["ffn1_wi"], w["ffn1_wo"], 0, tmf, 256)
    qkv = None
    tmi = 512
    inproj_tiles_per_mod = (seq // tmi) if per_batch_mod else 0
    if ctx_kv is None:
        qkv, gh = _inproj(x, mod3, inproj_tiles_per_mod, w["norm_mix"], w["w_qkv"], w["w_gh"], w["qg"],
                          w["kg"], w["bd"], None, tmi, attn_layout=False)
        attn = _attention_ctx(qkv, w["sink"], n_batch, seq)
    else:
        q, kp, vt, gh = _inproj(x, mod3, inproj_tiles_per_mod, w["norm_mix"], w["w_qkv"], w["w_gh"],
                                w["qg"], w["kg"], w["bd"], _rope_tables(seq), tmi, attn_layout=True)
        attn = _attention_lat(q, kp, vt, ctx_kv[0], ctx_kv[1], w["sink"], n_batch, seq,
                              ctx_kv[0].shape[0] // n_batch)
    hy = _hyena(gh, n_batch, seq, hy_blk, hy_tc, hy_bb, w["conv_w"], w["conv_b"], w["hyena_skip"], w["filt"])
    tmo = 512
    x = _outproj(x, attn, hy, gh, mod3, (seq // tmo) if per_batch_mod else 0,
                 w["wa"], w["wh"], w["wo"], tmo)
    x = _ffn(x, mod3, ffn_tiles_per_mod, w["norm_ffn2"], w["ffn2_wi"], w["ffn2_wo"], 6, tmf, 256)
    return x, qkv


def kernel(x_prompt, x_sample, cache_k, cache_v, c, c_ctx, w_mod, b_mod, norm_ffn1, ffn1_wi, ffn1_wo, norm_mix, w_in, q_norm, k_norm, attn_sink, conv_w, conv_b, filt_w1, filt_b1, filt_w2, filt_b2, filt_w3, filt_b3, filt_freq, filt_decay, hyena_skip, w_attn_branch, w_hyena_branch, w_out, norm_ffn2, ffn2_wi, ffn2_wo):
    batch, seq, _ = x_prompt.shape
    dec_batch, dec_seq, _ = x_sample.shape
    depth = w_mod.shape[0]
    past = cache_k.shape[2]

    yp = x_prompt.reshape(batch * seq, D_MODEL)
    ys = x_sample.reshape(dec_batch * dec_seq, D_MODEL)
    bd = jnp.asarray(np.kron(np.eye(SUB // HEAD_DIM), np.ones((HEAD_DIM, HEAD_DIM))), dtype=F32).astype(BF16)
    new_ks, new_vs = [], []
    for l in range(depth):
        c_rows = jnp.zeros((16, D_MODEL), F32).at[0:dec_batch].set(c).at[dec_batch].set(c_ctx)
        mod = _modulation(c_rows, w_mod[l], b_mod[l]).reshape(16, N_MOD, D_MODEL)
        mod_lat = mod[0:dec_batch]
        mod_ctx = mod[dec_batch:dec_batch + 1]
        wl = w_in[l]
        hy_end = QKV_COLS + 3 * D_HYENA
        w = {
            "norm_ffn1": norm_ffn1[l], "ffn1_wi": ffn1_wi[l].astype(BF16), "ffn1_wo": ffn1_wo[l].astype(BF16),
            "norm_mix": norm_mix[l],
            "w_qkv": wl[:, 0:QKV_COLS].astype(BF16),
            "w_gh": jnp.concatenate([wl[:, hy_end:], wl[:, QKV_COLS:hy_end]], axis=1).astype(BF16),
            "qg": jnp.tile(q_norm[l], SUB // HEAD_DIM).reshape(1, SUB),
            "kg": jnp.tile(k_norm[l], SUB // HEAD_DIM).reshape(1, SUB),
            "bd": bd, "sink": attn_sink[l],
            "conv_w": conv_w[l], "conv_b": conv_b[l], "hyena_skip": hyena_skip[l],
            "filt": (filt_w1[l], filt_b1[l], filt_w2[l], filt_b2[l], filt_w3[l], filt_b3[l],
                     filt_freq[l], filt_decay[l]),
            "wa": w_attn_branch[l].astype(BF16), "wh": w_hyena_branch[l].astype(BF16),
            "wo": w_out[l].astype(BF16),
            "norm_ffn2": norm_ffn2[l], "ffn2_wi": ffn2_wi[l].astype(BF16), "ffn2_wo": ffn2_wo[l].astype(BF16),
        }
        yp, qkv_p = _layer(yp, mod_ctx, False, batch, seq, w, None, hy_blk=256, hy_tc=128, hy_bb=8)
        ck = cache_k[:, l].reshape(dec_batch * past, N_KV_HEADS * HEAD_DIM)
        cv = cache_v[:, l].reshape(dec_batch * past, N_KV_HEADS * HEAD_DIM)
        ys, _ = _layer(ys, mod_lat, True, dec_batch, dec_seq, w, (ck, cv), hy_blk=512, hy_tc=128, hy_bb=2)
        new_ks.append(qkv_p[:, QKV_K:QKV_K + SUB].reshape(batch, seq, N_KV_HEADS, HEAD_DIM))
        new_vs.append(qkv_p[:, QKV_V:QKV_V + SUB].reshape(batch, seq, N_KV_HEADS, HEAD_DIM))
    new_k = jnp.stack(new_ks, axis=1)
    new_v = jnp.stack(new_vs, axis=1)
    return (yp.reshape(batch, seq, D_MODEL), ys.reshape(dec_batch, dec_seq, D_MODEL), new_k, new_v)
```

```python
import functools
import math

import numpy as np
import jax
import jax.numpy as jnp
from jax import lax
from jax.experimental import pallas as pl
from jax.experimental.pallas import tpu as pltpu

F32 = jnp.float32
BF16 = jnp.bfloat16
HIGHEST = lax.Precision.HIGHEST

D_MODEL = 1024
N_HEADS = 16
N_KV_HEADS = 4
HEAD_DIM = 64
GRID_W = 64
WINDOW = 128
ROPE_BASE = 10000.0
D_HYENA = 1024
N_BANDS = 16
FILTER_EMB = 1 + 2 * N_BANDS
FILTER_WIDTH = 64
D_FF = 2816
N_MOD = 9
EPS = 1e-6
NEG_INF = -1e30
LOG2E = math.log2(math.e)

QKV_COLS = 1536
QKV_K = 1024
QKV_V = 1280
GH_COLS = 5120
GH_GA = 0
GH_GH = 1024
GH_X1 = 2048
GH_X2 = 3072
GH_HV = 4096
SUB = 256

LANES = 128
VMEM_LIMIT = 61 * 1024 * 1024


def _cparams(sem):
    return pltpu.CompilerParams(dimension_semantics=sem, vmem_limit_bytes=VMEM_LIMIT)


def _resident(shape, index_map):
    return pl.BlockSpec(shape, index_map, pipeline_mode=pl.Buffered(1))


def _mod_kernel(c_ref, w_ref, b_ref, o_ref):
    c = c_ref[...]
    s = c * jax.nn.sigmoid(c)
    o_ref[...] = jnp.dot(s, w_ref[...], precision=HIGHEST, preferred_element_type=F32) + b_ref[...]


def _modulation(c_rows, w_mod, b_mod):
    rows = c_rows.shape[0]
    n_out = w_mod.shape[1]
    tn = 1024
    return pl.pallas_call(
        _mod_kernel,
        grid=(n_out // tn,),
        in_specs=[
            pl.BlockSpec((rows, D_MODEL), lambda j: (0, 0)),
            pl.BlockSpec((D_MODEL, tn), lambda j: (0, j)),
            pl.BlockSpec((1, tn), lambda j: (0, j)),
        ],
        out_specs=pl.BlockSpec((rows, tn), lambda j: (0, j)),
        out_shape=jax.ShapeDtypeStruct((rows, n_out), F32),
        compiler_params=_cparams(("arbitrary",)),
        name="modulation",
    )(c_rows, w_mod, b_mod.reshape(1, n_out))


def _norm_modulate(x, g, shift, scale):
    ms = jnp.mean(x * x, axis=-1, keepdims=True)
    y = x * lax.rsqrt(ms + EPS) * g
    return y * (1.0 + scale) + shift


def _ffn_kernel(x_ref, mod_ref, g_ref, wi_ref, wo_ref, o_ref, *, mod_base, tf):
    x = x_ref[...]
    shift = mod_ref[0, mod_base:mod_base + 1, :]
    scale = mod_ref[0, mod_base + 1:mod_base + 2, :]
    h = _norm_modulate(x, g_ref[...], shift, scale).astype(BF16)
    nj = D_FF // tf

    def gate_up(j):
        gate = jnp.dot(h, wi_ref[:, j * tf:(j + 1) * tf], preferred_element_type=F32)
        up = jnp.dot(h, wi_ref[:, D_FF + j * tf:D_FF + (j + 1) * tf], preferred_element_type=F32)
        return gate, up

    ahead = 2
    pending = {j: gate_up(j) for j in range(ahead)}
    acc = None
    for j in range(nj):
        if j + ahead < nj:
            pending[j + ahead] = gate_up(j + ahead)
        gate, up = pending.pop(j)
        a = (gate * jax.nn.sigmoid(gate) * up).astype(BF16)
        d = jnp.dot(a, wo_ref[j * tf:(j + 1) * tf, :], preferred_element_type=F32)
        acc = d if acc is None else acc + d
    gmod = mod_ref[0, mod_base + 2:mod_base + 3, :]
    o_ref[...] = x + (0.5 * gmod) * acc


def _ffn(x, mod3, tiles_per_mod, g, wi, wo, mod_base, tm, tf):
    t = x.shape[0]
    mod_map = (lambda i: (i // tiles_per_mod, 0, 0)) if tiles_per_mod else (lambda i: (0, 0, 0))
    return pl.pallas_call(
        functools.partial(_ffn_kernel, mod_base=mod_base, tf=tf),
        grid=(t // tm,),
        in_specs=[
            pl.BlockSpec((tm, D_MODEL), lambda i: (i, 0)),
            pl.BlockSpec((1, N_MOD, D_MODEL), mod_map),
            pl.BlockSpec((1, D_MODEL), lambda i: (0, 0)),
            _resident((D_MODEL, 2 * D_FF), lambda i: (0, 0)),
            _resident((D_FF, D_MODEL), lambda i: (0, 0)),
        ],
        out_specs=pl.BlockSpec((tm, D_MODEL), lambda i: (i, 0)),
        out_shape=jax.ShapeDtypeStruct((t, D_MODEL), F32),
        compiler_params=_cparams(("parallel",)),
        name="ffn",
    )(x, mod3, g.reshape(1, D_MODEL), wi, wo)


def _rope_partner(y):
    lane = lax.broadcasted_iota(jnp.int32, (1, LANES), 1)
    first = (lane % 32) < 16
    parts = []
    for c in range(y.shape[1] // LANES):
        yc = y[:, c * LANES:(c + 1) * LANES]
        fwd = pltpu.roll(yc, LANES - 16, axis=1)
        bwd = pltpu.roll(yc, 16, axis=1)
        parts.append(jnp.where(first, fwd, bwd))
    return jnp.concatenate(parts, axis=1)


def _place_halves(x, own_half):
    lane = lax.broadcasted_iota(jnp.int32, (1, LANES), 1)
    in_half = [lane < 64, lane >= 64]
    swapped = pltpu.roll(x, 64, axis=1)
    out = [None, None]
    out[own_half] = jnp.where(in_half[own_half], x, 0.0).astype(BF16)
    out[1 - own_half] = jnp.where(in_half[1 - own_half], swapped, 0.0).astype(BF16)
    return out


def _store_placed_keys(k, dst_ref):
    for g in range(N_KV_HEADS):
        halves = _place_halves(k[:, (g // 2) * LANES:(g // 2 + 1) * LANES], g % 2)
        for a in range(2):
            dst_ref[:, (2 * g + a) * LANES:(2 * g + a + 1) * LANES] = halves[a]


def _inproj_kernel(*refs, rope, attn_layout):
    refs = list(refs)
    x_ref, mod_ref, g_ref, wq_ref, wg_ref, qg_ref, kg_ref, bd_ref = refs[:8]
    refs = refs[8:]
    if rope:
        cos_ref, sin_ref = refs[:2]
        refs = refs[2:]
    if attn_layout:
        q_ref, kp_ref, vt_ref, gh_ref = refs
    else:
        qkv_ref, gh_ref = refs

    h = _norm_modulate(x_ref[...], g_ref[...], mod_ref[0, 3:4, :], mod_ref[0, 4:5, :]).astype(BF16)

    def head_norm(t, gain):
        ss = jnp.dot((t * t).astype(BF16), bd_ref[...], preferred_element_type=F32)
        y = t * lax.rsqrt(ss * (1.0 / HEAD_DIM) + EPS) * gain
        if rope:
            y = y * cos_ref[...] + _rope_partner(y) * sin_ref[...]
        return y

    def q_epilogue(acc, c0):
        gain = qg_ref[...] * (LOG2E / math.sqrt(HEAD_DIM))
        for c in range(acc.shape[1] // SUB):
            y = head_norm(acc[:, c * SUB:(c + 1) * SUB], gain)
            cols = slice(c0 + c * SUB, c0 + (c + 1) * SUB)
            if attn_layout:
                q_ref[:, cols] = y.astype(BF16)
            else:
                qkv_ref[:, cols] = y

    def kv_epilogue(acc, c0):
        k = head_norm(acc[:, 0:SUB], kg_ref[...])
        v = acc[:, SUB:2 * SUB]
        if attn_layout:
            _store_placed_keys(k, kp_ref)
            vt_ref[...] = v.T.astype(BF16)
        else:
            qkv_ref[:, QKV_K:QKV_K + SUB] = k
            qkv_ref[:, QKV_V:QKV_V + SUB] = v

    def gate_epilogue(acc, c0):
        gh_ref[:, c0:c0 + acc.shape[1]] = jax.nn.sigmoid(acc)

    def raw_epilogue(acc, c0):
        gh_ref[:, c0:c0 + acc.shape[1]] = acc

    half = QKV_K // 2
    chunks = [(wq_ref, 0, half, q_epilogue), (wq_ref, half, half, q_epilogue),
              (wq_ref, QKV_K, 2 * SUB, kv_epilogue)]
    for c0 in range(0, GH_COLS, D_MODEL):
        chunks.append((wg_ref, c0, D_MODEL, gate_epilogue if c0 < GH_X1 else raw_epilogue))

    def matmul(idx):
        w_ref, c0, width, _ = chunks[idx]
        return jnp.dot(h, w_ref[:, c0:c0 + width], preferred_element_type=F32)

    ahead = 2
    pending = {idx: matmul(idx) for idx in range(ahead)}
    for idx in range(len(chunks)):
        if idx + ahead < len(chunks):
            pending[idx + ahead] = matmul(idx + ahead)
        chunks[idx][3](pending.pop(idx), chunks[idx][1])


def _inproj(x, mod3, tiles_per_mod, g, wq, wg, qg, kg, bd, rope_tabs, tm, attn_layout):
    t = x.shape[0]
    rope = rope_tabs is not None
    mod_map = (lambda i: (i // tiles_per_mod, 0, 0)) if tiles_per_mod else (lambda i: (0, 0, 0))
    row = lambda i: (i, 0)
    const = lambda i: (0, 0)
    in_specs = [
        pl.BlockSpec((tm, D_MODEL), row),
        pl.BlockSpec((1, N_MOD, D_MODEL), mod_map),
        pl.BlockSpec((1, D_MODEL), const),
        _resident((D_MODEL, QKV_COLS), const),
        _resident((D_MODEL, GH_COLS), const),
        pl.BlockSpec((1, SUB), const),
        pl.BlockSpec((1, SUB), const),
        pl.BlockSpec((SUB, SUB), const),
    ]
    args = [x, mod3, g.reshape(1, D_MODEL), wq, wg, qg, kg, bd]
    if rope:
        seq_tiles = rope_tabs[0].shape[0] // tm
        for tab in rope_tabs:
            in_specs.append(pl.BlockSpec((tm, SUB), lambda i: (i % seq_tiles, 0)))
            args.append(tab)
    gh_spec = pl.BlockSpec((tm, GH_COLS), row)
    gh_shape = jax.ShapeDtypeStruct((t, GH_COLS), F32)
    if attn_layout:
        kwidth = 2 * N_KV_HEADS * LANES
        out_specs = [pl.BlockSpec((tm, QKV_K), row), pl.BlockSpec((tm, kwidth), row),
                     pl.BlockSpec((SUB, tm), lambda i: (0, i)), gh_spec]
        out_shape = [jax.ShapeDtypeStruct((t, QKV_K), BF16), jax.ShapeDtypeStruct((t, kwidth), BF16),
                     jax.ShapeDtypeStruct((SUB, t), BF16), gh_shape]
    else:
        out_specs = [pl.BlockSpec((tm, QKV_COLS), row), gh_spec]
        out_shape = [jax.ShapeDtypeStruct((t, QKV_COLS), F32), gh_shape]
    return pl.pallas_call(
        functools.partial(_inproj_kernel, rope=rope, attn_layout=attn_layout),
        grid=(t // tm,),
        in_specs=in_specs,
        out_specs=out_specs,
        out_shape=out_shape,
        compiler_params=_cparams(("parallel",)),
        name="inproj",
    )(*args)


def _col_reduce(x, op):
    rows, cols = x.shape
    part = op(x.reshape(8, rows // 8, cols), axis=0)
    return op(part, axis=0, keepdims=True)


def _attn_kernel(*refs, window, tq, n_qblocks):
    if window:
        (sink_ref, q_ref, kc_ref, vc_ref, kp_ref, kcur_ref, kn_ref,
         vp_ref, vcur_ref, vn_ref, bias_ref, o_ref, kcp_scr, vct_scr) = refs
    else:
        sink_ref, q_ref, kc_ref, vc_ref, o_ref, kcp_scr, vct_scr = refs
    i = pl.program_id(1)

    @pl.when(i == 0)
    def _():
        _store_placed_keys(kc_ref[...], kcp_scr)
        vct_scr[...] = vc_ref[...].T.astype(BF16)

    if window:
        krow = lax.broadcasted_iota(jnp.int32, (3 * WINDOW, 2 * tq), 0)
        edge = (jnp.where((krow < WINDOW) & (i == 0), NEG_INF, 0.0)
                + jnp.where((krow >= 2 * WINDOW) & (i == n_qblocks - 1), NEG_INF, 0.0))
        bias = bias_ref[...] + edge
    lane = lax.broadcasted_iota(jnp.int32, (1, 2 * tq), 1)
    dn = (((1,), (1,)), ((), ()))
    nkc = kcp_scr.shape[0]
    n_units = 2 * N_KV_HEADS

    def scores(u):
        g, a = divmod(u, 2)
        q0 = q_ref[:, g * 2 * LANES:g * 2 * LANES + LANES]
        q1 = q_ref[:, g * 2 * LANES + LANES:(g + 1) * 2 * LANES]
        qg = jnp.concatenate([q0, q1], axis=0).astype(BF16)
        kcols = slice(u * LANES, (u + 1) * LANES)
        keys = [kcp_scr[:, kcols]]
        if window:
            keys += [kp_ref[:, kcols], kcur_ref[:, kcols], kn_ref[:, kcols]]
        return lax.dot_general(jnp.concatenate(keys, axis=0), qg, dn, preferred_element_type=F32)

    def softmax(u, s):
        g, a = divmod(u, 2)
        sink = jnp.where(lane < tq, sink_ref[4 * g + a], sink_ref[4 * g + 2 + a]) * LOG2E
        s_c = s[0:nkc]
        m = jnp.maximum(_col_reduce(s_c, jnp.max), sink)
        if window:
            s_w = s[nkc:] + bias
            m = jnp.maximum(m, _col_reduce(s_w, jnp.max))
        p_c = jnp.exp2(s_c - m)
        l = _col_reduce(p_c, jnp.sum) + jnp.exp2(sink - m)
        probs = [p_c.astype(BF16)]
        if window:
            p_w = jnp.exp2(s_w - m)
            l = l + _col_reduce(p_w, jnp.sum)
            probs.append(p_w.astype(BF16))
        return probs, 1.0 / l

    def values(u, probs, inv_l):
        g = u // 2
        vrows = slice(g * HEAD_DIM, (g + 1) * HEAD_DIM)
        o = jnp.dot(vct_scr[vrows, :], probs[0], preferred_element_type=F32)
        if window:
            vw_t = jnp.concatenate([vp_ref[vrows, :], vcur_ref[vrows, :], vn_ref[vrows, :]], axis=1)
            o = o + jnp.dot(vw_t, probs[1], preferred_element_type=F32)
        return o * inv_l

    outs = []
    s_q = {0: scores(0), 1: scores(1)}
    p_q = {}
    for step in range(n_units + 1):
        if step + 2 < n_units:
            s_q[step + 2] = scores(step + 2)
        if 0 <= step < n_units:
            p_q[step] = softmax(step, s_q.pop(step))
        u = step - 1
        if 0 <= u < n_units:
            outs.append(values(u, *p_q.pop(u)))
            if u % 2 == 1:
                g = u // 2
                o_t = jnp.concatenate(outs[-2:], axis=0)
                o_ref[:, g * 2 * LANES:g * 2 * LANES + LANES] = o_t[:, 0:tq].T
                o_ref[:, g * 2 * LANES + LANES:(g + 1) * 2 * LANES] = o_t[:, tq:2 * tq].T


def _window_bias_t(tq):
    a = np.arange(WINDOW)[None, :]
    b = np.arange(WINDOW)[:, None]
    prev = np.where(b >= a, 0.0, NEG_INF)
    cur = np.zeros((WINDOW, WINDOW))
    nxt = np.where(b <= a, 0.0, NEG_INF)
    one = np.concatenate([prev, cur, nxt], axis=0).astype(np.float32)
    return jnp.asarray(np.concatenate([one, one], axis=1))


def _attention_ctx(qkv, sink, n_batch, seq):
    t = qkv.shape[0]
    tq = seq
    kblk = QKV_K // SUB
    vblk = QKV_V // SUB
    return pl.pallas_call(
        functools.partial(_attn_kernel, window=False, tq=tq, n_qblocks=1),
        grid=(n_batch, 1),
        in_specs=[
            pl.BlockSpec(memory_space=pltpu.SMEM),
            pl.BlockSpec((tq, D_MODEL), lambda b, i: (b, 0)),
            pl.BlockSpec((seq, SUB), lambda b, i: (b, kblk)),
            pl.BlockSpec((seq, SUB), lambda b, i: (b, vblk)),
        ],
        out_specs=pl.BlockSpec((tq, D_MODEL), lambda b, i: (b, 0)),
        out_shape=jax.ShapeDtypeStruct((t, D_MODEL), F32),
        scratch_shapes=[pltpu.VMEM((seq, 2 * N_KV_HEADS * LANES), BF16), pltpu.VMEM((SUB, seq), BF16)],
        compiler_params=_cparams(("parallel", "arbitrary")),
        name="attn_ctx",
    )(sink, qkv, qkv, qkv)


def _attention_lat(q, kp, vt, ck, cv, sink, n_batch, seq, past):
    t = q.shape[0]
    tq = WINDOW
    nqb = seq // tq
    kwidth = 2 * N_KV_HEADS * LANES

    def kblock(delta):
        return pl.BlockSpec((tq, kwidth), lambda b, i: (b * nqb + jnp.clip(i + delta, 0, nqb - 1), 0))

    def vblock(delta):
        return pl.BlockSpec((SUB, tq), lambda b, i: (0, b * nqb + jnp.clip(i + delta, 0, nqb - 1)))

    return pl.pallas_call(
        functools.partial(_attn_kernel, window=True, tq=tq, n_qblocks=nqb),
        grid=(n_batch, nqb),
        in_specs=[
            pl.BlockSpec(memory_space=pltpu.SMEM),
            pl.BlockSpec((tq, D_MODEL), lambda b, i: (b * nqb + i, 0)),
            pl.BlockSpec((past, SUB), lambda b, i: (b, 0)),
            pl.BlockSpec((past, SUB), lambda b, i: (b, 0)),
            kblock(-1), kblock(0), kblock(1),
            vblock(-1), vblock(0), vblock(1),
            pl.BlockSpec((3 * WINDOW, 2 * tq), lambda b, i: (0, 0)),
        ],
        out_specs=pl.BlockSpec((tq, D_MODEL), lambda b, i: (b * nqb + i, 0)),
        out_shape=jax.ShapeDtypeStruct((t, D_MODEL), F32),
        scratch_shapes=[pltpu.VMEM((past, kwidth), BF16), pltpu.VMEM((SUB, past), BF16)],
        compiler_params=_cparams(("parallel", "arbitrary")),
        name="attn_lat",
    )(sink, q, ck, cv, kp, kp, kp, vt, vt, vt, _window_bias_t(tq))


def _fgen_kernel(z_ref, w1_ref, b1_ref, w2_ref, b2_ref, fr_ref,
                 w3b_ref, b3b_ref, dcb_ref, w3f_ref, b3f_ref, dcf_ref, o_ref, a2_scr, *, n):
    @pl.when((pl.program_id(0) == 0) & (pl.program_id(1) == 0))
    def _():
        fr = fr_ref[...]
        a1 = jnp.sin(fr * (jnp.dot(z_ref[...], w1_ref[...], precision=HIGHEST,
                                   preferred_element_type=F32) + b1_ref[...]))
        a2_scr[...] = jnp.sin(fr * (jnp.dot(a1, w2_ref[...], precision=HIGHEST,
                                            preferred_element_type=F32) + b2_ref[...]))

    tc = o_ref.shape[1]
    tb = jnp.broadcast_to(z_ref[0:n, 0:1], (n, tc))
    tf = jnp.broadcast_to(z_ref[n:2 * n, 0:1], (n, tc))
    hb = (jnp.dot(a2_scr[0:n, :], w3b_ref[...], precision=HIGHEST, preferred_element_type=F32)
          + b3b_ref[...]) * jnp.exp(-tb * jnp.abs(dcb_ref[...]))
    hf = (jnp.dot(a2_scr[n:2 * n, :], w3f_ref[...], precision=HIGHEST, preferred_element_type=F32)
          + b3f_ref[...]) * jnp.exp(-tf * jnp.abs(dcf_ref[...]))
    tot = (jnp.sum(jnp.abs(hb), axis=0, keepdims=True)
           + jnp.sum(jnp.abs(hf), axis=0, keepdims=True))
    inv = 1.0 / (tot + EPS)
    rowid = lax.broadcasted_iota(jnp.int32, (n, tc), 0)
    o_ref[0:n, :] = jnp.where(rowid == 0, 0.0, hb * inv)
    o_ref[n:2 * n, :] = hf * inv


def _filter_gen(n, filt_w1, filt_b1, filt_w2, filt_b2, filt_w3, filt_b3, filt_freq, filt_decay, tc):
    t = np.arange(n, dtype=np.float64) / max(n - 1, 1)
    bands = np.arange(1, N_BANDS + 1, dtype=np.float64)
    ang = 2.0 * math.pi * t[:, None] * bands[None, :]
    z = np.concatenate([t[:, None], np.cos(ang), np.sin(ang)], axis=-1)
    zb = np.concatenate([z[0:1], z[1:][::-1]], axis=0)
    zfull = jnp.asarray(np.pad(np.concatenate([zb, z], axis=0), ((0, 0), (0, LANES - FILTER_EMB))), dtype=F32)
    padw = LANES - FILTER_WIDTH
    w1 = jnp.pad(filt_w1.astype(F32), ((0, LANES - FILTER_EMB), (0, padw)))
    b1 = jnp.pad(filt_b1.astype(F32), (0, padw)).reshape(1, LANES)
    w2 = jnp.pad(filt_w2.astype(F32), ((0, padw), (0, padw)))
    b2 = jnp.pad(filt_b2.astype(F32), (0, padw)).reshape(1, LANES)
    fr = jnp.pad(filt_freq.astype(F32), (0, padw)).reshape(1, LANES)
    w3 = jnp.pad(filt_w3.astype(F32), ((0, padw), (0, 0)))
    ncol = w3.shape[1]
    b3 = filt_b3.astype(F32).reshape(1, ncol)
    dc = filt_decay.astype(F32).reshape(1, ncol)
    ct = D_HYENA // tc
    full = lambda shape: pl.BlockSpec(shape, lambda o, c: (0, 0))
    bwd = lambda rows: pl.BlockSpec((rows, tc), lambda o, c: (0, (2 * o + 1) * ct + c))
    fwd = lambda rows: pl.BlockSpec((rows, tc), lambda o, c: (0, (2 * o) * ct + c))
    return pl.pallas_call(
        functools.partial(_fgen_kernel, n=n),
        grid=(2, ct),
        in_specs=[
            full((2 * n, LANES)), full((LANES, LANES)), full((1, LANES)),
            full((LANES, LANES)), full((1, LANES)), full((1, LANES)),
            bwd(LANES), bwd(1), bwd(1), fwd(LANES), fwd(1), fwd(1),
        ],
        out_specs=pl.BlockSpec((2 * n, tc), lambda o, c: (0, o * ct + c)),
        out_shape=jax.ShapeDtypeStruct((2 * n, 2 * D_HYENA), F32),
        scratch_shapes=[pltpu.VMEM((2 * n, LANES), F32)],
        compiler_params=_cparams(("arbitrary", "arbitrary")),
        name="hyena_filter_gen",
    )(zfull, w1, b1, w2, b2, fr, w3, b3, dc, w3, b3, dc)


def _dft_mats(blk):
    f = np.arange(blk, dtype=np.int64)[:, None]
    s = np.arange(blk, dtype=np.int64)[None, :]
    theta = (np.pi / (2 * blk)) * (((2 * f + 1) * s) % (4 * blk)).astype(np.float64)
    fwd = np.concatenate([np.cos(theta), -np.sin(theta)], axis=0)
    inv = np.concatenate([np.cos(theta).T, -np.sin(theta).T], axis=1) / blk
    return (jnp.asarray(fwd, dtype=F32).astype(BF16), jnp.asarray(inv, dtype=F32).astype(BF16))


def _ftf_kernel(k_ref, skip_ref, f_ref, o_ref, *, nblk2, blk):
    tc = o_ref.shape[2]
    fmat = f_ref[...]
    fidx = lax.broadcasted_iota(jnp.int32, (blk, tc), 0)
    sgn = jnp.where(fidx % 2 == 0, 1.0, -1.0)
    prev = None
    for e in range(nblk2):
        p = jnp.dot(fmat, k_ref[e * blk:(e + 1) * blk, :].astype(BF16), preferred_element_type=F32)
        if e >= 1:
            re = p[0:blk] - sgn * prev[blk:2 * blk]
            im = p[blk:2 * blk] + sgn * prev[0:blk]
            if e == nblk2 // 2:
                re = re + skip_ref[...]
            o_ref[e - 1, 0:blk, :] = re
            o_ref[e - 1, blk:2 * blk, :] = im
        prev = p


def _filter_transform(kfull, skip, fmat, n, blk, tc):
    nblk2 = 2 * n // blk
    nd = nblk2 - 1
    ct = D_HYENA // tc
    return pl.pallas_call(
        functools.partial(_ftf_kernel, nblk2=nblk2, blk=blk),
        grid=(2, ct),
        in_specs=[
            pl.BlockSpec((2 * n, tc), lambda o, c: (0, o * ct + c)),
            pl.BlockSpec((None, 1, tc), lambda o, c: (o, 0, c)),
            pl.BlockSpec((2 * blk, blk), lambda o, c: (0, 0)),
        ],
        out_specs=pl.BlockSpec((nd, 2 * blk, tc), lambda o, c: (o, 0, c)),
        out_shape=jax.ShapeDtypeStruct((2 * nd, 2 * blk, D_HYENA), F32),
        compiler_params=_cparams(("parallel", "parallel")),
        name="hyena_filter_transform",
    )(kfull, skip, fmat)


CHUNK = 32
HALO = 8


def _short_conv_block(src_ref, b, j, n, blk, w_ref, b_ref):
    nblk = n // blk
    lo = max(j * blk - HALO, 0)
    hi = min((j + 1) * blk + HALO, n)
    off = j * blk - lo
    win = src_ref[b, lo:hi, :]
    prev = pltpu.roll(win, 1, axis=0)[off:off + blk]
    nxt = pltpu.roll(win, hi - lo - 1, axis=0)[off:off + blk]
    u = win[off:off + blk]
    rowid = lax.broadcasted_iota(jnp.int32, u.shape, 0)
    if j == 0:
        prev = jnp.where(rowid == 0, 0.0, prev)
    if j == nblk - 1:
        nxt = jnp.where(rowid == blk - 1, 0.0, nxt)
    return prev * w_ref[0:1, :] + u * w_ref[1:2, :] + nxt * w_ref[2:3, :] + b_ref[...]


def _conv_kernel(zin_ref, gin_ref, wz_ref, bz_ref, wg_ref, bg_ref, g_ref, f_ref, fi_ref,
                 o_ref, zf_scr, yf_scr, *, n, blk, bb, conv_z):
    nblk = n // blk
    tc = o_ref.shape[2]
    fmat = f_ref[...]
    for j in range(nblk):
        rows = slice(j * blk, (j + 1) * blk)
        zs = []
        for b in range(bb):
            if conv_z:
                z = _short_conv_block(zin_ref, b, j, n, blk, wz_ref, bz_ref)
            else:
                z = zin_ref[b, rows, :]
            zs.append(z.astype(BF16))
        zf_scr[j] = jnp.dot(fmat, jnp.concatenate(zs, axis=1), preferred_element_type=F32)
        for b in range(bb):
            o_ref[b, rows, :] = _short_conv_block(gin_ref, b, j, n, blk, wg_ref, bg_ref)

    def pairs(i, slot):
        for r in range(blk // CHUNK):
            re_rows = slice(r * CHUNK, (r + 1) * CHUNK)
            im_rows = slice(blk + r * CHUNK, blk + (r + 1) * CHUNK)
            acc_re = [jnp.zeros((CHUNK, tc), F32) for _ in range(bb)]
            acc_im = [jnp.zeros((CHUNK, tc), F32) for _ in range(bb)]
            for j in range(nblk):
                d = i - j + (nblk - 1)
                g_re = g_ref[d, re_rows, :]
                g_im = g_ref[d, im_rows, :]
                for b in range(bb):
                    cols = slice(b * tc, (b + 1) * tc)
                    z_re = zf_scr[j, re_rows, cols]
                    z_im = zf_scr[j, im_rows, cols]
                    acc_re[b] = acc_re[b] + (g_re * z_re - g_im * z_im)
                    acc_im[b] = acc_im[b] + (g_re * z_im + g_im * z_re)
            for b in range(bb):
                cols = slice(b * tc, (b + 1) * tc)
                yf_scr[slot, re_rows, cols] = acc_re[b].astype(BF16)
                yf_scr[slot, im_rows, cols] = acc_im[b].astype(BF16)

    def inverse(slot):
        return jnp.dot(fi_ref[...], yf_scr[slot], preferred_element_type=F32)

    def gate_out(i, y):
        rows = pl.ds(pl.multiple_of(i * blk, blk), blk)
        for b in range(bb):
            o_ref[b, rows, :] = o_ref[b, rows, :] * y[:, b * tc:(b + 1) * tc]

    pairs(0, 0)
    if nblk > 1:
        def body(i, carry):
            y = inverse((i - 1) & 1)
            pairs(i, i & 1)
            gate_out(i - 1, y)
            return carry
        lax.fori_loop(1, nblk, body, 0)
    gate_out(nblk - 1, inverse((nblk - 1) & 1))


def _long_conv(zsrc, zcol, gsrc, gcol, conv_w, conv_b, gspec, order, fmat, fimat,
               n_batch, n, blk, tc, bb, conv_z):
    nblk = n // blk
    nd = 2 * nblk - 1
    ct = D_HYENA // tc
    hy0 = GH_X1 // tc
    wz = (zcol - hy0) if conv_z else 0
    wg = gcol - hy0
    return pl.pallas_call(
        functools.partial(_conv_kernel, n=n, blk=blk, bb=bb, conv_z=conv_z),
        grid=(ct, n_batch // bb),
        in_specs=[
            pl.BlockSpec((bb, n, tc), lambda c, b: (b, 0, zcol + c)),
            pl.BlockSpec((bb, n, tc), lambda c, b: (b, 0, gcol + c)),
            pl.BlockSpec((3, tc), lambda c, b: (0, wz + c)),
            pl.BlockSpec((1, tc), lambda c, b: (0, wz + c)),
            pl.BlockSpec((3, tc), lambda c, b: (0, wg + c)),
            pl.BlockSpec((1, tc), lambda c, b: (0, wg + c)),
            _resident((nd, 2 * blk, tc), lambda c, b: (order, 0, c)),
            _resident((2 * blk, blk), lambda c, b: (0, 0)),
            _resident((blk, 2 * blk), lambda c, b: (0, 0)),
        ],
        out_specs=pl.BlockSpec((bb, n, tc), lambda c, b: (b, 0, c)),
        out_shape=jax.ShapeDtypeStruct((n_batch, n, D_HYENA), F32),
        scratch_shapes=[
            pltpu.VMEM((nblk, 2 * blk, bb * tc), F32),
            pltpu.VMEM((2, 2 * blk, bb * tc), BF16),
        ],
        compiler_params=_cparams(("parallel", "arbitrary")),
        name="hyena_conv",
    )(zsrc, gsrc, conv_w, conv_b, conv_w, conv_b, gspec, fmat, fimat)


def _hyena(gh, n_batch, n, blk, tc, bb, conv_w, conv_b, skip, filt):
    fmat, fimat = _dft_mats(blk)
    kfull = _filter_gen(n, *filt, tc=128)
    sk = skip.astype(F32).reshape(2, 1, D_HYENA)
    gspec = _filter_transform(kfull, sk, fmat, n, blk, tc=256)
    cw = conv_w.astype(F32)
    cb = conv_b.astype(F32).reshape(1, -1)
    gh3 = gh.reshape(n_batch, n, GH_COLS)
    z2 = _long_conv(gh3, GH_HV // tc, gh3, GH_X1 // tc, cw, cb, gspec, 0, fmat, fimat,
                    n_batch, n, blk, tc, bb, conv_z=True)
    y = _long_conv(z2, 0, gh3, GH_X2 // tc, cw, cb, gspec, 1, fmat, fimat,
                   n_batch, n, blk, tc, bb, conv_z=False)
    return y.reshape(n_batch * n, D_HYENA)


def _outproj_kernel(x_ref, attn_ref, hy_ref, ga_ref, gh_ref, mod_ref, wa_ref, wh_ref, wo_ref, o_ref):
    a = jnp.dot(attn_ref[...].astype(BF16), wa_ref[...], preferred_element_type=F32)
    h = jnp.dot(hy_ref[...].astype(BF16), wh_ref[...], preferred_element_type=F32)
    merged = ga_ref[...] * a + gh_ref[...] * h
    out = jnp.dot(merged.astype(BF16), wo_ref[...], preferred_element_type=F32)
    o_ref[...] = x_ref[...] + mod_ref[0, 5:6, :] * out


def _outproj(x, attn, hy, gh, mod3, tiles_per_mod, wa, wh, wo, tm):
    t = x.shape[0]
    mod_map = (lambda i: (i // tiles_per_mod, 0, 0)) if tiles_per_mod else (lambda i: (0, 0, 0))
    row = lambda i: (i, 0)
    wspec = _resident((D_MODEL, D_MODEL), lambda i: (0, 0))
    return pl.pallas_call(
        _outproj_kernel,
        grid=(t // tm,),
        in_specs=[
            pl.BlockSpec((tm, D_MODEL), row),
            pl.BlockSpec((tm, D_MODEL), row),
            pl.BlockSpec((tm, D_MODEL), row),
            pl.BlockSpec((tm, D_MODEL), lambda i: (i, GH_GA // D_MODEL)),
            pl.BlockSpec((tm, D_MODEL), lambda i: (i, GH_GH // D_MODEL)),
            pl.BlockSpec((1, N_MOD, D_MODEL), mod_map),
            wspec, wspec, wspec,
        ],
        out_specs=pl.BlockSpec((tm, D_MODEL), row),
        out_shape=jax.ShapeDtypeStruct((t, D_MODEL), F32),
        compiler_params=_cparams(("parallel",)),
        name="mixer_out",
    )(x, attn, hy, gh, gh, mod3, wa, wh, wo)


def _rope_tables(n):
    pos = np.arange(n)
    row = (pos // GRID_W).astype(np.float64)
    col = (pos % GRID_W).astype(np.float64)
    n_freq = HEAD_DIM // 4
    inv = ROPE_BASE ** (-np.arange(n_freq, dtype=np.float64) / n_freq)
    ar = row[:, None] * inv[None, :]
    ac = col[:, None] * inv[None, :]
    cos = np.concatenate([np.cos(ar), np.cos(ar), np.cos(ac), np.cos(ac)], axis=-1)
    sin = np.concatenate([-np.sin(ar), np.sin(ar), -np.sin(ac), np.sin(ac)], axis=-1)
    reps = SUB // HEAD_DIM
    return (jnp.asarray(np.tile(cos, (1, reps)), dtype=F32), jnp.asarray(np.tile(sin, (1, reps)), dtype=F32))


def _layer(x, mod3, per_batch_mod, n_batch, seq, w, ctx_kv, hy_blk, hy_tc, hy_bb):
    tmf = 1024
    ffn_tiles_per_mod = (seq // tmf) if per_batch_mod else 0
    x = _ffn(x, mod3, ffn_tiles_per_mod, w["norm_ffn1"], w["ffn1_wi"], w["ffn1_wo"], 0, tmf, 256)
    qkv = None
    tmi = 512
    inproj_tiles_per_mod = (seq // tmi) if per_batch_mod else 0
    if ctx_kv is None:
        qkv, gh = _inproj(x, mod3, inproj_tiles_per_mod, w["norm_mix"], w["w_qkv"], w["w_gh"], w["qg"],
                          w["kg"], w["bd"], None, tmi, attn_layout=False)
        attn = _attention_ctx(qkv, w["sink"], n_batch, seq)
    else:
        q, kp, vt, gh = _inproj(x, mod3, inproj_tiles_per_mod, w["norm_mix"], w["w_qkv"], w["w_gh"],
                                w["qg"], w["kg"], w["bd"], _rope_tables(seq), tmi, attn_layout=True)
        attn = _attention_lat(q, kp, vt, ctx_kv[0], ctx_kv[1], w["sink"], n_batch, seq,
                              ctx_kv[0].shape[0] // n_batch)
    hy = _hyena(gh, n_batch, seq, hy_blk, hy_tc, hy_bb, w["conv_w"], w["conv_b"], w["hyena_skip"], w["filt"])
    tmo = 512
    x = _outproj(x, attn, hy, gh, mod3, (seq // tmo) if per_batch_mod else 0,
                 w["wa"], w["wh"], w["wo"], tmo)
    x = _ffn(x, mod3, ffn_tiles_per_mod, w["norm_ffn2"], w["ffn2_wi"], w["ffn2_wo"], 6, tmf, 256)
    return x, qkv


def kernel(x_prompt, x_sample, cache_k, cache_v, c, c_ctx, w_mod, b_mod, norm_ffn1, ffn1_wi, ffn1_wo, norm_mix, w_in, q_norm, k_norm, attn_sink, conv_w, conv_b, filt_w1, filt_b1, filt_w2, filt_b2, filt_w3, filt_b3, filt_freq, filt_decay, hyena_skip, w_attn_branch, w_hyena_branch, w_out, norm_ffn2, ffn2_wi, ffn2_wo):
    batch, seq, _ = x_prompt.shape
    dec_batch, dec_seq, _ = x_sample.shape
    depth = w_mod.shape[0]
    past = cache_k.shape[2]

    yp = x_prompt.reshape(batch * seq, D_MODEL)
    ys = x_sample.reshape(dec_batch * dec_seq, D_MODEL)
    bd = jnp.asarray(np.kron(np.eye(SUB // HEAD_DIM), np.ones((HEAD_DIM, HEAD_DIM))), dtype=F32).astype(BF16)
    new_ks, new_vs = [], []
    for l in range(depth):
        c_rows = jnp.zeros((16, D_MODEL), F32).at[0:dec_batch].set(c).at[dec_batch].set(c_ctx)
        mod = _modulation(c_rows, w_mod[l], b_mod[l]).reshape(16, N_MOD, D_MODEL)
        mod_lat = mod[0:dec_batch]
        mod_ctx = mod[dec_batch:dec_batch + 1]
        wl = w_in[l]
        hy_end = QKV_COLS + 3 * D_HYENA
        w = {
            "norm_ffn1": norm_ffn1[l], "ffn1_wi": ffn1_wi[l].astype(BF16), "ffn1_wo": ffn1_wo[l].astype(BF16),
            "norm_mix": norm_mix[l],
            "w_qkv": wl[:, 0:QKV_COLS].astype(BF16),
            "w_gh": jnp.concatenate([wl[:, hy_end:], wl[:, QKV_COLS:hy_end]], axis=1).astype(BF16),
            "qg": jnp.tile(q_norm[l], SUB // HEAD_DIM).reshape(1, SUB),
            "kg": jnp.tile(k_norm[l], SUB // HEAD_DIM).reshape(1, SUB),
            "bd": bd, "sink": attn_sink[l],
            "conv_w": conv_w[l], "conv_b": conv_b[l], "hyena_skip": hyena_skip[l],
            "filt": (filt_w1[l], filt_b1[l], filt_w2[l], filt_b2[l], filt_w3[l], filt_b3[l],
                     filt_freq[l], filt_decay[l]),
            "wa": w_attn_branch[l].astype(BF16), "wh": w_hyena_branch[l].astype(BF16),
            "wo": w_out[l].astype(BF16),
            "norm_ffn2": norm_ffn2[l], "ffn2_wi": ffn2_wi[l].astype(BF16), "ffn2_wo": ffn2_wo[l].astype(BF16),
        }
        yp, qkv_p = _layer(yp, mod_ctx, False, batch, seq, w, None, hy_blk=256, hy_tc=128, hy_bb=8)
        ck = cache_k[:, l].reshape(dec_batch * past, N_KV_HEADS * HEAD_DIM)
        cv = cache_v[:, l].reshape(dec_batch * past, N_KV_HEADS * HEAD_DIM)
        ys, _ = _layer(ys, mod_lat, True, dec_batch, dec_seq, w, (ck, cv), hy_blk=512, hy_tc=128, hy_bb=2)
        new_ks.append(qkv_p[:, QKV_K:QKV_K + SUB].reshape(batch, seq, N_KV_HEADS, HEAD_DIM))
        new_vs.append(qkv_p[:, QKV_V:QKV_V + SUB].reshape(batch, seq, N_KV_HEADS, HEAD_DIM))
    new_k = jnp.stack(new_ks, axis=1)
    new_v = jnp.stack(new_vs, axis=1)
    return (yp.reshape(batch, seq, D_MODEL), ys.reshape(dec_batch, dec_seq, D_MODEL), new_k, new_v)
```

```python
import functools
import math

import numpy as np
import jax
import jax.numpy as jnp
from jax import lax
from jax.experimental import pallas as pl
from jax.experimental.pallas import tpu as pltpu

F32 = jnp.float32
BF16 = jnp.bfloat16
HIGHEST = lax.Precision.HIGHEST

D_MODEL = 1024
N_HEADS = 16
N_KV_HEADS = 4
HEAD_DIM = 64
GRID_W = 64
WINDOW = 128
ROPE_BASE = 10000.0
D_HYENA = 1024
N_BANDS = 16
FILTER_EMB = 1 + 2 * N_BANDS
FILTER_WIDTH = 64
D_FF = 2816
N_MOD = 9
EPS = 1e-6
NEG_INF = -1e30
LOG2E = math.log2(math.e)

QKV_COLS = 1536
QKV_K = 1024
QKV_V = 1280
GH_COLS = 5120
GH_GA = 0
GH_GH = 1024
GH_X1 = 2048
GH_X2 = 3072
GH_HV = 4096
SUB = 256

LANES = 128
VMEM_LIMIT = 61 * 1024 * 1024


def _cparams(sem):
    return pltpu.CompilerParams(dimension_semantics=sem, vmem_limit_bytes=VMEM_LIMIT)


def _resident(shape, index_map):
    return pl.BlockSpec(shape, index_map, pipeline_mode=pl.Buffered(1))


def _mod_kernel(c_ref, w_ref, b_ref, o_ref):
    c = c_ref[...]
    s = c * jax.nn.sigmoid(c)
    o_ref[...] = jnp.dot(s, w_ref[...], precision=HIGHEST, preferred_element_type=F32) + b_ref[...]


def _modulation(c_rows, w_mod, b_mod):
    rows = c_rows.shape[0]
    n_out = w_mod.shape[1]
    tn = 1024
    return pl.pallas_call(
        _mod_kernel,
        grid=(n_out // tn,),
        in_specs=[
            pl.BlockSpec((rows, D_MODEL), lambda j: (0, 0)),
            pl.BlockSpec((D_MODEL, tn), lambda j: (0, j)),
            pl.BlockSpec((1, tn), lambda j: (0, j)),
        ],
        out_specs=pl.BlockSpec((rows, tn), lambda j: (0, j)),
        out_shape=jax.ShapeDtypeStruct((rows, n_out), F32),
        compiler_params=_cparams(("arbitrary",)),
        name="modulation",
    )(c_rows, w_mod, b_mod.reshape(1, n_out))


def _norm_modulate(x, g, shift, scale):
    ms = jnp.mean(x * x, axis=-1, keepdims=True)
    y = x * lax.rsqrt(ms + EPS) * g
    return y * (1.0 + scale) + shift


def _ffn_kernel(x_ref, mod_ref, g_ref, wi_ref, wo_ref, o_ref, *, mod_base, tf):
    x = x_ref[...]
    shift = mod_ref[0, mod_base:mod_base + 1, :]
    scale = mod_ref[0, mod_base + 1:mod_base + 2, :]
    h = _norm_modulate(x, g_ref[...], shift, scale).astype(BF16)
    nj = D_FF // tf

    def gate_up(j):
        gate = jnp.dot(h, wi_ref[:, j * tf:(j + 1) * tf], preferred_element_type=F32)
        up = jnp.dot(h, wi_ref[:, D_FF + j * tf:D_FF + (j + 1) * tf], preferred_element_type=F32)
        return gate, up

    ahead = 2
    pending = {j: gate_up(j) for j in range(ahead)}
    acc = None
    for j in range(nj):
        if j + ahead < nj:
            pending[j + ahead] = gate_up(j + ahead)
        gate, up = pending.pop(j)
        a = (gate * jax.nn.sigmoid(gate) * up).astype(BF16)
        d = jnp.dot(a, wo_ref[j * tf:(j + 1) * tf, :], preferred_element_type=F32)
        acc = d if acc is None else acc + d
    gmod = mod_ref[0, mod_base + 2:mod_base + 3, :]
    o_ref[...] = x + (0.5 * gmod) * acc


def _ffn(x, mod3, tiles_per_mod, g, wi, wo, mod_base, tm, tf):
    t = x.shape[0]
    mod_map = (lambda i: (i // tiles_per_mod, 0, 0)) if tiles_per_mod else (lambda i: (0, 0, 0))
    return pl.pallas_call(
        functools.partial(_ffn_kernel, mod_base=mod_base, tf=tf),
        grid=(t // tm,),
        in_specs=[
            pl.BlockSpec((tm, D_MODEL), lambda i: (i, 0)),
            pl.BlockSpec((1, N_MOD, D_MODEL), mod_map),
            pl.BlockSpec((1, D_MODEL), lambda i: (0, 0)),
            _resident((D_MODEL, 2 * D_FF), lambda i: (0, 0)),
            _resident((D_FF, D_MODEL), lambda i: (0, 0)),
        ],
        out_specs=pl.BlockSpec((tm, D_MODEL), lambda i: (i, 0)),
        out_shape=jax.ShapeDtypeStruct((t, D_MODEL), F32),
        compiler_params=_cparams(("parallel",)),
        name="ffn",
    )(x, mod3, g.reshape(1, D_MODEL), wi, wo)


def _rope_partner(y):
    lane = lax.broadcasted_iota(jnp.int32, (1, LANES), 1)
    first = (lane % 32) < 16
    parts = []
    for c in range(y.shape[1] // LANES):
        yc = y[:, c * LANES:(c + 1) * LANES]
        fwd = pltpu.roll(yc, LANES - 16, axis=1)
        bwd = pltpu.roll(yc, 16, axis=1)
        parts.append(jnp.where(first, fwd, bwd))
    return jnp.concatenate(parts, axis=1)


def _place_halves(x, own_half):
    lane = lax.broadcasted_iota(jnp.int32, (1, LANES), 1)
    in_half = [lane < 64, lane >= 64]
    swapped = pltpu.roll(x, 64, axis=1)
    out = [None, None]
    out[own_half] = jnp.where(in_half[own_half], x, 0.0).astype(BF16)
    out[1 - own_half] = jnp.where(in_half[1 - own_half], swapped, 0.0).astype(BF16)
    return out


def _store_placed_keys(k, dst_ref):
    for g in range(N_KV_HEADS):
        halves = _place_halves(k[:, (g // 2) * LANES:(g // 2 + 1) * LANES], g % 2)
        for a in range(2):
            dst_ref[:, (2 * g + a) * LANES:(2 * g + a + 1) * LANES] = halves[a]


ONES_ROWS = 16
VROWS = HEAD_DIM + ONES_ROWS
VT_ROWS = N_KV_HEADS * VROWS


def _store_values_t(v, dst_ref):
    vt = v.T
    ones = jnp.ones((ONES_ROWS, v.shape[0]), BF16)
    for g in range(N_KV_HEADS):
        dst_ref[g * VROWS:g * VROWS + HEAD_DIM, :] = vt[g * HEAD_DIM:(g + 1) * HEAD_DIM].astype(BF16)
        dst_ref[g * VROWS + HEAD_DIM:(g + 1) * VROWS, :] = ones


def _inproj_kernel(*refs, rope, attn_layout):
    refs = list(refs)
    x_ref, mod_ref, g_ref, wq_ref, wg_ref, qg_ref, kg_ref, bd_ref = refs[:8]
    refs = refs[8:]
    if rope:
        cos_ref, sin_ref = refs[:2]
        refs = refs[2:]
    if attn_layout:
        q_ref, kp_ref, vt_ref, gh_ref = refs
    else:
        qkv_ref, gh_ref = refs

    h = _norm_modulate(x_ref[...], g_ref[...], mod_ref[0, 3:4, :], mod_ref[0, 4:5, :]).astype(BF16)

    def head_norm(t, gain):
        ss = jnp.dot((t * t).astype(BF16), bd_ref[...], preferred_element_type=F32)
        y = t * lax.rsqrt(ss * (1.0 / HEAD_DIM) + EPS) * gain
        if rope:
            y = y * cos_ref[...] + _rope_partner(y) * sin_ref[...]
        return y

    def q_epilogue(acc, c0):
        gain = qg_ref[...] * (LOG2E / math.sqrt(HEAD_DIM))
        for c in range(acc.shape[1] // SUB):
            y = head_norm(acc[:, c * SUB:(c + 1) * SUB], gain)
            cols = slice(c0 + c * SUB, c0 + (c + 1) * SUB)
            if attn_layout:
                q_ref[:, cols] = y.astype(BF16)
            else:
                qkv_ref[:, cols] = y

    def kv_epilogue(acc, c0):
        k = head_norm(acc[:, 0:SUB], kg_ref[...])
        v = acc[:, SUB:2 * SUB]
        if attn_layout:
            _store_placed_keys(k, kp_ref)
            _store_values_t(v, vt_ref)
        else:
            qkv_ref[:, QKV_K:QKV_K + SUB] = k
            qkv_ref[:, QKV_V:QKV_V + SUB] = v

    def gate_epilogue(acc, c0):
        gh_ref[:, c0:c0 + acc.shape[1]] = jax.nn.sigmoid(acc)

    def raw_epilogue(acc, c0):
        gh_ref[:, c0:c0 + acc.shape[1]] = acc

    half = QKV_K // 2
    chunks = [(wq_ref, 0, half, q_epilogue), (wq_ref, half, half, q_epilogue),
              (wq_ref, QKV_K, 2 * SUB, kv_epilogue)]
    for c0 in range(0, GH_COLS, D_MODEL):
        chunks.append((wg_ref, c0, D_MODEL, gate_epilogue if c0 < GH_X1 else raw_epilogue))

    def matmul(idx):
        w_ref, c0, width, _ = chunks[idx]
        return jnp.dot(h, w_ref[:, c0:c0 + width], preferred_element_type=F32)

    ahead = 2
    pending = {idx: matmul(idx) for idx in range(ahead)}
    for idx in range(len(chunks)):
        if idx + ahead < len(chunks):
            pending[idx + ahead] = matmul(idx + ahead)
        chunks[idx][3](pending.pop(idx), chunks[idx][1])


def _inproj(x, mod3, tiles_per_mod, g, wq, wg, qg, kg, bd, rope_tabs, tm, attn_layout):
    t = x.shape[0]
    rope = rope_tabs is not None
    mod_map = (lambda i: (i // tiles_per_mod, 0, 0)) if tiles_per_mod else (lambda i: (0, 0, 0))
    row = lambda i: (i, 0)
    const = lambda i: (0, 0)
    in_specs = [
        pl.BlockSpec((tm, D_MODEL), row),
        pl.BlockSpec((1, N_MOD, D_MODEL), mod_map),
        pl.BlockSpec((1, D_MODEL), const),
        _resident((D_MODEL, QKV_COLS), const),
        _resident((D_MODEL, GH_COLS), const),
        pl.BlockSpec((1, SUB), const),
        pl.BlockSpec((1, SUB), const),
        pl.BlockSpec((SUB, SUB), const),
    ]
    args = [x, mod3, g.reshape(1, D_MODEL), wq, wg, qg, kg, bd]
    if rope:
        seq_tiles = rope_tabs[0].shape[0] // tm
        for tab in rope_tabs:
            in_specs.append(pl.BlockSpec((tm, SUB), lambda i: (i % seq_tiles, 0)))
            args.append(tab)
    gh_spec = pl.BlockSpec((tm, GH_COLS), row)
    gh_shape = jax.ShapeDtypeStruct((t, GH_COLS), F32)
    if attn_layout:
        kwidth = 2 * N_KV_HEADS * LANES
        out_specs = [pl.BlockSpec((tm, QKV_K), row), pl.BlockSpec((tm, kwidth), row),
                     pl.BlockSpec((VT_ROWS, tm), lambda i: (0, i)), gh_spec]
        out_shape = [jax.ShapeDtypeStruct((t, QKV_K), BF16), jax.ShapeDtypeStruct((t, kwidth), BF16),
                     jax.ShapeDtypeStruct((VT_ROWS, t), BF16), gh_shape]
    else:
        out_specs = [pl.BlockSpec((tm, QKV_COLS), row), gh_spec]
        out_shape = [jax.ShapeDtypeStruct((t, QKV_COLS), F32), gh_shape]
    return pl.pallas_call(
        functools.partial(_inproj_kernel, rope=rope, attn_layout=attn_layout),
        grid=(t // tm,),
        in_specs=in_specs,
        out_specs=out_specs,
        out_shape=out_shape,
        compiler_params=_cparams(("parallel",)),
        name="inproj",
    )(*args)


def _col_reduce(x, op):
    rows, cols = x.shape
    part = op(x.reshape(8, rows // 8, cols), axis=0)
    return op(part, axis=0, keepdims=True)


def _attn_kernel(*refs, window, tq, n_qblocks):
    if window:
        (sink_ref, q_ref, kc_ref, vc_ref, kp_ref, kcur_ref, kn_ref,
         vp_ref, vcur_ref, vn_ref, bias_ref, o_ref, kcp_scr, vct_scr) = refs
    else:
        sink_ref, q_ref, kc_ref, vc_ref, o_ref, kcp_scr, vct_scr = refs
    i = pl.program_id(1)

    @pl.when(i == 0)
    def _():
        _store_placed_keys(kc_ref[...], kcp_scr)
        _store_values_t(vc_ref[...], vct_scr)

    if window:
        krow = lax.broadcasted_iota(jnp.int32, (3 * WINDOW, 2 * tq), 0)
        edge = (jnp.where((krow < WINDOW) & (i == 0), NEG_INF, 0.0)
                + jnp.where((krow >= 2 * WINDOW) & (i == n_qblocks - 1), NEG_INF, 0.0))
        bias = bias_ref[...] + edge
    lane = lax.broadcasted_iota(jnp.int32, (1, 2 * tq), 1)
    dn = (((1,), (1,)), ((), ()))
    nkc = kcp_scr.shape[0]
    n_units = 2 * N_KV_HEADS

    def scores(u):
        g, a = divmod(u, 2)
        q0 = q_ref[:, g * 2 * LANES:g * 2 * LANES + LANES]
        q1 = q_ref[:, g * 2 * LANES + LANES:(g + 1) * 2 * LANES]
        qg = jnp.concatenate([q0, q1], axis=0).astype(BF16)
        kcols = slice(u * LANES, (u + 1) * LANES)
        keys = [kcp_scr[:, kcols]]
        if window:
            keys += [kp_ref[:, kcols], kcur_ref[:, kcols], kn_ref[:, kcols]]
        return lax.dot_general(jnp.concatenate(keys, axis=0), qg, dn, preferred_element_type=F32)

    def softmax(u, s):
        g, a = divmod(u, 2)
        sink = jnp.where(lane < tq, sink_ref[4 * g + a], sink_ref[4 * g + 2 + a]) * LOG2E
        s_c = s[0:nkc]
        m = jnp.maximum(_col_reduce(s_c, jnp.max), sink)
        if window:
            s_w = s[nkc:] + bias
            m = jnp.maximum(m, _col_reduce(s_w, jnp.max))
        probs = [jnp.exp2(s_c - m).astype(BF16)]
        if window:
            probs.append(jnp.exp2(s_w - m).astype(BF16))
        return probs, jnp.exp2(sink - m)

    def values(u, probs, sink_term):
        g = u // 2
        vrows = slice(g * VROWS, (g + 1) * VROWS)
        o = jnp.dot(vct_scr[vrows, :], probs[0], preferred_element_type=F32)
        if window:
            vw_t = jnp.concatenate([vp_ref[vrows, :], vcur_ref[vrows, :], vn_ref[vrows, :]], axis=1)
            o = o + jnp.dot(vw_t, probs[1], preferred_element_type=F32)
        l = o[HEAD_DIM:HEAD_DIM + 1, :] + sink_term
        return o[0:HEAD_DIM, :] * (1.0 / l)

    lead = 3
    outs = []
    s_q = {u: scores(u) for u in range(lead)}
    p_q = {}
    for step in range(n_units + 1):
        if step + lead < n_units:
            s_q[step + lead] = scores(step + lead)
        if 0 <= step < n_units:
            p_q[step] = softmax(step, s_q.pop(step))
        u = step - 1
        if 0 <= u < n_units:
            outs.append(values(u, *p_q.pop(u)))
            if u % 2 == 1:
                g = u // 2
                o_t = jnp.concatenate(outs[-2:], axis=0)
                o_ref[:, g * 2 * LANES:g * 2 * LANES + LANES] = o_t[:, 0:tq].T
                o_ref[:, g * 2 * LANES + LANES:(g + 1) * 2 * LANES] = o_t[:, tq:2 * tq].T


def _window_bias_t(tq):
    a = np.arange(WINDOW)[None, :]
    b = np.arange(WINDOW)[:, None]
    prev = np.where(b >= a, 0.0, NEG_INF)
    cur = np.zeros((WINDOW, WINDOW))
    nxt = np.where(b <= a, 0.0, NEG_INF)
    one = np.concatenate([prev, cur, nxt], axis=0).astype(np.float32)
    return jnp.asarray(np.concatenate([one, one], axis=1))


def _attention_ctx(qkv, sink, n_batch, seq):
    t = qkv.shape[0]
    tq = seq
    kblk = QKV_K // SUB
    vblk = QKV_V // SUB
    return pl.pallas_call(
        functools.partial(_attn_kernel, window=False, tq=tq, n_qblocks=1),
        grid=(n_batch, 1),
        in_specs=[
            pl.BlockSpec(memory_space=pltpu.SMEM),
            pl.BlockSpec((tq, D_MODEL), lambda b, i: (b, 0)),
            pl.BlockSpec((seq, SUB), lambda b, i: (b, kblk)),
            pl.BlockSpec((seq, SUB), lambda b, i: (b, vblk)),
        ],
        out_specs=pl.BlockSpec((tq, D_MODEL), lambda b, i: (b, 0)),
        out_shape=jax.ShapeDtypeStruct((t, D_MODEL), F32),
        scratch_shapes=[pltpu.VMEM((seq, 2 * N_KV_HEADS * LANES), BF16), pltpu.VMEM((VT_ROWS, seq), BF16)],
        compiler_params=_cparams(("parallel", "arbitrary")),
        name="attn_ctx",
    )(sink, qkv, qkv, qkv)


def _attention_lat(q, kp, vt, ck, cv, sink, n_batch, seq, past):
    t = q.shape[0]
    tq = WINDOW
    nqb = seq // tq
    kwidth = 2 * N_KV_HEADS * LANES

    def kblock(delta):
        return pl.BlockSpec((tq, kwidth), lambda b, i: (b * nqb + jnp.clip(i + delta, 0, nqb - 1), 0))

    def vblock(delta):
        return pl.BlockSpec((VT_ROWS, tq), lambda b, i: (0, b * nqb + jnp.clip(i + delta, 0, nqb - 1)))

    return pl.pallas_call(
        functools.partial(_attn_kernel, window=True, tq=tq, n_qblocks=nqb),
        grid=(n_batch, nqb),
        in_specs=[
            pl.BlockSpec(memory_space=pltpu.SMEM),
            pl.BlockSpec((tq, D_MODEL), lambda b, i: (b * nqb + i, 0)),
            pl.BlockSpec((past, SUB), lambda b, i: (b, 0)),
            pl.BlockSpec((past, SUB), lambda b, i: (b, 0)),
            kblock(-1), kblock(0), kblock(1),
            vblock(-1), vblock(0), vblock(1),
            pl.BlockSpec((3 * WINDOW, 2 * tq), lambda b, i: (0, 0)),
        ],
        out_specs=pl.BlockSpec((tq, D_MODEL), lambda b, i: (b * nqb + i, 0)),
        out_shape=jax.ShapeDtypeStruct((t, D_MODEL), F32),
        scratch_shapes=[pltpu.VMEM((past, kwidth), BF16), pltpu.VMEM((VT_ROWS, past), BF16)],
        compiler_params=_cparams(("parallel", "arbitrary")),
        name="attn_lat",
    )(sink, q, ck, cv, kp, kp, kp, vt, vt, vt, _window_bias_t(tq))


def _fgen_kernel(z_ref, w1_ref, b1_ref, w2_ref, b2_ref, fr_ref,
                 w3b_ref, b3b_ref, dcb_ref, w3f_ref, b3f_ref, dcf_ref, o_ref, a2_scr, *, n):
    @pl.when((pl.program_id(0) == 0) & (pl.program_id(1) == 0))
    def _():
        fr = fr_ref[...]
        a1 = jnp.sin(fr * (jnp.dot(z_ref[...], w1_ref[...], precision=HIGHEST,
                                   preferred_element_type=F32) + b1_ref[...]))
        a2_scr[...] = jnp.sin(fr * (jnp.dot(a1, w2_ref[...], precision=HIGHEST,
                                            preferred_element_type=F32) + b2_ref[...]))

    tc = o_ref.shape[1]
    tb = jnp.broadcast_to(z_ref[0:n, 0:1], (n, tc))
    tf = jnp.broadcast_to(z_ref[n:2 * n, 0:1], (n, tc))
    hb = (jnp.dot(a2_scr[0:n, :], w3b_ref[...], precision=HIGHEST, preferred_element_type=F32)
          + b3b_ref[...]) * jnp.exp(-tb * jnp.abs(dcb_ref[...]))
    hf = (jnp.dot(a2_scr[n:2 * n, :], w3f_ref[...], precision=HIGHEST, preferred_element_type=F32)
          + b3f_ref[...]) * jnp.exp(-tf * jnp.abs(dcf_ref[...]))
    tot = (jnp.sum(jnp.abs(hb), axis=0, keepdims=True)
           + jnp.sum(jnp.abs(hf), axis=0, keepdims=True))
    inv = 1.0 / (tot + EPS)
    rowid = lax.broadcasted_iota(jnp.int32, (n, tc), 0)
    o_ref[0:n, :] = jnp.where(rowid == 0, 0.0, hb * inv)
    o_ref[n:2 * n, :] = hf * inv


def _filter_gen(n, filt_w1, filt_b1, filt_w2, filt_b2, filt_w3, filt_b3, filt_freq, filt_decay, tc):
    t = np.arange(n, dtype=np.float64) / max(n - 1, 1)
    bands = np.arange(1, N_BANDS + 1, dtype=np.float64)
    ang = 2.0 * math.pi * t[:, None] * bands[None, :]
    z = np.concatenate([t[:, None], np.cos(ang), np.sin(ang)], axis=-1)
    zb = np.concatenate([z[0:1], z[1:][::-1]], axis=0)
    zfull = jnp.asarray(np.pad(np.concatenate([zb, z], axis=0), ((0, 0), (0, LANES - FILTER_EMB))), dtype=F32)
    padw = LANES - FILTER_WIDTH
    w1 = jnp.pad(filt_w1.astype(F32), ((0, LANES - FILTER_EMB), (0, padw)))
    b1 = jnp.pad(filt_b1.astype(F32), (0, padw)).reshape(1, LANES)
    w2 = jnp.pad(filt_w2.astype(F32), ((0, padw), (0, padw)))
    b2 = jnp.pad(filt_b2.astype(F32), (0, padw)).reshape(1, LANES)
    fr = jnp.pad(filt_freq.astype(F32), (0, padw)).reshape(1, LANES)
    w3 = jnp.pad(filt_w3.astype(F32), ((0, padw), (0, 0)))
    ncol = w3.shape[1]
    b3 = filt_b3.astype(F32).reshape(1, ncol)
    dc = filt_decay.astype(F32).reshape(1, ncol)
    ct = D_HYENA // tc
    full = lambda shape: pl.BlockSpec(shape, lambda o, c: (0, 0))
    bwd = lambda rows: pl.BlockSpec((rows, tc), lambda o, c: (0, (2 * o + 1) * ct + c))
    fwd = lambda rows: pl.BlockSpec((rows, tc), lambda o, c: (0, (2 * o) * ct + c))
    return pl.pallas_call(
        functools.partial(_fgen_kernel, n=n),
        grid=(2, ct),
        in_specs=[
            full((2 * n, LANES)), full((LANES, LANES)), full((1, LANES)),
            full((LANES, LANES)), full((1, LANES)), full((1, LANES)),
            bwd(LANES), bwd(1), bwd(1), fwd(LANES), fwd(1), fwd(1),
        ],
        out_specs=pl.BlockSpec((2 * n, tc), lambda o, c: (0, o * ct + c)),
        out_shape=jax.ShapeDtypeStruct((2 * n, 2 * D_HYENA), F32),
        scratch_shapes=[pltpu.VMEM((2 * n, LANES), F32)],
        compiler_params=_cparams(("arbitrary", "arbitrary")),
        name="hyena_filter_gen",
    )(zfull, w1, b1, w2, b2, fr, w3, b3, dc, w3, b3, dc)


def _dft_mats(blk):
    f = np.arange(blk, dtype=np.int64)[:, None]
    s = np.arange(blk, dtype=np.int64)[None, :]
    theta = (np.pi / (2 * blk)) * (((2 * f + 1) * s) % (4 * blk)).astype(np.float64)
    fwd = np.concatenate([np.cos(theta), -np.sin(theta)], axis=0)
    inv = np.concatenate([np.cos(theta).T, -np.sin(theta).T], axis=1) / blk
    return (jnp.asarray(fwd, dtype=F32).astype(BF16), jnp.asarray(inv, dtype=F32).astype(BF16))


def _ftf_kernel(k_ref, skip_ref, f_ref, o_ref, *, nblk2, blk):
    tc = o_ref.shape[2]
    fmat = f_ref[...]
    fidx = lax.broadcasted_iota(jnp.int32, (blk, tc), 0)
    sgn = jnp.where(fidx % 2 == 0, 1.0, -1.0)
    prev = None
    for e in range(nblk2):
        p = jnp.dot(fmat, k_ref[e * blk:(e + 1) * blk, :].astype(BF16), preferred_element_type=F32)
        if e >= 1:
            re = p[0:blk] - sgn * prev[blk:2 * blk]
            im = p[blk:2 * blk] + sgn * prev[0:blk]
            if e == nblk2 // 2:
                re = re + skip_ref[...]
            o_ref[e - 1, 0:blk, :] = re
            o_ref[e - 1, blk:2 * blk, :] = im
        prev = p


def _filter_transform(kfull, skip, fmat, n, blk, tc):
    nblk2 = 2 * n // blk
    nd = nblk2 - 1
    ct = D_HYENA // tc
    return pl.pallas_call(
        functools.partial(_ftf_kernel, nblk2=nblk2, blk=blk),
        grid=(2, ct),
        in_specs=[
            pl.BlockSpec((2 * n, tc), lambda o, c: (0, o * ct + c)),
            pl.BlockSpec((None, 1, tc), lambda o, c: (o, 0, c)),
            pl.BlockSpec((2 * blk, blk), lambda o, c: (0, 0)),
        ],
        out_specs=pl.BlockSpec((nd, 2 * blk, tc), lambda o, c: (o, 0, c)),
        out_shape=jax.ShapeDtypeStruct((2 * nd, 2 * blk, D_HYENA), F32),
        compiler_params=_cparams(("parallel", "parallel")),
        name="hyena_filter_transform",
    )(kfull, skip, fmat)


CHUNK = 32
HALO = 8


def _short_conv_block(src_ref, b, j, n, blk, w_ref, b_ref):
    nblk = n // blk
    lo = max(j * blk - HALO, 0)
    hi = min((j + 1) * blk + HALO, n)
    off = j * blk - lo
    win = src_ref[b, lo:hi, :]
    prev = pltpu.roll(win, 1, axis=0)[off:off + blk]
    nxt = pltpu.roll(win, hi - lo - 1, axis=0)[off:off + blk]
    u = win[off:off + blk]
    rowid = lax.broadcasted_iota(jnp.int32, u.shape, 0)
    if j == 0:
        prev = jnp.where(rowid == 0, 0.0, prev)
    if j == nblk - 1:
        nxt = jnp.where(rowid == blk - 1, 0.0, nxt)
    return prev * w_ref[0:1, :] + u * w_ref[1:2, :] + nxt * w_ref[2:3, :] + b_ref[...]


def _conv_kernel(zin_ref, gin_ref, wz_ref, bz_ref, wg_ref, bg_ref, g_ref, f_ref, fi_ref,
                 o_ref, zf_scr, yf_scr, *, n, blk, bb, conv_z):
    nblk = n // blk
    tc = o_ref.shape[2]
    fmat = f_ref[...]
    for j in range(nblk):
        rows = slice(j * blk, (j + 1) * blk)
        zs = []
        for b in range(bb):
            if conv_z:
                z = _short_conv_block(zin_ref, b, j, n, blk, wz_ref, bz_ref)
            else:
                z = zin_ref[b, rows, :]
            zs.append(z.astype(BF16))
        zf_scr[j] = jnp.dot(fmat, jnp.concatenate(zs, axis=1), preferred_element_type=F32)
        for b in range(bb):
            o_ref[b, rows, :] = _short_conv_block(gin_ref, b, j, n, blk, wg_ref, bg_ref)

    def pairs(i, slot):
        for r in range(blk // CHUNK):
            re_rows = slice(r * CHUNK, (r + 1) * CHUNK)
            im_rows = slice(blk + r * CHUNK, blk + (r + 1) * CHUNK)
            acc_re = [jnp.zeros((CHUNK, tc), F32) for _ in range(bb)]
            acc_im = [jnp.zeros((CHUNK, tc), F32) for _ in range(bb)]
            for j in range(nblk):
                d = i - j + (nblk - 1)
                g_re = g_ref[d, re_rows, :]
                g_im = g_ref[d, im_rows, :]
                for b in range(bb):
                    cols = slice(b * tc, (b + 1) * tc)
                    z_re = zf_scr[j, re_rows, cols]
                    z_im = zf_scr[j, im_rows, cols]
                    acc_re[b] = acc_re[b] + (g_re * z_re - g_im * z_im)
                    acc_im[b] = acc_im[b] + (g_re * z_im + g_im * z_re)
            for b in range(bb):
                cols = slice(b * tc, (b + 1) * tc)
                yf_scr[slot, re_rows, cols] = acc_re[b].astype(BF16)
                yf_scr[slot, im_rows, cols] = acc_im[b].astype(BF16)

    def inverse(slot):
        return jnp.dot(fi_ref[...], yf_scr[slot], preferred_element_type=F32)

    def gate_out(i, y):
        rows = pl.ds(pl.multiple_of(i * blk, blk), blk)
        for b in range(bb):
            o_ref[b, rows, :] = o_ref[b, rows, :] * y[:, b * tc:(b + 1) * tc]

    pairs(0, 0)
    if nblk > 1:
        def body(i, carry):
            y = inverse((i - 1) & 1)
            pairs(i, i & 1)
            gate_out(i - 1, y)
            return carry
        lax.fori_loop(1, nblk, body, 0)
    gate_out(nblk - 1, inverse((nblk - 1) & 1))


def _long_conv(zsrc, zcol, gsrc, gcol, conv_w, conv_b, gspec, order, fmat, fimat,
               n_batch, n, blk, tc, bb, conv_z):
    nblk = n // blk
    nd = 2 * nblk - 1
    ct = D_HYENA // tc
    hy0 = GH_X1 // tc
    wz = (zcol - hy0) if conv_z else 0
    wg = gcol - hy0
    return pl.pallas_call(
        functools.partial(_conv_kernel, n=n, blk=blk, bb=bb, conv_z=conv_z),
        grid=(ct, n_batch // bb),
        in_specs=[
            pl.BlockSpec((bb, n, tc), lambda c, b: (b, 0, zcol + c)),
            pl.BlockSpec((bb, n, tc), lambda c, b: (b, 0, gcol + c)),
            pl.BlockSpec((3, tc), lambda c, b: (0, wz + c)),
            pl.BlockSpec((1, tc), lambda c, b: (0, wz + c)),
            pl.BlockSpec((3, tc), lambda c, b: (0, wg + c)),
            pl.BlockSpec((1, tc), lambda c, b: (0, wg + c)),
            _resident((nd, 2 * blk, tc), lambda c, b: (order, 0, c)),
            _resident((2 * blk, blk), lambda c, b: (0, 0)),
            _resident((blk, 2 * blk), lambda c, b: (0, 0)),
        ],
        out_specs=pl.BlockSpec((bb, n, tc), lambda c, b: (b, 0, c)),
        out_shape=jax.ShapeDtypeStruct((n_batch, n, D_HYENA), F32),
        scratch_shapes=[
            pltpu.VMEM((nblk, 2 * blk, bb * tc), F32),
            pltpu.VMEM((2, 2 * blk, bb * tc), BF16),
        ],
        compiler_params=_cparams(("parallel", "arbitrary")),
        name="hyena_conv",
    )(zsrc, gsrc, conv_w, conv_b, conv_w, conv_b, gspec, fmat, fimat)


def _hyena(gh, n_batch, n, blk, tc, bb, conv_w, conv_b, skip, filt):
    fmat, fimat = _dft_mats(blk)
    kfull = _filter_gen(n, *filt, tc=128)
    sk = skip.astype(F32).reshape(2, 1, D_HYENA)
    gspec = _filter_transform(kfull, sk, fmat, n, blk, tc=256)
    cw = conv_w.astype(F32)
    cb = conv_b.astype(F32).reshape(1, -1)
    gh3 = gh.reshape(n_batch, n, GH_COLS)
    z2 = _long_conv(gh3, GH_HV // tc, gh3, GH_X1 // tc, cw, cb, gspec, 0, fmat, fimat,
                    n_batch, n, blk, tc, bb, conv_z=True)
    y = _long_conv(z2, 0, gh3, GH_X2 // tc, cw, cb, gspec, 1, fmat, fimat,
                   n_batch, n, blk, tc, bb, conv_z=False)
    return y.reshape(n_batch * n, D_HYENA)


def _outproj_kernel(x_ref, attn_ref, hy_ref, ga_ref, gh_ref, mod_ref, wa_ref, wh_ref, wo_ref, o_ref):
    n_chunks = 2
    rc = x_ref.shape[0] // n_chunks

    def branches(c):
        rows = slice(c * rc, (c + 1) * rc)
        a = jnp.dot(attn_ref[rows, :].astype(BF16), wa_ref[...], preferred_element_type=F32)
        h = jnp.dot(hy_ref[rows, :].astype(BF16), wh_ref[...], preferred_element_type=F32)
        return a, h

    pending = {0: branches(0)}
    for c in range(n_chunks):
        if c + 1 < n_chunks:
            pending[c + 1] = branches(c + 1)
        a, h = pending.pop(c)
        rows = slice(c * rc, (c + 1) * rc)
        merged = ga_ref[rows, :] * a + gh_ref[rows, :] * h
        out = jnp.dot(merged.astype(BF16), wo_ref[...], preferred_element_type=F32)
        o_ref[rows, :] = x_ref[rows, :] + mod_ref[0, 5:6, :] * out


def _outproj(x, attn, hy, gh, mod3, tiles_per_mod, wa, wh, wo, tm):
    t = x.shape[0]
    mod_map = (lambda i: (i // tiles_per_mod, 0, 0)) if tiles_per_mod else (lambda i: (0, 0, 0))
    row = lambda i: (i, 0)
    wspec = _resident((D_MODEL, D_MODEL), lambda i: (0, 0))
    return pl.pallas_call(
        _outproj_kernel,
        grid=(t // tm,),
        in_specs=[
            pl.BlockSpec((tm, D_MODEL), row),
            pl.BlockSpec((tm, D_MODEL), row),
            pl.BlockSpec((tm, D_MODEL), row),
            pl.BlockSpec((tm, D_MODEL), lambda i: (i, GH_GA // D_MODEL)),
            pl.BlockSpec((tm, D_MODEL), lambda i: (i, GH_GH // D_MODEL)),
            pl.BlockSpec((1, N_MOD, D_MODEL), mod_map),
            wspec, wspec, wspec,
        ],
        out_specs=pl.BlockSpec((tm, D_MODEL), row),
        out_shape=jax.ShapeDtypeStruct((t, D_MODEL), F32),
        compiler_params=_cparams(("parallel",)),
        name="mixer_out",
    )(x, attn, hy, gh, gh, mod3, wa, wh, wo)


def _rope_tables(n):
    pos = np.arange(n)
    row = (pos // GRID_W).astype(np.float64)
    col = (pos % GRID_W).astype(np.float64)
    n_freq = HEAD_DIM // 4
    inv = ROPE_BASE ** (-np.arange(n_freq, dtype=np.float64) / n_freq)
    ar = row[:, None] * inv[None, :]
    ac = col[:, None] * inv[None, :]
    cos = np.concatenate([np.cos(ar), np.cos(ar), np.cos(ac), np.cos(ac)], axis=-1)
    sin = np.concatenate([-np.sin(ar), np.sin(ar), -np.sin(ac), np.sin(ac)], axis=-1)
    reps = SUB // HEAD_DIM
    return (jnp.asarray(np.tile(cos, (1, reps)), dtype=F32), jnp.asarray(np.tile(sin, (1, reps)), dtype=F32))


def _layer(x, mod3, per_batch_mod, n_batch, seq, w, ctx_kv, hy_blk, hy_tc, hy_bb):
    tmf = 1024
    ffn_tiles_per_mod = (seq // tmf) if per_batch_mod else 0
    x = _ffn(x, mod3, ffn_tiles_per_mod, w["norm_ffn1"], w["ffn1_wi"], w["ffn1_wo"], 0, tmf, 256)
    qkv = None
    tmi = 512
    inproj_tiles_per_mod = (seq // tmi) if per_batch_mod else 0
    if ctx_kv is None:
        qkv, gh = _inproj(x, mod3, inproj_tiles_per_mod, w["norm_mix"], w["w_qkv"], w["w_gh"], w["qg"],
                          w["kg"], w["bd"], None, tmi, attn_layout=False)
        attn = _attention_ctx(qkv, w["sink"], n_batch, seq)
    else:
        q, kp, vt, gh = _inproj(x, mod3, inproj_tiles_per_mod, w["norm_mix"], w["w_qkv"], w["w_gh"],
                                w["qg"], w["kg"], w["bd"], _rope_tables(seq), tmi, attn_layout=True)
        attn = _attention_lat(q, kp, vt, ctx_kv[0], ctx_kv[1], w["sink"], n_batch, seq,
                              ctx_kv[0].shape[0] // n_batch)
    hy = _hyena(gh, n_batch, seq, hy_blk, hy_tc, hy_bb, w["conv_w"], w["conv_b"], w["hyena_skip"], w["filt"])
    tmo = 512
    x = _outproj(x, attn, hy, gh, mod3, (seq // tmo) if per_batch_mod else 0,
                 w["wa"], w["wh"], w["wo"], tmo)
    x = _ffn(x, mod3, ffn_tiles_per_mod, w["norm_ffn2"], w["ffn2_wi"], w["ffn2_wo"], 6, tmf, 256)
    return x, qkv


def kernel(x_prompt, x_sample, cache_k, cache_v, c, c_ctx, w_mod, b_mod, norm_ffn1, ffn1_wi, ffn1_wo, norm_mix, w_in, q_norm, k_norm, attn_sink, conv_w, conv_b, filt_w1, filt_b1, filt_w2, filt_b2, filt_w3, filt_b3, filt_freq, filt_decay, hyena_skip, w_attn_branch, w_hyena_branch, w_out, norm_ffn2, ffn2_wi, ffn2_wo):
    batch, seq, _ = x_prompt.shape
    dec_batch, dec_seq, _ = x_sample.shape
    depth = w_mod.shape[0]
    past = cache_k.shape[2]

    yp = x_prompt.reshape(batch * seq, D_MODEL)
    ys = x_sample.reshape(dec_batch * dec_seq, D_MODEL)
    bd = jnp.asarray(np.kron(np.eye(SUB // HEAD_DIM), np.ones((HEAD_DIM, HEAD_DIM))), dtype=F32).astype(BF16)
    new_ks, new_vs = [], []
    for l in range(depth):
        c_rows = jnp.zeros((16, D_MODEL), F32).at[0:dec_batch].set(c).at[dec_batch].set(c_ctx)
        mod = _modulation(c_rows, w_mod[l], b_mod[l]).reshape(16, N_MOD, D_MODEL)
        mod_lat = mod[0:dec_batch]
        mod_ctx = mod[dec_batch:dec_batch + 1]
        wl = w_in[l]
        hy_end = QKV_COLS + 3 * D_HYENA
        w = {
            "norm_ffn1": norm_ffn1[l], "ffn1_wi": ffn1_wi[l].astype(BF16), "ffn1_wo": ffn1_wo[l].astype(BF16),
            "norm_mix": norm_mix[l],
            "w_qkv": wl[:, 0:QKV_COLS].astype(BF16),
            "w_gh": jnp.concatenate([wl[:, hy_end:], wl[:, QKV_COLS:hy_end]], axis=1).astype(BF16),
            "qg": jnp.tile(q_norm[l], SUB // HEAD_DIM).reshape(1, SUB),
            "kg": jnp.tile(k_norm[l], SUB // HEAD_DIM).reshape(1, SUB),
            "bd": bd, "sink": attn_sink[l],
            "conv_w": conv_w[l], "conv_b": conv_b[l], "hyena_skip": hyena_skip[l],
            "filt": (filt_w1[l], filt_b1[l], filt_w2[l], filt_b2[l], filt_w3[l], filt_b3[l],
                     filt_freq[l], filt_decay[l]),
            "wa": w_attn_branch[l].astype(BF16), "wh": w_hyena_branch[l].astype(BF16),
            "wo": w_out[l].astype(BF16),
            "norm_ffn2": norm_ffn2[l], "ffn2_wi": ffn2_wi[l].astype(BF16), "ffn2_wo": ffn2_wo[l].astype(BF16),
        }
        yp, qkv_p = _layer(yp, mod_ctx, False, batch, seq, w, None, hy_blk=256, hy_tc=128, hy_bb=8)
        ck = cache_k[:, l].reshape(dec_batch * past, N_KV_HEADS * HEAD_DIM)
        cv = cache_v[:, l].reshape(dec_batch * past, N_KV_HEADS * HEAD_DIM)
        ys, _ = _layer(ys, mod_lat, True, dec_batch, dec_seq, w, (ck, cv), hy_blk=512, hy_tc=128, hy_bb=2)
        new_ks.append(qkv_p[:, QKV_K:QKV_K + SUB].reshape(batch, seq, N_KV_HEADS, HEAD_DIM))
        new_vs.append(qkv_p[:, QKV_V:QKV_V + SUB].reshape(batch, seq, N_KV_HEADS, HEAD_DIM))
    new_k = jnp.stack(new_ks, axis=1)
    new_v = jnp.stack(new_vs, axis=1)
    return (yp.reshape(batch, seq, D_MODEL), ys.reshape(dec_batch, dec_seq, D_MODEL), new_k, new_v)
```

```python
import functools
import math

import numpy as np
import jax
import jax.numpy as jnp
from jax import lax
from jax.experimental import pallas as pl
from jax.experimental.pallas import tpu as pltpu

F32 = jnp.float32
BF16 = jnp.bfloat16
HIGHEST = lax.Precision.HIGHEST

D_MODEL = 1024
N_HEADS = 16
N_KV_HEADS = 4
HEAD_DIM = 64
GRID_W = 64
WINDOW = 128
ROPE_BASE = 10000.0
D_HYENA = 1024
N_BANDS = 16
FILTER_EMB = 1 + 2 * N_BANDS
FILTER_WIDTH = 64
D_FF = 2816
N_MOD = 9
EPS = 1e-6
NEG_INF = -1e30
LOG2E = math.log2(math.e)

QKV_COLS = 1536
QKV_K = 1024
QKV_V = 1280
GH_COLS = 5120
GH_GA = 0
GH_GH = 1024
GH_X1 = 2048
GH_X2 = 3072
GH_HV = 4096
SUB = 256

LANES = 128
VMEM_LIMIT = 61 * 1024 * 1024


def _cparams(sem):
    return pltpu.CompilerParams(dimension_semantics=sem, vmem_limit_bytes=VMEM_LIMIT)


def _resident(shape, index_map):
    return pl.BlockSpec(shape, index_map, pipeline_mode=pl.Buffered(1))


def _mod_kernel(c_ref, w_ref, b_ref, o_ref):
    c = c_ref[...]
    s = c * jax.nn.sigmoid(c)
    o_ref[...] = jnp.dot(s, w_ref[...], precision=HIGHEST, preferred_element_type=F32) + b_ref[...]


def _modulation(c_rows, w_mod, b_mod):
    rows = c_rows.shape[0]
    n_out = w_mod.shape[1]
    tn = 1024
    return pl.pallas_call(
        _mod_kernel,
        grid=(n_out // tn,),
        in_specs=[
            pl.BlockSpec((rows, D_MODEL), lambda j: (0, 0)),
            pl.BlockSpec((D_MODEL, tn), lambda j: (0, j)),
            pl.BlockSpec((1, tn), lambda j: (0, j)),
        ],
        out_specs=pl.BlockSpec((rows, tn), lambda j: (0, j)),
        out_shape=jax.ShapeDtypeStruct((rows, n_out), F32),
        compiler_params=_cparams(("arbitrary",)),
        name="modulation",
    )(c_rows, w_mod, b_mod.reshape(1, n_out))


def _norm_modulate(x, g, shift, scale):
    ms = jnp.mean(x * x, axis=-1, keepdims=True)
    y = x * lax.rsqrt(ms + EPS) * g
    return y * (1.0 + scale) + shift


def _ffn_kernel(x_ref, mod_ref, g_ref, wi_ref, wo_ref, o_ref, *, mod_base, tf):
    x = x_ref[...]
    shift = mod_ref[0, mod_base:mod_base + 1, :]
    scale = mod_ref[0, mod_base + 1:mod_base + 2, :]
    h = _norm_modulate(x, g_ref[...], shift, scale).astype(BF16)
    nj = D_FF // tf

    def gate_up(j):
        gate = jnp.dot(h, wi_ref[:, j * tf:(j + 1) * tf], preferred_element_type=F32)
        up = jnp.dot(h, wi_ref[:, D_FF + j * tf:D_FF + (j + 1) * tf], preferred_element_type=F32)
        return gate, up

    ahead = 2
    pending = {j: gate_up(j) for j in range(ahead)}
    acc = None
    for j in range(nj):
        if j + ahead < nj:
            pending[j + ahead] = gate_up(j + ahead)
        gate, up = pending.pop(j)
        a = (gate * jax.nn.sigmoid(gate) * up).astype(BF16)
        d = jnp.dot(a, wo_ref[j * tf:(j + 1) * tf, :], preferred_element_type=F32)
        acc = d if acc is None else acc + d
    gmod = mod_ref[0, mod_base + 2:mod_base + 3, :]
    o_ref[...] = x + (0.5 * gmod) * acc


def _ffn(x, mod3, tiles_per_mod, g, wi, wo, mod_base, tm, tf):
    t = x.shape[0]
    mod_map = (lambda i: (i // tiles_per_mod, 0, 0)) if tiles_per_mod else (lambda i: (0, 0, 0))
    return pl.pallas_call(
        functools.partial(_ffn_kernel, mod_base=mod_base, tf=tf),
        grid=(t // tm,),
        in_specs=[
            pl.BlockSpec((tm, D_MODEL), lambda i: (i, 0)),
            pl.BlockSpec((1, N_MOD, D_MODEL), mod_map),
            pl.BlockSpec((1, D_MODEL), lambda i: (0, 0)),
            _resident((D_MODEL, 2 * D_FF), lambda i: (0, 0)),
            _resident((D_FF, D_MODEL), lambda i: (0, 0)),
        ],
        out_specs=pl.BlockSpec((tm, D_MODEL), lambda i: (i, 0)),
        out_shape=jax.ShapeDtypeStruct((t, D_MODEL), F32),
        compiler_params=_cparams(("parallel",)),
        name="ffn",
    )(x, mod3, g.reshape(1, D_MODEL), wi, wo)


def _rope_partner(y):
    lane = lax.broadcasted_iota(jnp.int32, (1, LANES), 1)
    first = (lane % 32) < 16
    parts = []
    for c in range(y.shape[1] // LANES):
        yc = y[:, c * LANES:(c + 1) * LANES]
        fwd = pltpu.roll(yc, LANES - 16, axis=1)
        bwd = pltpu.roll(yc, 16, axis=1)
        parts.append(jnp.where(first, fwd, bwd))
    return jnp.concatenate(parts, axis=1)


def _place_halves(x, own_half):
    lane = lax.broadcasted_iota(jnp.int32, (1, LANES), 1)
    in_half = [lane < 64, lane >= 64]
    swapped = pltpu.roll(x, 64, axis=1)
    out = [None, None]
    out[own_half] = jnp.where(in_half[own_half], x, 0.0).astype(BF16)
    out[1 - own_half] = jnp.where(in_half[1 - own_half], swapped, 0.0).astype(BF16)
    return out


def _store_placed_keys(k, dst_ref):
    for g in range(N_KV_HEADS):
        halves = _place_halves(k[:, (g // 2) * LANES:(g // 2 + 1) * LANES], g % 2)
        for a in range(2):
            dst_ref[:, (2 * g + a) * LANES:(2 * g + a + 1) * LANES] = halves[a]


ONES_ROWS = 16
VROWS = HEAD_DIM + ONES_ROWS
VT_ROWS = N_KV_HEADS * VROWS


def _store_values_t(v, dst_ref):
    vt = v.T
    ones = jnp.ones((ONES_ROWS, v.shape[0]), BF16)
    for g in range(N_KV_HEADS):
        dst_ref[g * VROWS:g * VROWS + HEAD_DIM, :] = vt[g * HEAD_DIM:(g + 1) * HEAD_DIM].astype(BF16)
        dst_ref[g * VROWS + HEAD_DIM:(g + 1) * VROWS, :] = ones


def _inproj_kernel(*refs, rope, attn_layout):
    refs = list(refs)
    x_ref, mod_ref, g_ref, wq_ref, wg_ref, qg_ref, kg_ref, bd_ref = refs[:8]
    refs = refs[8:]
    if rope:
        cos_ref, sin_ref = refs[:2]
        refs = refs[2:]
    if attn_layout:
        q_ref, kp_ref, vt_ref, gh_ref = refs
    else:
        qkv_ref, gh_ref = refs

    h = _norm_modulate(x_ref[...], g_ref[...], mod_ref[0, 3:4, :], mod_ref[0, 4:5, :]).astype(BF16)

    def head_norm(t, gain):
        ss = jnp.dot((t * t).astype(BF16), bd_ref[...], preferred_element_type=F32)
        y = t * lax.rsqrt(ss * (1.0 / HEAD_DIM) + EPS) * gain
        if rope:
            y = y * cos_ref[...] + _rope_partner(y) * sin_ref[...]
        return y

    def q_epilogue(acc, c0):
        gain = qg_ref[...] * (LOG2E / math.sqrt(HEAD_DIM))
        for c in range(acc.shape[1] // SUB):
            y = head_norm(acc[:, c * SUB:(c + 1) * SUB], gain)
            cols = slice(c0 + c * SUB, c0 + (c + 1) * SUB)
            if attn_layout:
                q_ref[:, cols] = y.astype(BF16)
            else:
                qkv_ref[:, cols] = y

    def kv_epilogue(acc, c0):
        k = head_norm(acc[:, 0:SUB], kg_ref[...])
        v = acc[:, SUB:2 * SUB]
        if attn_layout:
            _store_placed_keys(k, kp_ref)
            _store_values_t(v, vt_ref)
        else:
            qkv_ref[:, QKV_K:QKV_K + SUB] = k
            qkv_ref[:, QKV_V:QKV_V + SUB] = v

    def gate_epilogue(acc, c0):
        gh_ref[:, c0:c0 + acc.shape[1]] = jax.nn.sigmoid(acc)

    def raw_epilogue(acc, c0):
        gh_ref[:, c0:c0 + acc.shape[1]] = acc

    half = QKV_K // 2
    chunks = [(wq_ref, 0, half, q_epilogue), (wq_ref, half, half, q_epilogue),
              (wq_ref, QKV_K, 2 * SUB, kv_epilogue)]
    for c0 in range(0, GH_COLS, D_MODEL):
        chunks.append((wg_ref, c0, D_MODEL, gate_epilogue if c0 < GH_X1 else raw_epilogue))

    def matmul(idx):
        w_ref, c0, width, _ = chunks[idx]
        return jnp.dot(h, w_ref[:, c0:c0 + width], preferred_element_type=F32)

    ahead = 2
    pending = {idx: matmul(idx) for idx in range(ahead)}
    for idx in range(len(chunks)):
        if idx + ahead < len(chunks):
            pending[idx + ahead] = matmul(idx + ahead)
        chunks[idx][3](pending.pop(idx), chunks[idx][1])


def _inproj(x, mod3, tiles_per_mod, g, wq, wg, qg, kg, bd, rope_tabs, tm, attn_layout):
    t = x.shape[0]
    rope = rope_tabs is not None
    mod_map = (lambda i: (i // tiles_per_mod, 0, 0)) if tiles_per_mod else (lambda i: (0, 0, 0))
    row = lambda i: (i, 0)
    const = lambda i: (0, 0)
    in_specs = [
        pl.BlockSpec((tm, D_MODEL), row),
        pl.BlockSpec((1, N_MOD, D_MODEL), mod_map),
        pl.BlockSpec((1, D_MODEL), const),
        _resident((D_MODEL, QKV_COLS), const),
        _resident((D_MODEL, GH_COLS), const),
        pl.BlockSpec((1, SUB), const),
        pl.BlockSpec((1, SUB), const),
        pl.BlockSpec((SUB, SUB), const),
    ]
    args = [x, mod3, g.reshape(1, D_MODEL), wq, wg, qg, kg, bd]
    if rope:
        seq_tiles = rope_tabs[0].shape[0] // tm
        for tab in rope_tabs:
            in_specs.append(pl.BlockSpec((tm, SUB), lambda i: (i % seq_tiles, 0)))
            args.append(tab)
    gh_spec = pl.BlockSpec((tm, GH_COLS), row)
    gh_shape = jax.ShapeDtypeStruct((t, GH_COLS), F32)
    if attn_layout:
        kwidth = 2 * N_KV_HEADS * LANES
        out_specs = [pl.BlockSpec((tm, QKV_K), row), pl.BlockSpec((tm, kwidth), row),
                     pl.BlockSpec((VT_ROWS, tm), lambda i: (0, i)), gh_spec]
        out_shape = [jax.ShapeDtypeStruct((t, QKV_K), BF16), jax.ShapeDtypeStruct((t, kwidth), BF16),
                     jax.ShapeDtypeStruct((VT_ROWS, t), BF16), gh_shape]
    else:
        out_specs = [pl.BlockSpec((tm, QKV_COLS), row), gh_spec]
        out_shape = [jax.ShapeDtypeStruct((t, QKV_COLS), F32), gh_shape]
    return pl.pallas_call(
        functools.partial(_inproj_kernel, rope=rope, attn_layout=attn_layout),
        grid=(t // tm,),
        in_specs=in_specs,
        out_specs=out_specs,
        out_shape=out_shape,
        compiler_params=_cparams(("parallel",)),
        name="inproj",
    )(*args)


def _col_reduce(x, op):
    rows, cols = x.shape
    part = op(x.reshape(8, rows // 8, cols), axis=0)
    return op(part, axis=0, keepdims=True)


def _attn_kernel(*refs, window, tq, n_qblocks):
    if window:
        (sink_ref, q_ref, kc_ref, vc_ref, kp_ref, kcur_ref, kn_ref,
         vp_ref, vcur_ref, vn_ref, bias_ref, o_ref, kcp_scr, vct_scr) = refs
    else:
        sink_ref, q_ref, kc_ref, vc_ref, o_ref, kcp_scr, vct_scr = refs
    i = pl.program_id(1)

    @pl.when(i == 0)
    def _():
        _store_placed_keys(kc_ref[...], kcp_scr)
        _store_values_t(vc_ref[...], vct_scr)

    if window:
        krow = lax.broadcasted_iota(jnp.int32, (3 * WINDOW, 2 * tq), 0)
        edge = (jnp.where((krow < WINDOW) & (i == 0), NEG_INF, 0.0)
                + jnp.where((krow >= 2 * WINDOW) & (i == n_qblocks - 1), NEG_INF, 0.0))
        bias = bias_ref[...] + edge
    lane = lax.broadcasted_iota(jnp.int32, (1, 2 * tq), 1)
    dn = (((1,), (1,)), ((), ()))
    nkc = kcp_scr.shape[0]
    n_units = 2 * N_KV_HEADS

    def scores(u):
        g, a = divmod(u, 2)
        q0 = q_ref[:, g * 2 * LANES:g * 2 * LANES + LANES]
        q1 = q_ref[:, g * 2 * LANES + LANES:(g + 1) * 2 * LANES]
        qg = jnp.concatenate([q0, q1], axis=0).astype(BF16)
        kcols = slice(u * LANES, (u + 1) * LANES)
        keys = [kcp_scr[:, kcols]]
        if window:
            keys += [kp_ref[:, kcols], kcur_ref[:, kcols], kn_ref[:, kcols]]
        return lax.dot_general(jnp.concatenate(keys, axis=0), qg, dn, preferred_element_type=F32)

    def softmax(u, s):
        g, a = divmod(u, 2)
        sink = jnp.where(lane < tq, sink_ref[4 * g + a], sink_ref[4 * g + 2 + a]) * LOG2E
        s_c = s[0:nkc]
        m = jnp.maximum(_col_reduce(s_c, jnp.max), sink)
        if window:
            s_w = s[nkc:] + bias
            m = jnp.maximum(m, _col_reduce(s_w, jnp.max))
        probs = [jnp.exp2(s_c - m).astype(BF16)]
        if window:
            probs.append(jnp.exp2(s_w - m).astype(BF16))
        return probs, jnp.exp2(sink - m)

    def values(u, probs, sink_term):
        g = u // 2
        vrows = slice(g * VROWS, (g + 1) * VROWS)
        o = jnp.dot(vct_scr[vrows, :], probs[0], preferred_element_type=F32)
        if window:
            vw_t = jnp.concatenate([vp_ref[vrows, :], vcur_ref[vrows, :], vn_ref[vrows, :]], axis=1)
            o = o + jnp.dot(vw_t, probs[1], preferred_element_type=F32)
        l = o[HEAD_DIM:HEAD_DIM + 1, :] + sink_term
        return o[0:HEAD_DIM, :] * (1.0 / l)

    lead = 3
    outs = []
    s_q = {u: scores(u) for u in range(lead)}
    p_q = {}
    for step in range(n_units + 1):
        if step + lead < n_units:
            s_q[step + lead] = scores(step + lead)
        if 0 <= step < n_units:
            p_q[step] = softmax(step, s_q.pop(step))
        u = step - 1
        if 0 <= u < n_units:
            outs.append(values(u, *p_q.pop(u)))
            if u % 2 == 1:
                g = u // 2
                o_t = jnp.concatenate(outs[-2:], axis=0)
                o_ref[:, g * 2 * LANES:g * 2 * LANES + LANES] = o_t[:, 0:tq].T.astype(BF16)
                o_ref[:, g * 2 * LANES + LANES:(g + 1) * 2 * LANES] = o_t[:, tq:2 * tq].T.astype(BF16)


def _window_bias_t(tq):
    a = np.arange(WINDOW)[None, :]
    b = np.arange(WINDOW)[:, None]
    prev = np.where(b >= a, 0.0, NEG_INF)
    cur = np.zeros((WINDOW, WINDOW))
    nxt = np.where(b <= a, 0.0, NEG_INF)
    one = np.concatenate([prev, cur, nxt], axis=0).astype(np.float32)
    return jnp.asarray(np.concatenate([one, one], axis=1))


def _attention_ctx(qkv, sink, n_batch, seq):
    t = qkv.shape[0]
    tq = seq
    kblk = QKV_K // SUB
    vblk = QKV_V // SUB
    return pl.pallas_call(
        functools.partial(_attn_kernel, window=False, tq=tq, n_qblocks=1),
        grid=(n_batch, 1),
        in_specs=[
            pl.BlockSpec(memory_space=pltpu.SMEM),
            pl.BlockSpec((tq, D_MODEL), lambda b, i: (b, 0)),
            pl.BlockSpec((seq, SUB), lambda b, i: (b, kblk)),
            pl.BlockSpec((seq, SUB), lambda b, i: (b, vblk)),
        ],
        out_specs=pl.BlockSpec((tq, D_MODEL), lambda b, i: (b, 0)),
        out_shape=jax.ShapeDtypeStruct((t, D_MODEL), BF16),
        scratch_shapes=[pltpu.VMEM((seq, 2 * N_KV_HEADS * LANES), BF16), pltpu.VMEM((VT_ROWS, seq), BF16)],
        compiler_params=_cparams(("parallel", "arbitrary")),
        name="attn_ctx",
    )(sink, qkv, qkv, qkv)


def _attention_lat(q, kp, vt, ck, cv, sink, n_batch, seq, past):
    t = q.shape[0]
    tq = WINDOW
    nqb = seq // tq
    kwidth = 2 * N_KV_HEADS * LANES

    def kblock(delta):
        return pl.BlockSpec((tq, kwidth), lambda b, i: (b * nqb + jnp.clip(i + delta, 0, nqb - 1), 0))

    def vblock(delta):
        return pl.BlockSpec((VT_ROWS, tq), lambda b, i: (0, b * nqb + jnp.clip(i + delta, 0, nqb - 1)))

    return pl.pallas_call(
        functools.partial(_attn_kernel, window=True, tq=tq, n_qblocks=nqb),
        grid=(n_batch, nqb),
        in_specs=[
            pl.BlockSpec(memory_space=pltpu.SMEM),
            pl.BlockSpec((tq, D_MODEL), lambda b, i: (b * nqb + i, 0)),
            pl.BlockSpec((past, SUB), lambda b, i: (b, 0)),
            pl.BlockSpec((past, SUB), lambda b, i: (b, 0)),
            kblock(-1), kblock(0), kblock(1),
            vblock(-1), vblock(0), vblock(1),
            pl.BlockSpec((3 * WINDOW, 2 * tq), lambda b, i: (0, 0)),
        ],
        out_specs=pl.BlockSpec((tq, D_MODEL), lambda b, i: (b * nqb + i, 0)),
        out_shape=jax.ShapeDtypeStruct((t, D_MODEL), BF16),
        scratch_shapes=[pltpu.VMEM((past, kwidth), BF16), pltpu.VMEM((VT_ROWS, past), BF16)],
        compiler_params=_cparams(("parallel", "arbitrary")),
        name="attn_lat",
    )(sink, q, ck, cv, kp, kp, kp, vt, vt, vt, _window_bias_t(tq))


def _fgen_kernel(z_ref, w1_ref, b1_ref, w2_ref, b2_ref, fr_ref,
                 w3b_ref, b3b_ref, dcb_ref, w3f_ref, b3f_ref, dcf_ref, o_ref, a2_scr, *, n):
    @pl.when((pl.program_id(0) == 0) & (pl.program_id(1) == 0))
    def _():
        fr = fr_ref[...]
        a1 = jnp.sin(fr * (jnp.dot(z_ref[...], w1_ref[...], precision=HIGHEST,
                                   preferred_element_type=F32) + b1_ref[...]))
        a2_scr[...] = jnp.sin(fr * (jnp.dot(a1, w2_ref[...], precision=HIGHEST,
                                            preferred_element_type=F32) + b2_ref[...]))

    tc = o_ref.shape[1]
    tf = jnp.broadcast_to(z_ref[:, 0:1], (n, tc))
    tb = jnp.broadcast_to(z_ref[:, FILTER_WIDTH:FILTER_WIDTH + 1], (n, tc))
    a2 = a2_scr[...]
    hb = (jnp.dot(a2, w3b_ref[...], precision=HIGHEST, preferred_element_type=F32)
          + b3b_ref[...]) * jnp.exp(-tb * jnp.abs(dcb_ref[...]))
    hf = (jnp.dot(a2, w3f_ref[...], precision=HIGHEST, preferred_element_type=F32)
          + b3f_ref[...]) * jnp.exp(-tf * jnp.abs(dcf_ref[...]))
    tot = (jnp.sum(jnp.abs(hb), axis=0, keepdims=True)
           + jnp.sum(jnp.abs(hf), axis=0, keepdims=True))
    inv = 1.0 / (tot + EPS)
    rowid = lax.broadcasted_iota(jnp.int32, (n, tc), 0)
    o_ref[0:n, :] = jnp.where(rowid == 0, 0.0, hb * inv)
    o_ref[n:2 * n, :] = hf * inv


def _filter_gen(n, filt_w1, filt_b1, filt_w2, filt_b2, filt_w3, filt_b3, filt_freq, filt_decay, tc):
    t = np.arange(n, dtype=np.float64) / max(n - 1, 1)
    bands = np.arange(1, N_BANDS + 1, dtype=np.float64)
    ang = 2.0 * math.pi * t[:, None] * bands[None, :]
    z = np.concatenate([t[:, None], np.cos(ang), np.sin(ang)], axis=-1)
    zb = np.concatenate([z[0:1], z[1:][::-1]], axis=0)
    fw = FILTER_WIDTH
    zpad = np.zeros((n, fw - FILTER_EMB))
    zpacked = jnp.asarray(np.concatenate([z, zpad, zb, zpad], axis=1), dtype=F32)
    w1p = jnp.pad(filt_w1.astype(F32), ((0, fw - FILTER_EMB), (0, 0)))
    zero = jnp.zeros((fw, fw), F32)
    w1 = jnp.block([[w1p, zero], [zero, w1p]])
    w2f = filt_w2.astype(F32)
    w2 = jnp.block([[w2f, zero], [zero, w2f]])
    twice = lambda v: jnp.tile(v.astype(F32), 2).reshape(1, LANES)
    b1, b2, fr = twice(filt_b1), twice(filt_b2), twice(filt_freq)
    w3 = filt_w3.astype(F32)
    ncol = w3.shape[1]
    w3_fwd = jnp.concatenate([w3, jnp.zeros_like(w3)], axis=0)
    w3_bwd = jnp.concatenate([jnp.zeros_like(w3), w3], axis=0)
    b3 = filt_b3.astype(F32).reshape(1, ncol)
    dc = filt_decay.astype(F32).reshape(1, ncol)
    ct = D_HYENA // tc
    full = lambda shape: pl.BlockSpec(shape, lambda o, c: (0, 0))
    bwd = lambda rows: pl.BlockSpec((rows, tc), lambda o, c: (0, (2 * o + 1) * ct + c))
    fwd = lambda rows: pl.BlockSpec((rows, tc), lambda o, c: (0, (2 * o) * ct + c))
    return pl.pallas_call(
        functools.partial(_fgen_kernel, n=n),
        grid=(2, ct),
        in_specs=[
            full((n, LANES)), full((LANES, LANES)), full((1, LANES)),
            full((LANES, LANES)), full((1, LANES)), full((1, LANES)),
            bwd(LANES), bwd(1), bwd(1), fwd(LANES), fwd(1), fwd(1),
        ],
        out_specs=pl.BlockSpec((2 * n, tc), lambda o, c: (0, o * ct + c)),
        out_shape=jax.ShapeDtypeStruct((2 * n, 2 * D_HYENA), F32),
        scratch_shapes=[pltpu.VMEM((n, LANES), F32)],
        compiler_params=_cparams(("arbitrary", "arbitrary")),
        name="hyena_filter_gen",
    )(zpacked, w1, b1, w2, b2, fr, w3_bwd, b3, dc, w3_fwd, b3, dc)


def _dft_mats(blk):
    f = np.arange(blk, dtype=np.int64)[:, None]
    s = np.arange(blk, dtype=np.int64)[None, :]
    theta = (np.pi / (2 * blk)) * (((2 * f + 1) * s) % (4 * blk)).astype(np.float64)
    fwd = np.concatenate([np.cos(theta), -np.sin(theta)], axis=0)
    inv = np.concatenate([np.cos(theta).T, -np.sin(theta).T], axis=1) / blk
    return (jnp.asarray(fwd, dtype=F32).astype(BF16), jnp.asarray(inv, dtype=F32).astype(BF16))


def _ftf_kernel(k_ref, skip_ref, f_ref, o_ref, *, nblk2, blk):
    tc = o_ref.shape[2]
    fmat = f_ref[...]
    fidx = lax.broadcasted_iota(jnp.int32, (blk, tc), 0)
    sgn = jnp.where(fidx % 2 == 0, 1.0, -1.0)
    prev = None
    for e in range(nblk2):
        p = jnp.dot(fmat, k_ref[e * blk:(e + 1) * blk, :].astype(BF16), preferred_element_type=F32)
        if e >= 1:
            re = p[0:blk] - sgn * prev[blk:2 * blk]
            im = p[blk:2 * blk] + sgn * prev[0:blk]
            if e == nblk2 // 2:
                re = re + skip_ref[...]
            o_ref[e - 1, 0:blk, :] = re
            o_ref[e - 1, blk:2 * blk, :] = im
        prev = p


def _filter_transform(kfull, skip, fmat, n, blk, tc):
    nblk2 = 2 * n // blk
    nd = nblk2 - 1
    ct = D_HYENA // tc
    return pl.pallas_call(
        functools.partial(_ftf_kernel, nblk2=nblk2, blk=blk),
        grid=(2, ct),
        in_specs=[
            pl.BlockSpec((2 * n, tc), lambda o, c: (0, o * ct + c)),
            pl.BlockSpec((None, 1, tc), lambda o, c: (o, 0, c)),
            pl.BlockSpec((2 * blk, blk), lambda o, c: (0, 0)),
        ],
        out_specs=pl.BlockSpec((nd, 2 * blk, tc), lambda o, c: (o, 0, c)),
        out_shape=jax.ShapeDtypeStruct((2 * nd, 2 * blk, D_HYENA), F32),
        compiler_params=_cparams(("parallel", "parallel")),
        name="hyena_filter_transform",
    )(kfull, skip, fmat)


CHUNK = 32
HALO = 8
SC_ROWS = 128


def _short_conv_block(src_ref, b, j, n, blk, w_ref, b_ref):
    pieces = []
    for r0 in range(j * blk, (j + 1) * blk, SC_ROWS):
        lo = max(r0 - HALO, 0)
        hi = min(r0 + SC_ROWS + HALO, n)
        off = r0 - lo
        win = src_ref[b, lo:hi, :]
        prev = pltpu.roll(win, 1, axis=0)[off:off + SC_ROWS]
        nxt = pltpu.roll(win, hi - lo - 1, axis=0)[off:off + SC_ROWS]
        u = win[off:off + SC_ROWS]
        rowid = lax.broadcasted_iota(jnp.int32, u.shape, 0)
        if r0 == 0:
            prev = jnp.where(rowid == 0, 0.0, prev)
        if r0 + SC_ROWS == n:
            nxt = jnp.where(rowid == SC_ROWS - 1, 0.0, nxt)
        pieces.append(prev * w_ref[0:1, :] + u * w_ref[1:2, :] + nxt * w_ref[2:3, :] + b_ref[...])
    return jnp.concatenate(pieces, axis=0)


def _conv_kernel(zin_ref, gin_ref, wz_ref, bz_ref, wg_ref, bg_ref, g_ref, f_ref, fi_ref,
                 o_ref, zf_scr, yf_scr, *, n, blk, bb, conv_z):
    nblk = n // blk
    tc = o_ref.shape[2]
    fmat = f_ref[...]
    for j in range(nblk):
        rows = slice(j * blk, (j + 1) * blk)
        zs = []
        for b in range(bb):
            if conv_z:
                z = _short_conv_block(zin_ref, b, j, n, blk, wz_ref, bz_ref)
            else:
                z = zin_ref[b, rows, :]
            zs.append(z.astype(BF16))
        zf_scr[j] = jnp.dot(fmat, jnp.concatenate(zs, axis=1), preferred_element_type=F32)
        for b in range(bb):
            o_ref[b, rows, :] = _short_conv_block(gin_ref, b, j, n, blk, wg_ref, bg_ref)

    def pairs(i, slot):
        for r in range(blk // CHUNK):
            re_rows = slice(r * CHUNK, (r + 1) * CHUNK)
            im_rows = slice(blk + r * CHUNK, blk + (r + 1) * CHUNK)
            acc_re = [jnp.zeros((CHUNK, tc), F32) for _ in range(bb)]
            acc_im = [jnp.zeros((CHUNK, tc), F32) for _ in range(bb)]
            for j in range(nblk):
                d = i - j + (nblk - 1)
                g_re = g_ref[d, re_rows, :]
                g_im = g_ref[d, im_rows, :]
                for b in range(bb):
                    cols = slice(b * tc, (b + 1) * tc)
                    z_re = zf_scr[j, re_rows, cols]
                    z_im = zf_scr[j, im_rows, cols]
                    acc_re[b] = acc_re[b] + (g_re * z_re - g_im * z_im)
                    acc_im[b] = acc_im[b] + (g_re * z_im + g_im * z_re)
            for b in range(bb):
                cols = slice(b * tc, (b + 1) * tc)
                yf_scr[slot, re_rows, cols] = acc_re[b].astype(BF16)
                yf_scr[slot, im_rows, cols] = acc_im[b].astype(BF16)

    def inverse(slot):
        return jnp.dot(fi_ref[...], yf_scr[slot], preferred_element_type=F32)

    def gate_out(i, y):
        rows = pl.ds(pl.multiple_of(i * blk, blk), blk)
        for b in range(bb):
            o_ref[b, rows, :] = o_ref[b, rows, :] * y[:, b * tc:(b + 1) * tc]

    pairs(0, 0)
    if nblk > 1:
        def body(i, carry):
            y = inverse((i - 1) & 1)
            pairs(i, i & 1)
            gate_out(i - 1, y)
            return carry
        lax.fori_loop(1, nblk, body, 0)
    gate_out(nblk - 1, inverse((nblk - 1) & 1))


def _long_conv(zsrc, zcol, gsrc, gcol, conv_w, conv_b, gspec, order, fmat, fimat,
               n_batch, n, blk, tc, bb, conv_z):
    nblk = n // blk
    nd = 2 * nblk - 1
    ct = D_HYENA // tc
    hy0 = GH_X1 // tc
    wz = (zcol - hy0) if conv_z else 0
    wg = gcol - hy0
    return pl.pallas_call(
        functools.partial(_conv_kernel, n=n, blk=blk, bb=bb, conv_z=conv_z),
        grid=(ct, n_batch // bb),
        in_specs=[
            pl.BlockSpec((bb, n, tc), lambda c, b: (b, 0, zcol + c)),
            pl.BlockSpec((bb, n, tc), lambda c, b: (b, 0, gcol + c)),
            pl.BlockSpec((3, tc), lambda c, b: (0, wz + c)),
            pl.BlockSpec((1, tc), lambda c, b: (0, wz + c)),
            pl.BlockSpec((3, tc), lambda c, b: (0, wg + c)),
            pl.BlockSpec((1, tc), lambda c, b: (0, wg + c)),
            _resident((nd, 2 * blk, tc), lambda c, b: (order, 0, c)),
            _resident((2 * blk, blk), lambda c, b: (0, 0)),
            _resident((blk, 2 * blk), lambda c, b: (0, 0)),
        ],
        out_specs=pl.BlockSpec((bb, n, tc), lambda c, b: (b, 0, c)),
        out_shape=jax.ShapeDtypeStruct((n_batch, n, D_HYENA), F32),
        scratch_shapes=[
            pltpu.VMEM((nblk, 2 * blk, bb * tc), F32),
            pltpu.VMEM((2, 2 * blk, bb * tc), BF16),
        ],
        compiler_params=_cparams(("parallel", "arbitrary")),
        name="hyena_conv",
    )(zsrc, gsrc, conv_w, conv_b, conv_w, conv_b, gspec, fmat, fimat)


def _hyena(gh, n_batch, n, blk, tc, bb, conv_w, conv_b, skip, filt):
    fmat, fimat = _dft_mats(blk)
    kfull = _filter_gen(n, *filt, tc=128)
    sk = skip.astype(F32).reshape(2, 1, D_HYENA)
    gspec = _filter_transform(kfull, sk, fmat, n, blk, tc=256)
    cw = conv_w.astype(F32)
    cb = conv_b.astype(F32).reshape(1, -1)
    gh3 = gh.reshape(n_batch, n, GH_COLS)
    z2 = _long_conv(gh3, GH_HV // tc, gh3, GH_X1 // tc, cw, cb, gspec, 0, fmat, fimat,
                    n_batch, n, blk, tc, bb, conv_z=True)
    y = _long_conv(z2, 0, gh3, GH_X2 // tc, cw, cb, gspec, 1, fmat, fimat,
                   n_batch, n, blk, tc, bb, conv_z=False)
    return y.reshape(n_batch * n, D_HYENA)


def _outproj_kernel(x_ref, attn_ref, hy_ref, ga_ref, gh_ref, mod_ref, wa_ref, wh_ref, wo_ref, o_ref):
    n_chunks = 2
    rc = x_ref.shape[0] // n_chunks

    def branches(c):
        rows = slice(c * rc, (c + 1) * rc)
        a = jnp.dot(attn_ref[rows, :].astype(BF16), wa_ref[...], preferred_element_type=F32)
        h = jnp.dot(hy_ref[rows, :].astype(BF16), wh_ref[...], preferred_element_type=F32)
        return a, h

    pending = {0: branches(0)}
    for c in range(n_chunks):
        if c + 1 < n_chunks:
            pending[c + 1] = branches(c + 1)
        a, h = pending.pop(c)
        rows = slice(c * rc, (c + 1) * rc)
        merged = ga_ref[rows, :] * a + gh_ref[rows, :] * h
        out = jnp.dot(merged.astype(BF16), wo_ref[...], preferred_element_type=F32)
        o_ref[rows, :] = x_ref[rows, :] + mod_ref[0, 5:6, :] * out


def _outproj(x, attn, hy, gh, mod3, tiles_per_mod, wa, wh, wo, tm):
    t = x.shape[0]
    mod_map = (lambda i: (i // tiles_per_mod, 0, 0)) if tiles_per_mod else (lambda i: (0, 0, 0))
    row = lambda i: (i, 0)
    wspec = _resident((D_MODEL, D_MODEL), lambda i: (0, 0))
    return pl.pallas_call(
        _outproj_kernel,
        grid=(t // tm,),
        in_specs=[
            pl.BlockSpec((tm, D_MODEL), row),
            pl.BlockSpec((tm, D_MODEL), row),
            pl.BlockSpec((tm, D_MODEL), row),
            pl.BlockSpec((tm, D_MODEL), lambda i: (i, GH_GA // D_MODEL)),
            pl.BlockSpec((tm, D_MODEL), lambda i: (i, GH_GH // D_MODEL)),
            pl.BlockSpec((1, N_MOD, D_MODEL), mod_map),
            wspec, wspec, wspec,
        ],
        out_specs=pl.BlockSpec((tm, D_MODEL), row),
        out_shape=jax.ShapeDtypeStruct((t, D_MODEL), F32),
        compiler_params=_cparams(("parallel",)),
        name="mixer_out",
    )(x, attn, hy, gh, gh, mod3, wa, wh, wo)


def _rope_tables(n):
    pos = np.arange(n)
    row = (pos // GRID_W).astype(np.float64)
    col = (pos % GRID_W).astype(np.float64)
    n_freq = HEAD_DIM // 4
    inv = ROPE_BASE ** (-np.arange(n_freq, dtype=np.float64) / n_freq)
    ar = row[:, None] * inv[None, :]
    ac = col[:, None] * inv[None, :]
    cos = np.concatenate([np.cos(ar), np.cos(ar), np.cos(ac), np.cos(ac)], axis=-1)
    sin = np.concatenate([-np.sin(ar), np.sin(ar), -np.sin(ac), np.sin(ac)], axis=-1)
    reps = SUB // HEAD_DIM
    return (jnp.asarray(np.tile(cos, (1, reps)), dtype=F32), jnp.asarray(np.tile(sin, (1, reps)), dtype=F32))


def _layer(x, mod3, per_batch_mod, n_batch, seq, w, ctx_kv, hy_blk, hy_tc, hy_bb):
    tmf = 1024
    ffn_tiles_per_mod = (seq // tmf) if per_batch_mod else 0
    x = _ffn(x, mod3, ffn_tiles_per_mod, w["norm_ffn1"], w["ffn1_wi"], w["ffn1_wo"], 0, tmf, 256)
    qkv = None
    tmi = 512
    inproj_tiles_per_mod = (seq // tmi) if per_batch_mod else 0
    if ctx_kv is None:
        qkv, gh = _inproj(x, mod3, inproj_tiles_per_mod, w["norm_mix"], w["w_qkv"], w["w_gh"], w["qg"],
                          w["kg"], w["bd"], None, tmi, attn_layout=False)
        attn = _attention_ctx(qkv, w["sink"], n_batch, seq)
    else:
        q, kp, vt, gh = _inproj(x, mod3, inproj_tiles_per_mod, w["norm_mix"], w["w_qkv"], w["w_gh"],
                                w["qg"], w["kg"], w["bd"], _rope_tables(seq), tmi, attn_layout=True)
        attn = _attention_lat(q, kp, vt, ctx_kv[0], ctx_kv[1], w["sink"], n_batch, seq,
                              ctx_kv[0].shape[0] // n_batch)
    hy = _hyena(gh, n_batch, seq, hy_blk, hy_tc, hy_bb, w["conv_w"], w["conv_b"], w["hyena_skip"], w["filt"])
    tmo = 512
    x = _outproj(x, attn, hy, gh, mod3, (seq // tmo) if per_batch_mod else 0,
                 w["wa"], w["wh"], w["wo"], tmo)
    x = _ffn(x, mod3, ffn_tiles_per_mod, w["norm_ffn2"], w["ffn2_wi"], w["ffn2_wo"], 6, tmf, 256)
    return x, qkv


def kernel(x_prompt, x_sample, cache_k, cache_v, c, c_ctx, w_mod, b_mod, norm_ffn1, ffn1_wi, ffn1_wo, norm_mix, w_in, q_norm, k_norm, attn_sink, conv_w, conv_b, filt_w1, filt_b1, filt_w2, filt_b2, filt_w3, filt_b3, filt_freq, filt_decay, hyena_skip, w_attn_branch, w_hyena_branch, w_out, norm_ffn2, ffn2_wi, ffn2_wo):
    batch, seq, _ = x_prompt.shape
    dec_batch, dec_seq, _ = x_sample.shape
    depth = w_mod.shape[0]
    past = cache_k.shape[2]

    yp = x_prompt.reshape(batch * seq, D_MODEL)
    ys = x_sample.reshape(dec_batch * dec_seq, D_MODEL)
    bd = jnp.asarray(np.kron(np.eye(SUB // HEAD_DIM), np.ones((HEAD_DIM, HEAD_DIM))), dtype=F32).astype(BF16)
    new_ks, new_vs = [], []
    for l in range(depth):
        c_rows = jnp.zeros((16, D_MODEL), F32).at[0:dec_batch].set(c).at[dec_batch].set(c_ctx)
        mod = _modulation(c_rows, w_mod[l], b_mod[l]).reshape(16, N_MOD, D_MODEL)
        mod_lat = mod[0:dec_batch]
        mod_ctx = mod[dec_batch:dec_batch + 1]
        wl = w_in[l]
        hy_end = QKV_COLS + 3 * D_HYENA
        w = {
            "norm_ffn1": norm_ffn1[l], "ffn1_wi": ffn1_wi[l].astype(BF16), "ffn1_wo": ffn1_wo[l].astype(BF16),
            "norm_mix": norm_mix[l],
            "w_qkv": wl[:, 0:QKV_COLS].astype(BF16),
            "w_gh": jnp.concatenate([wl[:, hy_end:], wl[:, QKV_COLS:hy_end]], axis=1).astype(BF16),
            "qg": jnp.tile(q_norm[l], SUB // HEAD_DIM).reshape(1, SUB),
            "kg": jnp.tile(k_norm[l], SUB // HEAD_DIM).reshape(1, SUB),
            "bd": bd, "sink": attn_sink[l],
            "conv_w": conv_w[l], "conv_b": conv_b[l], "hyena_skip": hyena_skip[l],
            "filt": (filt_w1[l], filt_b1[l], filt_w2[l], filt_b2[l], filt_w3[l], filt_b3[l],
                     filt_freq[l], filt_decay[l]),
            "wa": w_attn_branch[l].astype(BF16), "wh": w_hyena_branch[l].astype(BF16),
            "wo": w_out[l].astype(BF16),
            "norm_ffn2": norm_ffn2[l], "ffn2_wi": ffn2_wi[l].astype(BF16), "ffn2_wo": ffn2_wo[l].astype(BF16),
        }
        yp, qkv_p = _layer(yp, mod_ctx, False, batch, seq, w, None, hy_blk=256, hy_tc=128, hy_bb=8)
        ck = cache_k[:, l].reshape(dec_batch * past, N_KV_HEADS * HEAD_DIM)
        cv = cache_v[:, l].reshape(dec_batch * past, N_KV_HEADS * HEAD_DIM)
        ys, _ = _layer(ys, mod_lat, True, dec_batch, dec_seq, w, (ck, cv), hy_blk=512, hy_tc=128, hy_bb=2)
        new_ks.append(qkv_p[:, QKV_K:QKV_K + SUB].reshape(batch, seq, N_KV_HEADS, HEAD_DIM))
        new_vs.append(qkv_p[:, QKV_V:QKV_V + SUB].reshape(batch, seq, N_KV_HEADS, HEAD_DIM))
    new_k = jnp.stack(new_ks, axis=1)
    new_v = jnp.stack(new_vs, axis=1)
    return (yp.reshape(batch, seq, D_MODEL), ys.reshape(dec_batch, dec_seq, D_MODEL), new_k, new_v)
```

```python
import functools
import math

import numpy as np
import jax
import jax.numpy as jnp
from jax import lax
from jax.experimental import pallas as pl
from jax.experimental.pallas import tpu as pltpu

F32 = jnp.float32
BF16 = jnp.bfloat16
HIGHEST = lax.Precision.HIGHEST

D_MODEL = 1024
N_HEADS = 16
N_KV_HEADS = 4
HEAD_DIM = 64
GRID_W = 64
WINDOW = 128
ROPE_BASE = 10000.0
D_HYENA = 1024
N_BANDS = 16
FILTER_EMB = 1 + 2 * N_BANDS
FILTER_WIDTH = 64
D_FF = 2816
N_MOD = 9
EPS = 1e-6
NEG_INF = -1e30
LOG2E = math.log2(math.e)

QKV_COLS = 1536
QKV_K = 1024
QKV_V = 1280
GH_COLS = 5120
GH_GA = 0
GH_GH = 1024
GH_X1 = 2048
GH_X2 = 3072
GH_HV = 4096
SUB = 256

LANES = 128
VMEM_LIMIT = 61 * 1024 * 1024


def _cparams(sem):
    return pltpu.CompilerParams(dimension_semantics=sem, vmem_limit_bytes=VMEM_LIMIT)


def _resident(shape, index_map):
    return pl.BlockSpec(shape, index_map, pipeline_mode=pl.Buffered(1))


def _mod_kernel(c_ref, w_ref, b_ref, o_ref):
    c = c_ref[...]
    s = c * jax.nn.sigmoid(c)
    o_ref[...] = jnp.dot(s, w_ref[...], precision=HIGHEST, preferred_element_type=F32) + b_ref[...]


def _modulation(c_rows, w_mod, b_mod):
    rows = c_rows.shape[0]
    n_out = w_mod.shape[1]
    tn = 1024
    return pl.pallas_call(
        _mod_kernel,
        grid=(n_out // tn,),
        in_specs=[
            pl.BlockSpec((rows, D_MODEL), lambda j: (0, 0)),
            pl.BlockSpec((D_MODEL, tn), lambda j: (0, j)),
            pl.BlockSpec((1, tn), lambda j: (0, j)),
        ],
        out_specs=pl.BlockSpec((rows, tn), lambda j: (0, j)),
        out_shape=jax.ShapeDtypeStruct((rows, n_out), F32),
        compiler_params=_cparams(("arbitrary",)),
        name="modulation",
    )(c_rows, w_mod, b_mod.reshape(1, n_out))


def _norm_modulate(x, g, shift, scale):
    ms = jnp.mean(x * x, axis=-1, keepdims=True)
    y = x * lax.rsqrt(ms + EPS) * g
    return y * (1.0 + scale) + shift


def _ffn_kernel(x_ref, mod_ref, g_ref, wi_ref, wo_ref, o_ref, *, mod_base, tf):
    x = x_ref[...]
    shift = mod_ref[0, mod_base:mod_base + 1, :]
    scale = mod_ref[0, mod_base + 1:mod_base + 2, :]
    h = _norm_modulate(x, g_ref[...], shift, scale).astype(BF16)
    nj = D_FF // tf

    def gate_up(j):
        gate = jnp.dot(h, wi_ref[:, j * tf:(j + 1) * tf], preferred_element_type=F32)
        up = jnp.dot(h, wi_ref[:, D_FF + j * tf:D_FF + (j + 1) * tf], preferred_element_type=F32)
        return gate, up

    ahead = 2
    pending = {j: gate_up(j) for j in range(ahead)}
    acc = None
    for j in range(nj):
        if j + ahead < nj:
            pending[j + ahead] = gate_up(j + ahead)
        gate, up = pending.pop(j)
        a = (gate * jax.nn.sigmoid(gate) * up).astype(BF16)
        d = jnp.dot(a, wo_ref[j * tf:(j + 1) * tf, :], preferred_element_type=F32)
        acc = d if acc is None else acc + d
    gmod = mod_ref[0, mod_base + 2:mod_base + 3, :]
    o_ref[...] = x + (0.5 * gmod) * acc


def _ffn(x, mod3, tiles_per_mod, g, wi, wo, mod_base, tm, tf):
    t = x.shape[0]
    mod_map = (lambda i: (i // tiles_per_mod, 0, 0)) if tiles_per_mod else (lambda i: (0, 0, 0))
    return pl.pallas_call(
        functools.partial(_ffn_kernel, mod_base=mod_base, tf=tf),
        grid=(t // tm,),
        in_specs=[
            pl.BlockSpec((tm, D_MODEL), lambda i: (i, 0)),
            pl.BlockSpec((1, N_MOD, D_MODEL), mod_map),
            pl.BlockSpec((1, D_MODEL), lambda i: (0, 0)),
            _resident((D_MODEL, 2 * D_FF), lambda i: (0, 0)),
            _resident((D_FF, D_MODEL), lambda i: (0, 0)),
        ],
        out_specs=pl.BlockSpec((tm, D_MODEL), lambda i: (i, 0)),
        out_shape=jax.ShapeDtypeStruct((t, D_MODEL), F32),
        compiler_params=_cparams(("parallel",)),
        name="ffn",
    )(x, mod3, g.reshape(1, D_MODEL), wi, wo)


def _rope_partner(y):
    lane = lax.broadcasted_iota(jnp.int32, (1, LANES), 1)
    first = (lane % 32) < 16
    parts = []
    for c in range(y.shape[1] // LANES):
        yc = y[:, c * LANES:(c + 1) * LANES]
        fwd = pltpu.roll(yc, LANES - 16, axis=1)
        bwd = pltpu.roll(yc, 16, axis=1)
        parts.append(jnp.where(first, fwd, bwd))
    return jnp.concatenate(parts, axis=1)


def _place_halves(x, own_half):
    lane = lax.broadcasted_iota(jnp.int32, (1, LANES), 1)
    in_half = [lane < 64, lane >= 64]
    swapped = pltpu.roll(x, 64, axis=1)
    out = [None, None]
    out[own_half] = jnp.where(in_half[own_half], x, 0.0).astype(BF16)
    out[1 - own_half] = jnp.where(in_half[1 - own_half], swapped, 0.0).astype(BF16)
    return out


def _store_placed_keys(k, dst_ref):
    for g in range(N_KV_HEADS):
        halves = _place_halves(k[:, (g // 2) * LANES:(g // 2 + 1) * LANES], g % 2)
        for a in range(2):
            dst_ref[:, (2 * g + a) * LANES:(2 * g + a + 1) * LANES] = halves[a]


ONES_ROWS = 16
VROWS = HEAD_DIM + ONES_ROWS
VT_ROWS = N_KV_HEADS * VROWS


def _store_values_t(v, dst_ref):
    vt = v.T
    ones = jnp.ones((ONES_ROWS, v.shape[0]), BF16)
    for g in range(N_KV_HEADS):
        dst_ref[g * VROWS:g * VROWS + HEAD_DIM, :] = vt[g * HEAD_DIM:(g + 1) * HEAD_DIM].astype(BF16)
        dst_ref[g * VROWS + HEAD_DIM:(g + 1) * VROWS, :] = ones


def _inproj_kernel(*refs, rope, attn_layout):
    refs = list(refs)
    x_ref, mod_ref, g_ref, wq_ref, wg_ref, qg_ref, kg_ref, bd_ref = refs[:8]
    refs = refs[8:]
    if rope:
        cos_ref, sin_ref = refs[:2]
        refs = refs[2:]
    if attn_layout:
        q_ref, kp_ref, vt_ref, gh_ref = refs
    else:
        qkv_ref, gh_ref = refs

    h = _norm_modulate(x_ref[...], g_ref[...], mod_ref[0, 3:4, :], mod_ref[0, 4:5, :]).astype(BF16)

    def head_norm(t, gain):
        ss = jnp.dot((t * t).astype(BF16), bd_ref[...], preferred_element_type=F32)
        y = t * lax.rsqrt(ss * (1.0 / HEAD_DIM) + EPS) * gain
        if rope:
            y = y * cos_ref[...] + _rope_partner(y) * sin_ref[...]
        return y

    def q_epilogue(acc, c0):
        gain = qg_ref[...] * (LOG2E / math.sqrt(HEAD_DIM))
        for c in range(acc.shape[1] // SUB):
            y = head_norm(acc[:, c * SUB:(c + 1) * SUB], gain)
            cols = slice(c0 + c * SUB, c0 + (c + 1) * SUB)
            if attn_layout:
                q_ref[:, cols] = y.astype(BF16)
            else:
                qkv_ref[:, cols] = y

    def kv_epilogue(acc, c0):
        k = head_norm(acc[:, 0:SUB], kg_ref[...])
        v = acc[:, SUB:2 * SUB]
        if attn_layout:
            _store_placed_keys(k, kp_ref)
            _store_values_t(v, vt_ref)
        else:
            qkv_ref[:, QKV_K:QKV_K + SUB] = k
            qkv_ref[:, QKV_V:QKV_V + SUB] = v

    def gate_epilogue(acc, c0):
        gh_ref[:, c0:c0 + acc.shape[1]] = jax.nn.sigmoid(acc)

    def raw_epilogue(acc, c0):
        gh_ref[:, c0:c0 + acc.shape[1]] = acc

    half = QKV_K // 2
    chunks = [(wq_ref, 0, half, q_epilogue), (wq_ref, half, half, q_epilogue),
              (wq_ref, QKV_K, 2 * SUB, kv_epilogue)]
    for c0 in range(0, GH_COLS, D_MODEL):
        chunks.append((wg_ref, c0, D_MODEL, gate_epilogue if c0 < GH_X1 else raw_epilogue))

    def matmul(idx):
        w_ref, c0, width, _ = chunks[idx]
        return jnp.dot(h, w_ref[:, c0:c0 + width], preferred_element_type=F32)

    ahead = 2
    pending = {idx: matmul(idx) for idx in range(ahead)}
    for idx in range(len(chunks)):
        if idx + ahead < len(chunks):
            pending[idx + ahead] = matmul(idx + ahead)
        chunks[idx][3](pending.pop(idx), chunks[idx][1])


def _inproj(x, mod3, tiles_per_mod, g, wq, wg, qg, kg, bd, rope_tabs, tm, attn_layout):
    t = x.shape[0]
    rope = rope_tabs is not None
    mod_map = (lambda i: (i // tiles_per_mod, 0, 0)) if tiles_per_mod else (lambda i: (0, 0, 0))
    row = lambda i: (i, 0)
    const = lambda i: (0, 0)
    in_specs = [
        pl.BlockSpec((tm, D_MODEL), row),
        pl.BlockSpec((1, N_MOD, D_MODEL), mod_map),
        pl.BlockSpec((1, D_MODEL), const),
        _resident((D_MODEL, QKV_COLS), const),
        _resident((D_MODEL, GH_COLS), const),
        pl.BlockSpec((1, SUB), const),
        pl.BlockSpec((1, SUB), const),
        pl.BlockSpec((SUB, SUB), const),
    ]
    args = [x, mod3, g.reshape(1, D_MODEL), wq, wg, qg, kg, bd]
    if rope:
        seq_tiles = rope_tabs[0].shape[0] // tm
        for tab in rope_tabs:
            in_specs.append(pl.BlockSpec((tm, SUB), lambda i: (i % seq_tiles, 0)))
            args.append(tab)
    gh_spec = pl.BlockSpec((tm, GH_COLS), row)
    gh_shape = jax.ShapeDtypeStruct((t, GH_COLS), F32)
    if attn_layout:
        kwidth = 2 * N_KV_HEADS * LANES
        out_specs = [pl.BlockSpec((tm, QKV_K), row), pl.BlockSpec((tm, kwidth), row),
                     pl.BlockSpec((VT_ROWS, tm), lambda i: (0, i)), gh_spec]
        out_shape = [jax.ShapeDtypeStruct((t, QKV_K), BF16), jax.ShapeDtypeStruct((t, kwidth), BF16),
                     jax.ShapeDtypeStruct((VT_ROWS, t), BF16), gh_shape]
    else:
        out_specs = [pl.BlockSpec((tm, QKV_COLS), row), gh_spec]
        out_shape = [jax.ShapeDtypeStruct((t, QKV_COLS), F32), gh_shape]
    return pl.pallas_call(
        functools.partial(_inproj_kernel, rope=rope, attn_layout=attn_layout),
        grid=(t // tm,),
        in_specs=in_specs,
        out_specs=out_specs,
        out_shape=out_shape,
        compiler_params=_cparams(("parallel",)),
        name="inproj",
    )(*args)


def _col_reduce(x, op):
    rows, cols = x.shape
    part = op(x.reshape(8, rows // 8, cols), axis=0)
    return op(part, axis=0, keepdims=True)


def _attn_kernel(*refs, window, tq, qb_per_step, n_steps):
    refs = list(refs)
    sink_ref, q_ref, kc_ref, vc_ref = refs[:4]
    refs = refs[4:]
    if window:
        n_kb = qb_per_step + 2
        kb_refs, vb_refs, bias_ref = refs[:n_kb], refs[n_kb:2 * n_kb], refs[2 * n_kb]
        refs = refs[2 * n_kb + 1:]
    o_ref, kcp_scr, vct_scr = refs
    i = pl.program_id(1)

    @pl.when(i == 0)
    def _():
        _store_placed_keys(kc_ref[...], kcp_scr)
        _store_values_t(vc_ref[...], vct_scr)

    if window:
        krow = lax.broadcasted_iota(jnp.int32, (3 * WINDOW, 2 * tq), 0)
        band = bias_ref[...]
        no_prev = jnp.where((krow < WINDOW) & (i == 0), NEG_INF, 0.0)
        no_next = jnp.where((krow >= 2 * WINDOW) & (i == n_steps - 1), NEG_INF, 0.0)
        biases = [band] * qb_per_step
        biases[0] = biases[0] + no_prev
        biases[-1] = biases[-1] + no_next
    lane = lax.broadcasted_iota(jnp.int32, (1, 2 * tq), 1)
    dn = (((1,), (1,)), ((), ()))
    nkc = kcp_scr.shape[0]
    heads_units = 2 * N_KV_HEADS
    n_units = qb_per_step * heads_units

    def scores(unit):
        qb, u = divmod(unit, heads_units)
        g, a = divmod(u, 2)
        rows = slice(qb * tq, (qb + 1) * tq)
        q0 = q_ref[rows, g * 2 * LANES:g * 2 * LANES + LANES]
        q1 = q_ref[rows, g * 2 * LANES + LANES:(g + 1) * 2 * LANES]
        qg = jnp.concatenate([q0, q1], axis=0).astype(BF16)
        kcols = slice(u * LANES, (u + 1) * LANES)
        keys = [kcp_scr[:, kcols]]
        if window:
            keys += [kb_refs[qb + t][:, kcols] for t in range(3)]
        return lax.dot_general(jnp.concatenate(keys, axis=0), qg, dn, preferred_element_type=F32)

    def softmax(unit, s):
        qb, u = divmod(unit, heads_units)
        g, a = divmod(u, 2)
        sink = jnp.where(lane < tq, sink_ref[4 * g + a], sink_ref[4 * g + 2 + a]) * LOG2E
        s_c = s[0:nkc]
        m = jnp.maximum(_col_reduce(s_c, jnp.max), sink)
        if window:
            s_w = s[nkc:] + biases[qb]
            m = jnp.maximum(m, _col_reduce(s_w, jnp.max))
        probs = [jnp.exp2(s_c - m).astype(BF16)]
        if window:
            probs.append(jnp.exp2(s_w - m).astype(BF16))
        return probs, jnp.exp2(sink - m)

    def values(unit, probs, sink_term):
        qb, u = divmod(unit, heads_units)
        g = u // 2
        vrows = slice(g * VROWS, (g + 1) * VROWS)
        o = jnp.dot(vct_scr[vrows, :], probs[0], preferred_element_type=F32)
        if window:
            vw_t = jnp.concatenate([vb_refs[qb + t][vrows, :] for t in range(3)], axis=1)
            o = o + jnp.dot(vw_t, probs[1], preferred_element_type=F32)
        l = o[HEAD_DIM:HEAD_DIM + 1, :] + sink_term
        return o[0:HEAD_DIM, :] * (1.0 / l)

    lead = 3
    outs = []
    s_q = {unit: scores(unit) for unit in range(lead)}
    p_q = {}
    for step in range(n_units + 1):
        if step + lead < n_units:
            s_q[step + lead] = scores(step + lead)
        if 0 <= step < n_units:
            p_q[step] = softmax(step, s_q.pop(step))
        unit = step - 1
        if 0 <= unit < n_units:
            outs.append(values(unit, *p_q.pop(unit)))
            if unit % 2 == 1:
                qb, u = divmod(unit, heads_units)
                g = u // 2
                rows = slice(qb * tq, (qb + 1) * tq)
                o_t = jnp.concatenate(outs[-2:], axis=0)
                o_ref[rows, g * 2 * LANES:g * 2 * LANES + LANES] = o_t[:, 0:tq].T.astype(BF16)
                o_ref[rows, g * 2 * LANES + LANES:(g + 1) * 2 * LANES] = o_t[:, tq:2 * tq].T.astype(BF16)


def _window_bias_t(tq):
    a = np.arange(WINDOW)[None, :]
    b = np.arange(WINDOW)[:, None]
    prev = np.where(b >= a, 0.0, NEG_INF)
    cur = np.zeros((WINDOW, WINDOW))
    nxt = np.where(b <= a, 0.0, NEG_INF)
    one = np.concatenate([prev, cur, nxt], axis=0).astype(np.float32)
    return jnp.asarray(np.concatenate([one, one], axis=1))


def _attention_ctx(qkv, sink, n_batch, seq):
    t = qkv.shape[0]
    tq = seq
    kblk = QKV_K // SUB
    vblk = QKV_V // SUB
    return pl.pallas_call(
        functools.partial(_attn_kernel, window=False, tq=tq, qb_per_step=1, n_steps=1),
        grid=(n_batch, 1),
        in_specs=[
            pl.BlockSpec(memory_space=pltpu.SMEM),
            pl.BlockSpec((tq, D_MODEL), lambda b, i: (b, 0)),
            pl.BlockSpec((seq, SUB), lambda b, i: (b, kblk)),
            pl.BlockSpec((seq, SUB), lambda b, i: (b, vblk)),
        ],
        out_specs=pl.BlockSpec((tq, D_MODEL), lambda b, i: (b, 0)),
        out_shape=jax.ShapeDtypeStruct((t, D_MODEL), BF16),
        scratch_shapes=[pltpu.VMEM((seq, 2 * N_KV_HEADS * LANES), BF16), pltpu.VMEM((VT_ROWS, seq), BF16)],
        compiler_params=_cparams(("parallel", "arbitrary")),
        name="attn_ctx",
    )(sink, qkv, qkv, qkv)


def _attention_lat(q, kp, vt, ck, cv, sink, n_batch, seq, past, qb_per_step=4):
    t = q.shape[0]
    tq = WINDOW
    nqb = seq // tq
    n_steps = nqb // qb_per_step
    kwidth = 2 * N_KV_HEADS * LANES

    def block_index(b, i, k):
        return b * nqb + jnp.clip(qb_per_step * i - 1 + k, 0, nqb - 1)

    kblocks = [pl.BlockSpec((tq, kwidth), functools.partial(lambda b, i, k: (block_index(b, i, k), 0), k=k))
               for k in range(qb_per_step + 2)]
    vblocks = [pl.BlockSpec((VT_ROWS, tq), functools.partial(lambda b, i, k: (0, block_index(b, i, k)), k=k))
               for k in range(qb_per_step + 2)]
    return pl.pallas_call(
        functools.partial(_attn_kernel, window=True, tq=tq, qb_per_step=qb_per_step, n_steps=n_steps),
        grid=(n_batch, n_steps),
        in_specs=[
            pl.BlockSpec(memory_space=pltpu.SMEM),
            pl.BlockSpec((qb_per_step * tq, D_MODEL), lambda b, i: (b * n_steps + i, 0)),
            pl.BlockSpec((past, SUB), lambda b, i: (b, 0)),
            pl.BlockSpec((past, SUB), lambda b, i: (b, 0)),
            *kblocks, *vblocks,
            pl.BlockSpec((3 * WINDOW, 2 * tq), lambda b, i: (0, 0)),
        ],
        out_specs=pl.BlockSpec((qb_per_step * tq, D_MODEL), lambda b, i: (b * n_steps + i, 0)),
        out_shape=jax.ShapeDtypeStruct((t, D_MODEL), BF16),
        scratch_shapes=[pltpu.VMEM((past, kwidth), BF16), pltpu.VMEM((VT_ROWS, past), BF16)],
        compiler_params=_cparams(("parallel", "arbitrary")),
        name="attn_lat",
    )(sink, q, ck, cv, *([kp] * (qb_per_step + 2)), *([vt] * (qb_per_step + 2)), _window_bias_t(tq))


def _fgen_kernel(z_ref, w1_ref, b1_ref, w2_ref, b2_ref, fr_ref,
                 w3b_ref, b3b_ref, dcb_ref, w3f_ref, b3f_ref, dcf_ref, o_ref, a2_scr, *, n):
    @pl.when((pl.program_id(0) == 0) & (pl.program_id(1) == 0))
    def _():
        fr = fr_ref[...]
        a1 = jnp.sin(fr * (jnp.dot(z_ref[...], w1_ref[...], precision=HIGHEST,
                                   preferred_element_type=F32) + b1_ref[...]))
        a2_scr[...] = jnp.sin(fr * (jnp.dot(a1, w2_ref[...], precision=HIGHEST,
                                            preferred_element_type=F32) + b2_ref[...]))

    tc = o_ref.shape[1]
    tf = jnp.broadcast_to(z_ref[:, 0:1], (n, tc))
    tb = jnp.broadcast_to(z_ref[:, FILTER_WIDTH:FILTER_WIDTH + 1], (n, tc))
    a2 = a2_scr[...]
    hb = (jnp.dot(a2, w3b_ref[...], precision=HIGHEST, preferred_element_type=F32)
          + b3b_ref[...]) * jnp.exp(-tb * jnp.abs(dcb_ref[...]))
    hf = (jnp.dot(a2, w3f_ref[...], precision=HIGHEST, preferred_element_type=F32)
          + b3f_ref[...]) * jnp.exp(-tf * jnp.abs(dcf_ref[...]))
    tot = (jnp.sum(jnp.abs(hb), axis=0, keepdims=True)
           + jnp.sum(jnp.abs(hf), axis=0, keepdims=True))
    inv = 1.0 / (tot + EPS)
    rowid = lax.broadcasted_iota(jnp.int32, (n, tc), 0)
    o_ref[0:n, :] = jnp.where(rowid == 0, 0.0, hb * inv)
    o_ref[n:2 * n, :] = hf * inv


def _filter_gen(n, filt_w1, filt_b1, filt_w2, filt_b2, filt_w3, filt_b3, filt_freq, filt_decay, tc):
    t = np.arange(n, dtype=np.float64) / max(n - 1, 1)
    bands = np.arange(1, N_BANDS + 1, dtype=np.float64)
    ang = 2.0 * math.pi * t[:, None] * bands[None, :]
    z = np.concatenate([t[:, None], np.cos(ang), np.sin(ang)], axis=-1)
    zb = np.concatenate([z[0:1], z[1:][::-1]], axis=0)
    fw = FILTER_WIDTH
    zpad = np.zeros((n, fw - FILTER_EMB))
    zpacked = jnp.asarray(np.concatenate([z, zpad, zb, zpad], axis=1), dtype=F32)
    w1p = jnp.pad(filt_w1.astype(F32), ((0, fw - FILTER_EMB), (0, 0)))
    zero = jnp.zeros((fw, fw), F32)
    w1 = jnp.block([[w1p, zero], [zero, w1p]])
    w2f = filt_w2.astype(F32)
    w2 = jnp.block([[w2f, zero], [zero, w2f]])
    twice = lambda v: jnp.tile(v.astype(F32), 2).reshape(1, LANES)
    b1, b2, fr = twice(filt_b1), twice(filt_b2), twice(filt_freq)
    w3 = filt_w3.astype(F32)
    ncol = w3.shape[1]
    w3_fwd = jnp.concatenate([w3, jnp.zeros_like(w3)], axis=0)
    w3_bwd = jnp.concatenate([jnp.zeros_like(w3), w3], axis=0)
    b3 = filt_b3.astype(F32).reshape(1, ncol)
    dc = filt_decay.astype(F32).reshape(1, ncol)
    ct = D_HYENA // tc
    full = lambda shape: pl.BlockSpec(shape, lambda o, c: (0, 0))
    bwd = lambda rows: pl.BlockSpec((rows, tc), lambda o, c: (0, (2 * o + 1) * ct + c))
    fwd = lambda rows: pl.BlockSpec((rows, tc), lambda o, c: (0, (2 * o) * ct + c))
    return pl.pallas_call(
        functools.partial(_fgen_kernel, n=n),
        grid=(2, ct),
        in_specs=[
            full((n, LANES)), full((LANES, LANES)), full((1, LANES)),
            full((LANES, LANES)), full((1, LANES)), full((1, LANES)),
            bwd(LANES), bwd(1), bwd(1), fwd(LANES), fwd(1), fwd(1),
        ],
        out_specs=pl.BlockSpec((2 * n, tc), lambda o, c: (0, o * ct + c)),
        out_shape=jax.ShapeDtypeStruct((2 * n, 2 * D_HYENA), F32),
        scratch_shapes=[pltpu.VMEM((n, LANES), F32)],
        compiler_params=_cparams(("arbitrary", "arbitrary")),
        name="hyena_filter_gen",
    )(zpacked, w1, b1, w2, b2, fr, w3_bwd, b3, dc, w3_fwd, b3, dc)


def _dft_mats(blk):
    f = np.arange(blk, dtype=np.int64)[:, None]
    s = np.arange(blk, dtype=np.int64)[None, :]
    theta = (np.pi / (2 * blk)) * (((2 * f + 1) * s) % (4 * blk)).astype(np.float64)
    fwd = np.concatenate([np.cos(theta), -np.sin(theta)], axis=0)
    inv = np.concatenate([np.cos(theta).T, -np.sin(theta).T], axis=1) / blk
    return (jnp.asarray(fwd, dtype=F32).astype(BF16), jnp.asarray(inv, dtype=F32).astype(BF16))


def _ftf_kernel(k_ref, skip_ref, f_ref, o_ref, *, nblk2, blk):
    tc = o_ref.shape[2]
    fmat = f_ref[...]
    fidx = lax.broadcasted_iota(jnp.int32, (blk, tc), 0)
    sgn = jnp.where(fidx % 2 == 0, 1.0, -1.0)
    prev = None
    for e in range(nblk2):
        p = jnp.dot(fmat, k_ref[e * blk:(e + 1) * blk, :].astype(BF16), preferred_element_type=F32)
        if e >= 1:
            re = p[0:blk] - sgn * prev[blk:2 * blk]
            im = p[blk:2 * blk] + sgn * prev[0:blk]
            if e == nblk2 // 2:
                re = re + skip_ref[...]
            o_ref[e - 1, 0:blk, :] = re
            o_ref[e - 1, blk:2 * blk, :] = im
        prev = p


def _filter_transform(kfull, skip, fmat, n, blk, tc):
    nblk2 = 2 * n // blk
    nd = nblk2 - 1
    ct = D_HYENA // tc
    return pl.pallas_call(
        functools.partial(_ftf_kernel, nblk2=nblk2, blk=blk),
        grid=(2, ct),
        in_specs=[
            pl.BlockSpec((2 * n, tc), lambda o, c: (0, o * ct + c)),
            pl.BlockSpec((None, 1, tc), lambda o, c: (o, 0, c)),
            pl.BlockSpec((2 * blk, blk), lambda o, c: (0, 0)),
        ],
        out_specs=pl.BlockSpec((nd, 2 * blk, tc), lambda o, c: (o, 0, c)),
        out_shape=jax.ShapeDtypeStruct((2 * nd, 2 * blk, D_HYENA), F32),
        compiler_params=_cparams(("parallel", "parallel")),
        name="hyena_filter_transform",
    )(kfull, skip, fmat)


CHUNK = 32
HALO = 8
SC_ROWS = 128


def _short_conv_block(src_ref, b, j, n, blk, w_ref, b_ref):
    pieces = []
    for r0 in range(j * blk, (j + 1) * blk, SC_ROWS):
        lo = max(r0 - HALO, 0)
        hi = min(r0 + SC_ROWS + HALO, n)
        off = r0 - lo
        win = src_ref[b, lo:hi, :]
        prev = pltpu.roll(win, 1, axis=0)[off:off + SC_ROWS]
        nxt = pltpu.roll(win, hi - lo - 1, axis=0)[off:off + SC_ROWS]
        u = win[off:off + SC_ROWS]
        rowid = lax.broadcasted_iota(jnp.int32, u.shape, 0)
        if r0 == 0:
            prev = jnp.where(rowid == 0, 0.0, prev)
        if r0 + SC_ROWS == n:
            nxt = jnp.where(rowid == SC_ROWS - 1, 0.0, nxt)
        pieces.append(prev * w_ref[0:1, :] + u * w_ref[1:2, :] + nxt * w_ref[2:3, :] + b_ref[...])
    return jnp.concatenate(pieces, axis=0)


def _conv_kernel(zin_ref, gin_ref, wz_ref, bz_ref, wg_ref, bg_ref, g_ref, f_ref, fi_ref,
                 o_ref, zf_scr, yf_scr, *, n, blk, bb, conv_z):
    nblk = n // blk
    tc = o_ref.shape[2]
    fmat = f_ref[...]
    for j in range(nblk):
        rows = slice(j * blk, (j + 1) * blk)
        zs = []
        for b in range(bb):
            if conv_z:
                z = _short_conv_block(zin_ref, b, j, n, blk, wz_ref, bz_ref)
            else:
                z = zin_ref[b, rows, :]
            zs.append(z.astype(BF16))
        zf_scr[j] = jnp.dot(fmat, jnp.concatenate(zs, axis=1), preferred_element_type=F32)
        for b in range(bb):
            o_ref[b, rows, :] = _short_conv_block(gin_ref, b, j, n, blk, wg_ref, bg_ref)

    def pairs(i, slot):
        for r in range(blk // CHUNK):
            re_rows = slice(r * CHUNK, (r + 1) * CHUNK)
            im_rows = slice(blk + r * CHUNK, blk + (r + 1) * CHUNK)
            acc_re = [jnp.zeros((CHUNK, tc), F32) for _ in range(bb)]
            acc_im = [jnp.zeros((CHUNK, tc), F32) for _ in range(bb)]
            for j in range(nblk):
                d = i - j + (nblk - 1)
                g_re = g_ref[d, re_rows, :]
                g_im = g_ref[d, im_rows, :]
                for b in range(bb):
                    cols = slice(b * tc, (b + 1) * tc)
                    z_re = zf_scr[j, re_rows, cols]
                    z_im = zf_scr[j, im_rows, cols]
                    acc_re[b] = acc_re[b] + (g_re * z_re - g_im * z_im)
                    acc_im[b] = acc_im[b] + (g_re * z_im + g_im * z_re)
            for b in range(bb):
                cols = slice(b * tc, (b + 1) * tc)
                yf_scr[slot, re_rows, cols] = acc_re[b].astype(BF16)
                yf_scr[slot, im_rows, cols] = acc_im[b].astype(BF16)

    def inverse(slot):
        return jnp.dot(fi_ref[...], yf_scr[slot], preferred_element_type=F32)

    def gate_out(i, y):
        rows = pl.ds(pl.multiple_of(i * blk, blk), blk)
        for b in range(bb):
            o_ref[b, rows, :] = o_ref[b, rows, :] * y[:, b * tc:(b + 1) * tc]

    pairs(0, 0)
    if nblk > 1:
        def body(i, carry):
            y = inverse((i - 1) & 1)
            pairs(i, i & 1)
            gate_out(i - 1, y)
            return carry
        lax.fori_loop(1, nblk, body, 0)
    gate_out(nblk - 1, inverse((nblk - 1) & 1))


def _long_conv(zsrc, zcol, gsrc, gcol, conv_w, conv_b, gspec, order, fmat, fimat,
               n_batch, n, blk, tc, bb, conv_z):
    nblk = n // blk
    nd = 2 * nblk - 1
    ct = D_HYENA // tc
    hy0 = GH_X1 // tc
    wz = (zcol - hy0) if conv_z else 0
    wg = gcol - hy0
    return pl.pallas_call(
        functools.partial(_conv_kernel, n=n, blk=blk, bb=bb, conv_z=conv_z),
        grid=(ct, n_batch // bb),
        in_specs=[
            pl.BlockSpec((bb, n, tc), lambda c, b: (b, 0, zcol + c)),
            pl.BlockSpec((bb, n, tc), lambda c, b: (b, 0, gcol + c)),
            pl.BlockSpec((3, tc), lambda c, b: (0, wz + c)),
            pl.BlockSpec((1, tc), lambda c, b: (0, wz + c)),
            pl.BlockSpec((3, tc), lambda c, b: (0, wg + c)),
            pl.BlockSpec((1, tc), lambda c, b: (0, wg + c)),
            _resident((nd, 2 * blk, tc), lambda c, b: (order, 0, c)),
            _resident((2 * blk, blk), lambda c, b: (0, 0)),
            _resident((blk, 2 * blk), lambda c, b: (0, 0)),
        ],
        out_specs=pl.BlockSpec((bb, n, tc), lambda c, b: (b, 0, c)),
        out_shape=jax.ShapeDtypeStruct((n_batch, n, D_HYENA), F32),
        scratch_shapes=[
            pltpu.VMEM((nblk, 2 * blk, bb * tc), F32),
            pltpu.VMEM((2, 2 * blk, bb * tc), BF16),
        ],
        compiler_params=_cparams(("parallel", "arbitrary")),
        name="hyena_conv",
    )(zsrc, gsrc, conv_w, conv_b, conv_w, conv_b, gspec, fmat, fimat)


def _hyena(gh, n_batch, n, blk, tc, bb, conv_w, conv_b, skip, filt):
    fmat, fimat = _dft_mats(blk)
    kfull = _filter_gen(n, *filt, tc=128)
    sk = skip.astype(F32).reshape(2, 1, D_HYENA)
    gspec = _filter_transform(kfull, sk, fmat, n, blk, tc=256)
    cw = conv_w.astype(F32)
    cb = conv_b.astype(F32).reshape(1, -1)
    gh3 = gh.reshape(n_batch, n, GH_COLS)
    z2 = _long_conv(gh3, GH_HV // tc, gh3, GH_X1 // tc, cw, cb, gspec, 0, fmat, fimat,
                    n_batch, n, blk, tc, bb, conv_z=True)
    y = _long_conv(z2, 0, gh3, GH_X2 // tc, cw, cb, gspec, 1, fmat, fimat,
                   n_batch, n, blk, tc, bb, conv_z=False)
    return y.reshape(n_batch * n, D_HYENA)


def _outproj_kernel(x_ref, attn_ref, hy_ref, ga_ref, gh_ref, mod_ref, wa_ref, wh_ref, wo_ref, o_ref):
    n_chunks = 2
    rc = x_ref.shape[0] // n_chunks

    def branches(c):
        rows = slice(c * rc, (c + 1) * rc)
        a = jnp.dot(attn_ref[rows, :].astype(BF16), wa_ref[...], preferred_element_type=F32)
        h = jnp.dot(hy_ref[rows, :].astype(BF16), wh_ref[...], preferred_element_type=F32)
        return a, h

    pending = {0: branches(0)}
    for c in range(n_chunks):
        if c + 1 < n_chunks:
            pending[c + 1] = branches(c + 1)
        a, h = pending.pop(c)
        rows = slice(c * rc, (c + 1) * rc)
        merged = ga_ref[rows, :] * a + gh_ref[rows, :] * h
        out = jnp.dot(merged.astype(BF16), wo_ref[...], preferred_element_type=F32)
        o_ref[rows, :] = x_ref[rows, :] + mod_ref[0, 5:6, :] * out


def _outproj(x, attn, hy, gh, mod3, tiles_per_mod, wa, wh, wo, tm):
    t = x.shape[0]
    mod_map = (lambda i: (i // tiles_per_mod, 0, 0)) if tiles_per_mod else (lambda i: (0, 0, 0))
    row = lambda i: (i, 0)
    wspec = _resident((D_MODEL, D_MODEL), lambda i: (0, 0))
    return pl.pallas_call(
        _outproj_kernel,
        grid=(t // tm,),
        in_specs=[
            pl.BlockSpec((tm, D_MODEL), row),
            pl.BlockSpec((tm, D_MODEL), row),
            pl.BlockSpec((tm, D_MODEL), row),
            pl.BlockSpec((tm, D_MODEL), lambda i: (i, GH_GA // D_MODEL)),
            pl.BlockSpec((tm, D_MODEL), lambda i: (i, GH_GH // D_MODEL)),
            pl.BlockSpec((1, N_MOD, D_MODEL), mod_map),
            wspec, wspec, wspec,
        ],
        out_specs=pl.BlockSpec((tm, D_MODEL), row),
        out_shape=jax.ShapeDtypeStruct((t, D_MODEL), F32),
        compiler_params=_cparams(("parallel",)),
        name="mixer_out",
    )(x, attn, hy, gh, gh, mod3, wa, wh, wo)


def _rope_tables(n):
    pos = np.arange(n)
    row = (pos // GRID_W).astype(np.float64)
    col = (pos % GRID_W).astype(np.float64)
    n_freq = HEAD_DIM // 4
    inv = ROPE_BASE ** (-np.arange(n_freq, dtype=np.float64) / n_freq)
    ar = row[:, None] * inv[None, :]
    ac = col[:, None] * inv[None, :]
    cos = np.concatenate([np.cos(ar), np.cos(ar), np.cos(ac), np.cos(ac)], axis=-1)
    sin = np.concatenate([-np.sin(ar), np.sin(ar), -np.sin(ac), np.sin(ac)], axis=-1)
    reps = SUB // HEAD_DIM
    return (jnp.asarray(np.tile(cos, (1, reps)), dtype=F32), jnp.asarray(np.tile(sin, (1, reps)), dtype=F32))


def _layer(x, mod3, per_batch_mod, n_batch, seq, w, ctx_kv, hy_blk, hy_tc, hy_bb):
    tmf = 1024
    ffn_tiles_per_mod = (seq // tmf) if per_batch_mod else 0
    x = _ffn(x, mod3, ffn_tiles_per_mod, w["norm_ffn1"], w["ffn1_wi"], w["ffn1_wo"], 0, tmf, 256)
    qkv = None
    tmi = 512
    inproj_tiles_per_mod = (seq // tmi) if per_batch_mod else 0
    if ctx_kv is None:
        qkv, gh = _inproj(x, mod3, inproj_tiles_per_mod, w["norm_mix"], w["w_qkv"], w["w_gh"], w["qg"],
                          w["kg"], w["bd"], None, tmi, attn_layout=False)
        attn = _attention_ctx(qkv, w["sink"], n_batch, seq)
    else:
        q, kp, vt, gh = _inproj(x, mod3, inproj_tiles_per_mod, w["norm_mix"], w["w_qkv"], w["w_gh"],
                                w["qg"], w["kg"], w["bd"], _rope_tables(seq), tmi, attn_layout=True)
        attn = _attention_lat(q, kp, vt, ctx_kv[0], ctx_kv[1], w["sink"], n_batch, seq,
                              ctx_kv[0].shape[0] // n_batch)
    hy = _hyena(gh, n_batch, seq, hy_blk, hy_tc, hy_bb, w["conv_w"], w["conv_b"], w["hyena_skip"], w["filt"])
    tmo = 512
    x = _outproj(x, attn, hy, gh, mod3, (seq // tmo) if per_batch_mod else 0,
                 w["wa"], w["wh"], w["wo"], tmo)
    x = _ffn(x, mod3, ffn_tiles_per_mod, w["norm_ffn2"], w["ffn2_wi"], w["ffn2_wo"], 6, tmf, 256)
    return x, qkv


def kernel(x_prompt, x_sample, cache_k, cache_v, c, c_ctx, w_mod, b_mod, norm_ffn1, ffn1_wi, ffn1_wo, norm_mix, w_in, q_norm, k_norm, attn_sink, conv_w, conv_b, filt_w1, filt_b1, filt_w2, filt_b2, filt_w3, filt_b3, filt_freq, filt_decay, hyena_skip, w_attn_branch, w_hyena_branch, w_out, norm_ffn2, ffn2_wi, ffn2_wo):
    batch, seq, _ = x_prompt.shape
    dec_batch, dec_seq, _ = x_sample.shape
    depth = w_mod.shape[0]
    past = cache_k.shape[2]

    yp = x_prompt.reshape(batch * seq, D_MODEL)
    ys = x_sample.reshape(dec_batch * dec_seq, D_MODEL)
    bd = jnp.asarray(np.kron(np.eye(SUB // HEAD_DIM), np.ones((HEAD_DIM, HEAD_DIM))), dtype=F32).astype(BF16)
    new_ks, new_vs = [], []
    for l in range(depth):
        c_rows = jnp.zeros((16, D_MODEL), F32).at[0:dec_batch].set(c).at[dec_batch].set(c_ctx)
        mod = _modulation(c_rows, w_mod[l], b_mod[l]).reshape(16, N_MOD, D_MODEL)
        mod_lat = mod[0:dec_batch]
        mod_ctx = mod[dec_batch:dec_batch + 1]
        wl = w_in[l]
        hy_end = QKV_COLS + 3 * D_HYENA
        w = {
            "norm_ffn1": norm_ffn1[l], "ffn1_wi": ffn1_wi[l].astype(BF16), "ffn1_wo": ffn1_wo[l].astype(BF16),
            "norm_mix": norm_mix[l],
            "w_qkv": wl[:, 0:QKV_COLS].astype(BF16),
            "w_gh": jnp.concatenate([wl[:, hy_end:], wl[:, QKV_COLS:hy_end]], axis=1).astype(BF16),
            "qg": jnp.tile(q_norm[l], SUB // HEAD_DIM).reshape(1, SUB),
            "kg": jnp.tile(k_norm[l], SUB // HEAD_DIM).reshape(1, SUB),
            "bd": bd, "sink": attn_sink[l],
            "conv_w": conv_w[l], "conv_b": conv_b[l], "hyena_skip": hyena_skip[l],
            "filt": (filt_w1[l], filt_b1[l], filt_w2[l], filt_b2[l], filt_w3[l], filt_b3[l],
                     filt_freq[l], filt_decay[l]),
            "wa": w_attn_branch[l].astype(BF16), "wh": w_hyena_branch[l].astype(BF16),
            "wo": w_out[l].astype(BF16),
            "norm_ffn2": norm_ffn2[l], "ffn2_wi": ffn2_wi[l].astype(BF16), "ffn2_wo": ffn2_wo[l].astype(BF16),
        }
        yp, qkv_p = _layer(yp, mod_ctx, False, batch, seq, w, None, hy_blk=256, hy_tc=128, hy_bb=8)
        ck = cache_k[:, l].reshape(dec_batch * past, N_KV_HEADS * HEAD_DIM)
        cv = cache_v[:, l].reshape(dec_batch * past, N_KV_HEADS * HEAD_DIM)
        ys, _ = _layer(ys, mod_lat, True, dec_batch, dec_seq, w, (ck, cv), hy_blk=512, hy_tc=128, hy_bb=2)
        new_ks.append(qkv_p[:, QKV_K:QKV_K + SUB].reshape(batch, seq, N_KV_HEADS, HEAD_DIM))
        new_vs.append(qkv_p[:, QKV_V:QKV_V + SUB].reshape(batch, seq, N_KV_HEADS, HEAD_DIM))
    new_k = jnp.stack(new_ks, axis=1)
    new_v = jnp.stack(new_vs, axis=1)
    return (yp.reshape(batch, seq, D_MODEL), ys.reshape(dec_batch, dec_seq, D_MODEL), new_k, new_v)
```

```python
import functools
import math

import numpy as np
import jax
import jax.numpy as jnp
from jax import lax
from jax.experimental import pallas as pl
from jax.experimental.pallas import tpu as pltpu

F32 = jnp.float32
BF16 = jnp.bfloat16
HIGHEST = lax.Precision.HIGHEST

D_MODEL = 1024
N_HEADS = 16
N_KV_HEADS = 4
HEAD_DIM = 64
GRID_W = 64
WINDOW = 128
ROPE_BASE = 10000.0
D_HYENA = 1024
N_BANDS = 16
FILTER_EMB = 1 + 2 * N_BANDS
FILTER_WIDTH = 64
D_FF = 2816
N_MOD = 9
EPS = 1e-6
NEG_INF = -1e30
LOG2E = math.log2(math.e)

QKV_COLS = 1536
QKV_K = 1024
QKV_V = 1280
GH_COLS = 5120
GH_GA = 0
GH_GH = 1024
GH_X1 = 2048
GH_X2 = 3072
GH_HV = 4096
SUB = 256

LANES = 128
VMEM_LIMIT = 61 * 1024 * 1024


def _cparams(sem):
    return pltpu.CompilerParams(dimension_semantics=sem, vmem_limit_bytes=VMEM_LIMIT)


def _resident(shape, index_map):
    return pl.BlockSpec(shape, index_map, pipeline_mode=pl.Buffered(1))


def _mod_kernel(c_ref, w_ref, b_ref, o_ref):
    c = c_ref[...]
    s = c * jax.nn.sigmoid(c)
    o_ref[...] = jnp.dot(s, w_ref[...], precision=HIGHEST, preferred_element_type=F32) + b_ref[...]


def _modulation(c_rows, w_mod, b_mod):
    rows = c_rows.shape[0]
    n_out = w_mod.shape[1]
    tn = 1024
    return pl.pallas_call(
        _mod_kernel,
        grid=(n_out // tn,),
        in_specs=[
            pl.BlockSpec((rows, D_MODEL), lambda j: (0, 0)),
            pl.BlockSpec((D_MODEL, tn), lambda j: (0, j)),
            pl.BlockSpec((1, tn), lambda j: (0, j)),
        ],
        out_specs=pl.BlockSpec((rows, tn), lambda j: (0, j)),
        out_shape=jax.ShapeDtypeStruct((rows, n_out), F32),
        compiler_params=_cparams(("arbitrary",)),
        name="modulation",
    )(c_rows, w_mod, b_mod.reshape(1, n_out))


def _norm_modulate(x, g, shift, scale):
    ms = jnp.mean(x * x, axis=-1, keepdims=True)
    y = x * lax.rsqrt(ms + EPS) * g
    return y * (1.0 + scale) + shift


def _ffn_kernel(x_ref, mod_ref, g_ref, wi_ref, wo_ref, o_ref, *, mod_base, tf):
    x = x_ref[...]
    shift = mod_ref[0, mod_base:mod_base + 1, :]
    scale = mod_ref[0, mod_base + 1:mod_base + 2, :]
    h = _norm_modulate(x, g_ref[...], shift, scale).astype(BF16)
    nj = D_FF // tf

    def gate_up(j):
        gate = jnp.dot(h, wi_ref[:, j * tf:(j + 1) * tf], preferred_element_type=F32)
        up = jnp.dot(h, wi_ref[:, D_FF + j * tf:D_FF + (j + 1) * tf], preferred_element_type=F32)
        return gate, up

    ahead = 2
    pending = {j: gate_up(j) for j in range(ahead)}
    acc = None
    for j in range(nj):
        if j + ahead < nj:
            pending[j + ahead] = gate_up(j + ahead)
        gate, up = pending.pop(j)
        a = (gate * jax.nn.sigmoid(gate) * up).astype(BF16)
        d = jnp.dot(a, wo_ref[j * tf:(j + 1) * tf, :], preferred_element_type=F32)
        acc = d if acc is None else acc + d
    gmod = mod_ref[0, mod_base + 2:mod_base + 3, :]
    o_ref[...] = x + (0.5 * gmod) * acc


def _ffn(x, mod3, tiles_per_mod, g, wi, wo, mod_base, tm, tf):
    t = x.shape[0]
    mod_map = (lambda i: (i // tiles_per_mod, 0, 0)) if tiles_per_mod else (lambda i: (0, 0, 0))
    return pl.pallas_call(
        functools.partial(_ffn_kernel, mod_base=mod_base, tf=tf),
        grid=(t // tm,),
        in_specs=[
            pl.BlockSpec((tm, D_MODEL), lambda i: (i, 0)),
            pl.BlockSpec((1, N_MOD, D_MODEL), mod_map),
            pl.BlockSpec((1, D_MODEL), lambda i: (0, 0)),
            _resident((D_MODEL, 2 * D_FF), lambda i: (0, 0)),
            _resident((D_FF, D_MODEL), lambda i: (0, 0)),
        ],
        out_specs=pl.BlockSpec((tm, D_MODEL), lambda i: (i, 0)),
        out_shape=jax.ShapeDtypeStruct((t, D_MODEL), F32),
        compiler_params=_cparams(("parallel",)),
        name="ffn",
    )(x, mod3, g.reshape(1, D_MODEL), wi, wo)


def _rope_partner(y):
    lane = lax.broadcasted_iota(jnp.int32, (1, LANES), 1)
    first = (lane % 32) < 16
    parts = []
    for c in range(y.shape[1] // LANES):
        yc = y[:, c * LANES:(c + 1) * LANES]
        fwd = pltpu.roll(yc, LANES - 16, axis=1)
        bwd = pltpu.roll(yc, 16, axis=1)
        parts.append(jnp.where(first, fwd, bwd))
    return jnp.concatenate(parts, axis=1)


def _place_halves(x, own_half):
    lane = lax.broadcasted_iota(jnp.int32, (1, LANES), 1)
    in_half = [lane < 64, lane >= 64]
    swapped = pltpu.roll(x, 64, axis=1)
    out = [None, None]
    out[own_half] = jnp.where(in_half[own_half], x, 0.0).astype(BF16)
    out[1 - own_half] = jnp.where(in_half[1 - own_half], swapped, 0.0).astype(BF16)
    return out


def _store_placed_keys(k, dst_ref):
    for g in range(N_KV_HEADS):
        halves = _place_halves(k[:, (g // 2) * LANES:(g // 2 + 1) * LANES], g % 2)
        for a in range(2):
            dst_ref[:, (2 * g + a) * LANES:(2 * g + a + 1) * LANES] = halves[a]


ONES_ROWS = 16
VROWS = HEAD_DIM + ONES_ROWS
VT_ROWS = N_KV_HEADS * VROWS


def _store_values_t(v, dst_ref):
    vt = v.T
    ones = jnp.ones((ONES_ROWS, v.shape[0]), BF16)
    for g in range(N_KV_HEADS):
        dst_ref[g * VROWS:g * VROWS + HEAD_DIM, :] = vt[g * HEAD_DIM:(g + 1) * HEAD_DIM].astype(BF16)
        dst_ref[g * VROWS + HEAD_DIM:(g + 1) * VROWS, :] = ones


def _inproj_kernel(*refs, rope, attn_layout):
    refs = list(refs)
    x_ref, mod_ref, g_ref, wq_ref, wg_ref, qg_ref, kg_ref, bd_ref = refs[:8]
    refs = refs[8:]
    if rope:
        cos_ref, sin_ref = refs[:2]
        refs = refs[2:]
    if attn_layout:
        q_ref, kp_ref, vt_ref, gh_ref = refs
    else:
        qkv_ref, gh_ref = refs

    h = _norm_modulate(x_ref[...], g_ref[...], mod_ref[0, 3:4, :], mod_ref[0, 4:5, :]).astype(BF16)

    def head_norm(t, gain):
        ss = jnp.dot((t * t).astype(BF16), bd_ref[...], preferred_element_type=F32)
        y = t * lax.rsqrt(ss * (1.0 / HEAD_DIM) + EPS) * gain
        if rope:
            y = y * cos_ref[...] + _rope_partner(y) * sin_ref[...]
        return y

    def q_epilogue(acc, c0):
        gain = qg_ref[...] * (LOG2E / math.sqrt(HEAD_DIM))
        for c in range(acc.shape[1] // SUB):
            y = head_norm(acc[:, c * SUB:(c + 1) * SUB], gain)
            cols = slice(c0 + c * SUB, c0 + (c + 1) * SUB)
            if attn_layout:
                q_ref[:, cols] = y.astype(BF16)
            else:
                qkv_ref[:, cols] = y

    def kv_epilogue(acc, c0):
        k = head_norm(acc[:, 0:SUB], kg_ref[...])
        v = acc[:, SUB:2 * SUB]
        if attn_layout:
            _store_placed_keys(k, kp_ref)
            _store_values_t(v, vt_ref)
        else:
            qkv_ref[:, QKV_K:QKV_K + SUB] = k
            qkv_ref[:, QKV_V:QKV_V + SUB] = v

    def gate_epilogue(acc, c0):
        gh_ref[:, c0:c0 + acc.shape[1]] = jax.nn.sigmoid(acc)

    def raw_epilogue(acc, c0):
        gh_ref[:, c0:c0 + acc.shape[1]] = acc

    half = QKV_K // 2
    chunks = [(wq_ref, 0, half, q_epilogue), (wq_ref, half, half, q_epilogue),
              (wq_ref, QKV_K, 2 * SUB, kv_epilogue)]
    for c0 in range(0, GH_COLS, D_MODEL):
        chunks.append((wg_ref, c0, D_MODEL, gate_epilogue if c0 < GH_X1 else raw_epilogue))

    def matmul(idx):
        w_ref, c0, width, _ = chunks[idx]
        return jnp.dot(h, w_ref[:, c0:c0 + width], preferred_element_type=F32)

    ahead = 2
    pending = {idx: matmul(idx) for idx in range(ahead)}
    for idx in range(len(chunks)):
        if idx + ahead < len(chunks):
            pending[idx + ahead] = matmul(idx + ahead)
        chunks[idx][3](pending.pop(idx), chunks[idx][1])


def _inproj(x, mod3, tiles_per_mod, g, wq, wg, qg, kg, bd, rope_tabs, tm, attn_layout):
    t = x.shape[0]
    rope = rope_tabs is not None
    mod_map = (lambda i: (i // tiles_per_mod, 0, 0)) if tiles_per_mod else (lambda i: (0, 0, 0))
    row = lambda i: (i, 0)
    const = lambda i: (0, 0)
    in_specs = [
        pl.BlockSpec((tm, D_MODEL), row),
        pl.BlockSpec((1, N_MOD, D_MODEL), mod_map),
        pl.BlockSpec((1, D_MODEL), const),
        _resident((D_MODEL, QKV_COLS), const),
        _resident((D_MODEL, GH_COLS), const),
        pl.BlockSpec((1, SUB), const),
        pl.BlockSpec((1, SUB), const),
        pl.BlockSpec((SUB, SUB), const),
    ]
    args = [x, mod3, g.reshape(1, D_MODEL), wq, wg, qg, kg, bd]
    if rope:
        seq_tiles = rope_tabs[0].shape[0] // tm
        for tab in rope_tabs:
            in_specs.append(pl.BlockSpec((tm, SUB), lambda i: (i % seq_tiles, 0)))
            args.append(tab)
    gh_spec = pl.BlockSpec((tm, GH_COLS), row)
    gh_shape = jax.ShapeDtypeStruct((t, GH_COLS), F32)
    if attn_layout:
        kwidth = 2 * N_KV_HEADS * LANES
        out_specs = [pl.BlockSpec((tm, QKV_K), row), pl.BlockSpec((tm, kwidth), row),
                     pl.BlockSpec((VT_ROWS, tm), lambda i: (0, i)), gh_spec]
        out_shape = [jax.ShapeDtypeStruct((t, QKV_K), BF16), jax.ShapeDtypeStruct((t, kwidth), BF16),
                     jax.ShapeDtypeStruct((VT_ROWS, t), BF16), gh_shape]
    else:
        out_specs = [pl.BlockSpec((tm, QKV_COLS), row), gh_spec]
        out_shape = [jax.ShapeDtypeStruct((t, QKV_COLS), F32), gh_shape]
    return pl.pallas_call(
        functools.partial(_inproj_kernel, rope=rope, attn_layout=attn_layout),
        grid=(t // tm,),
        in_specs=in_specs,
        out_specs=out_specs,
        out_shape=out_shape,
        compiler_params=_cparams(("parallel",)),
        name="inproj",
    )(*args)


def _col_reduce(x, op):
    rows, cols = x.shape
    part = op(x.reshape(8, rows // 8, cols), axis=0)
    return op(part, axis=0, keepdims=True)


def _attn_kernel(*refs, window, tq, qb_per_step, n_steps):
    refs = list(refs)
    sink_ref, q_ref, kc_ref, vc_ref = refs[:4]
    refs = refs[4:]
    if window:
        n_kb = qb_per_step + 2
        kb_refs, vb_refs, bias_ref = refs[:n_kb], refs[n_kb:2 * n_kb], refs[2 * n_kb]
        refs = refs[2 * n_kb + 1:]
    o_ref, kcp_scr, vct_scr = refs
    i = pl.program_id(1)

    @pl.when(i == 0)
    def _():
        _store_placed_keys(kc_ref[...], kcp_scr)
        _store_values_t(vc_ref[...], vct_scr)

    if window:
        krow = lax.broadcasted_iota(jnp.int32, (3 * WINDOW, 2 * tq), 0)
        band = bias_ref[...]
        no_prev = jnp.where((krow < WINDOW) & (i == 0), NEG_INF, 0.0)
        no_next = jnp.where((krow >= 2 * WINDOW) & (i == n_steps - 1), NEG_INF, 0.0)
        biases = [band] * qb_per_step
        biases[0] = biases[0] + no_prev
        biases[-1] = biases[-1] + no_next
    lane = lax.broadcasted_iota(jnp.int32, (1, 2 * tq), 1)
    dn = (((1,), (1,)), ((), ()))
    nkc = kcp_scr.shape[0]
    heads_units = 2 * N_KV_HEADS
    n_units = qb_per_step * heads_units

    def scores(unit):
        qb, u = divmod(unit, heads_units)
        g, a = divmod(u, 2)
        rows = slice(qb * tq, (qb + 1) * tq)
        q0 = q_ref[rows, g * 2 * LANES:g * 2 * LANES + LANES]
        q1 = q_ref[rows, g * 2 * LANES + LANES:(g + 1) * 2 * LANES]
        qg = jnp.concatenate([q0, q1], axis=0).astype(BF16)
        kcols = slice(u * LANES, (u + 1) * LANES)
        keys = [kcp_scr[:, kcols]]
        if window:
            keys += [kb_refs[qb + t][:, kcols] for t in range(3)]
        return lax.dot_general(jnp.concatenate(keys, axis=0), qg, dn, preferred_element_type=F32)

    def softmax(unit, s):
        qb, u = divmod(unit, heads_units)
        g, a = divmod(u, 2)
        sink = jnp.where(lane < tq, sink_ref[4 * g + a], sink_ref[4 * g + 2 + a]) * LOG2E
        s_c = s[0:nkc]
        m = jnp.maximum(_col_reduce(s_c, jnp.max), sink)
        if window:
            s_w = s[nkc:] + biases[qb]
            m = jnp.maximum(m, _col_reduce(s_w, jnp.max))
        probs = [jnp.exp2(s_c - m).astype(BF16)]
        if window:
            probs.append(jnp.exp2(s_w - m).astype(BF16))
        return probs, jnp.exp2(sink - m)

    def values(unit, probs, sink_term):
        qb, u = divmod(unit, heads_units)
        g = u // 2
        vrows = slice(g * VROWS, (g + 1) * VROWS)
        o = jnp.dot(vct_scr[vrows, :], probs[0], preferred_element_type=F32)
        if window:
            vw_t = jnp.concatenate([vb_refs[qb + t][vrows, :] for t in range(3)], axis=1)
            o = o + jnp.dot(vw_t, probs[1], preferred_element_type=F32)
        l = o[HEAD_DIM:HEAD_DIM + 1, :] + sink_term
        return o[0:HEAD_DIM, :] * (1.0 / l)

    lead = 3
    outs = []
    s_q = {unit: scores(unit) for unit in range(lead)}
    p_q = {}
    for step in range(n_units + 1):
        if step + lead < n_units:
            s_q[step + lead] = scores(step + lead)
        if 0 <= step < n_units:
            p_q[step] = softmax(step, s_q.pop(step))
        unit = step - 1
        if 0 <= unit < n_units:
            outs.append(values(unit, *p_q.pop(unit)))
            if unit % 2 == 1:
                qb, u = divmod(unit, heads_units)
                g = u // 2
                rows = slice(qb * tq, (qb + 1) * tq)
                o_t = jnp.concatenate(outs[-2:], axis=0)
                o_ref[rows, g * 2 * LANES:g * 2 * LANES + LANES] = o_t[:, 0:tq].T.astype(BF16)
                o_ref[rows, g * 2 * LANES + LANES:(g + 1) * 2 * LANES] = o_t[:, tq:2 * tq].T.astype(BF16)


def _window_bias_t(tq):
    a = np.arange(WINDOW)[None, :]
    b = np.arange(WINDOW)[:, None]
    prev = np.where(b >= a, 0.0, NEG_INF)
    cur = np.zeros((WINDOW, WINDOW))
    nxt = np.where(b <= a, 0.0, NEG_INF)
    one = np.concatenate([prev, cur, nxt], axis=0).astype(np.float32)
    return jnp.asarray(np.concatenate([one, one], axis=1))


def _attention_ctx(qkv, sink, n_batch, seq):
    t = qkv.shape[0]
    tq = seq
    kblk = QKV_K // SUB
    vblk = QKV_V // SUB
    return pl.pallas_call(
        functools.partial(_attn_kernel, window=False, tq=tq, qb_per_step=1, n_steps=1),
        grid=(n_batch, 1),
        in_specs=[
            pl.BlockSpec(memory_space=pltpu.SMEM),
            pl.BlockSpec((tq, D_MODEL), lambda b, i: (b, 0)),
            pl.BlockSpec((seq, SUB), lambda b, i: (b, kblk)),
            pl.BlockSpec((seq, SUB), lambda b, i: (b, vblk)),
        ],
        out_specs=pl.BlockSpec((tq, D_MODEL), lambda b, i: (b, 0)),
        out_shape=jax.ShapeDtypeStruct((t, D_MODEL), BF16),
        scratch_shapes=[pltpu.VMEM((seq, 2 * N_KV_HEADS * LANES), BF16), pltpu.VMEM((VT_ROWS, seq), BF16)],
        compiler_params=_cparams(("parallel", "arbitrary")),
        name="attn_ctx",
    )(sink, qkv, qkv, qkv)


def _attention_lat(q, kp, vt, ck, cv, sink, n_batch, seq, past, qb_per_step=4):
    t = q.shape[0]
    tq = WINDOW
    nqb = seq // tq
    n_steps = nqb // qb_per_step
    kwidth = 2 * N_KV_HEADS * LANES

    def block_index(b, i, k):
        return b * nqb + jnp.clip(qb_per_step * i - 1 + k, 0, nqb - 1)

    kblocks = [pl.BlockSpec((tq, kwidth), functools.partial(lambda b, i, k: (block_index(b, i, k), 0), k=k))
               for k in range(qb_per_step + 2)]
    vblocks = [pl.BlockSpec((VT_ROWS, tq), functools.partial(lambda b, i, k: (0, block_index(b, i, k)), k=k))
               for k in range(qb_per_step + 2)]
    return pl.pallas_call(
        functools.partial(_attn_kernel, window=True, tq=tq, qb_per_step=qb_per_step, n_steps=n_steps),
        grid=(n_batch, n_steps),
        in_specs=[
            pl.BlockSpec(memory_space=pltpu.SMEM),
            pl.BlockSpec((qb_per_step * tq, D_MODEL), lambda b, i: (b * n_steps + i, 0)),
            pl.BlockSpec((past, SUB), lambda b, i: (b, 0)),
            pl.BlockSpec((past, SUB), lambda b, i: (b, 0)),
            *kblocks, *vblocks,
            pl.BlockSpec((3 * WINDOW, 2 * tq), lambda b, i: (0, 0)),
        ],
        out_specs=pl.BlockSpec((qb_per_step * tq, D_MODEL), lambda b, i: (b * n_steps + i, 0)),
        out_shape=jax.ShapeDtypeStruct((t, D_MODEL), BF16),
        scratch_shapes=[pltpu.VMEM((past, kwidth), BF16), pltpu.VMEM((VT_ROWS, past), BF16)],
        compiler_params=_cparams(("parallel", "arbitrary")),
        name="attn_lat",
    )(sink, q, ck, cv, *([kp] * (qb_per_step + 2)), *([vt] * (qb_per_step + 2)), _window_bias_t(tq))


def _fgen_kernel(z_ref, w1_ref, b1_ref, w2_ref, b2_ref, fr_ref,
                 w3b_ref, b3b_ref, dcb_ref, w3f_ref, b3f_ref, dcf_ref, o_ref, a2_scr, *, n):
    @pl.when((pl.program_id(0) == 0) & (pl.program_id(1) == 0))
    def _():
        fr = fr_ref[...]
        a1 = jnp.sin(fr * (jnp.dot(z_ref[...], w1_ref[...], precision=HIGHEST,
                                   preferred_element_type=F32) + b1_ref[...]))
        a2_scr[...] = jnp.sin(fr * (jnp.dot(a1, w2_ref[...], precision=HIGHEST,
                                            preferred_element_type=F32) + b2_ref[...]))

    tc = o_ref.shape[1]
    tf = jnp.broadcast_to(z_ref[:, 0:1], (n, tc))
    tb = jnp.broadcast_to(z_ref[:, FILTER_WIDTH:FILTER_WIDTH + 1], (n, tc))
    a2 = a2_scr[...]
    hb = (jnp.dot(a2, w3b_ref[...], precision=HIGHEST, preferred_element_type=F32)
          + b3b_ref[...]) * jnp.exp(-tb * jnp.abs(dcb_ref[...]))
    hf = (jnp.dot(a2, w3f_ref[...], precision=HIGHEST, preferred_element_type=F32)
          + b3f_ref[...]) * jnp.exp(-tf * jnp.abs(dcf_ref[...]))
    tot = (jnp.sum(jnp.abs(hb), axis=0, keepdims=True)
           + jnp.sum(jnp.abs(hf), axis=0, keepdims=True))
    inv = 1.0 / (tot + EPS)
    rowid = lax.broadcasted_iota(jnp.int32, (n, tc), 0)
    o_ref[0:n, :] = jnp.where(rowid == 0, 0.0, hb * inv)
    o_ref[n:2 * n, :] = hf * inv


def _filter_gen(n, filt_w1, filt_b1, filt_w2, filt_b2, filt_w3, filt_b3, filt_freq, filt_decay, tc):
    t = np.arange(n, dtype=np.float64) / max(n - 1, 1)
    bands = np.arange(1, N_BANDS + 1, dtype=np.float64)
    ang = 2.0 * math.pi * t[:, None] * bands[None, :]
    z = np.concatenate([t[:, None], np.cos(ang), np.sin(ang)], axis=-1)
    zb = np.concatenate([z[0:1], z[1:][::-1]], axis=0)
    fw = FILTER_WIDTH
    zpad = np.zeros((n, fw - FILTER_EMB))
    zpacked = jnp.asarray(np.concatenate([z, zpad, zb, zpad], axis=1), dtype=F32)
    w1p = jnp.pad(filt_w1.astype(F32), ((0, fw - FILTER_EMB), (0, 0)))
    zero = jnp.zeros((fw, fw), F32)
    w1 = jnp.block([[w1p, zero], [zero, w1p]])
    w2f = filt_w2.astype(F32)
    w2 = jnp.block([[w2f, zero], [zero, w2f]])
    twice = lambda v: jnp.tile(v.astype(F32), 2).reshape(1, LANES)
    b1, b2, fr = twice(filt_b1), twice(filt_b2), twice(filt_freq)
    w3 = filt_w3.astype(F32)
    ncol = w3.shape[1]
    w3_fwd = jnp.concatenate([w3, jnp.zeros_like(w3)], axis=0)
    w3_bwd = jnp.concatenate([jnp.zeros_like(w3), w3], axis=0)
    b3 = filt_b3.astype(F32).reshape(1, ncol)
    dc = filt_decay.astype(F32).reshape(1, ncol)
    ct = D_HYENA // tc
    full = lambda shape: pl.BlockSpec(shape, lambda o, c: (0, 0))
    bwd = lambda rows: pl.BlockSpec((rows, tc), lambda o, c: (0, (2 * o + 1) * ct + c))
    fwd = lambda rows: pl.BlockSpec((rows, tc), lambda o, c: (0, (2 * o) * ct + c))
    return pl.pallas_call(
        functools.partial(_fgen_kernel, n=n),
        grid=(2, ct),
        in_specs=[
            full((n, LANES)), full((LANES, LANES)), full((1, LANES)),
            full((LANES, LANES)), full((1, LANES)), full((1, LANES)),
            bwd(LANES), bwd(1), bwd(1), fwd(LANES), fwd(1), fwd(1),
        ],
        out_specs=pl.BlockSpec((2 * n, tc), lambda o, c: (0, o * ct + c)),
        out_shape=jax.ShapeDtypeStruct((2 * n, 2 * D_HYENA), F32),
        scratch_shapes=[pltpu.VMEM((n, LANES), F32)],
        compiler_params=_cparams(("arbitrary", "arbitrary")),
        name="hyena_filter_gen",
    )(zpacked, w1, b1, w2, b2, fr, w3_bwd, b3, dc, w3_fwd, b3, dc)


def _dft_mats(blk):
    f = np.arange(blk, dtype=np.int64)[:, None]
    s = np.arange(blk, dtype=np.int64)[None, :]
    theta = (np.pi / (2 * blk)) * (((2 * f + 1) * s) % (4 * blk)).astype(np.float64)
    fwd = np.concatenate([np.cos(theta), -np.sin(theta)], axis=0)
    inv = np.concatenate([np.cos(theta).T, -np.sin(theta).T], axis=1) / blk
    return (jnp.asarray(fwd, dtype=F32).astype(BF16), jnp.asarray(inv, dtype=F32).astype(BF16))


def _ftf_kernel(k_ref, skip_ref, f_ref, o_ref, *, nblk2, blk):
    tc = o_ref.shape[2]
    fmat = f_ref[...]
    fidx = lax.broadcasted_iota(jnp.int32, (blk, tc), 0)
    sgn = jnp.where(fidx % 2 == 0, 1.0, -1.0)
    prev = None
    for e in range(nblk2):
        p = jnp.dot(fmat, k_ref[e * blk:(e + 1) * blk, :].astype(BF16), preferred_element_type=F32)
        if e >= 1:
            re = p[0:blk] - sgn * prev[blk:2 * blk]
            im = p[blk:2 * blk] + sgn * prev[0:blk]
            if e == nblk2 // 2:
                re = re + skip_ref[...]
            o_ref[e - 1, 0:blk, :] = re
            o_ref[e - 1, blk:2 * blk, :] = im
        prev = p


def _filter_transform(kfull, skip, fmat, n, blk, tc):
    nblk2 = 2 * n // blk
    nd = nblk2 - 1
    ct = D_HYENA // tc
    return pl.pallas_call(
        functools.partial(_ftf_kernel, nblk2=nblk2, blk=blk),
        grid=(2, ct),
        in_specs=[
            pl.BlockSpec((2 * n, tc), lambda o, c: (0, o * ct + c)),
            pl.BlockSpec((None, 1, tc), lambda o, c: (o, 0, c)),
            pl.BlockSpec((2 * blk, blk), lambda o, c: (0, 0)),
        ],
        out_specs=pl.BlockSpec((nd, 2 * blk, tc), lambda o, c: (o, 0, c)),
        out_shape=jax.ShapeDtypeStruct((2 * nd, 2 * blk, D_HYENA), F32),
        compiler_params=_cparams(("parallel", "parallel")),
        name="hyena_filter_transform",
    )(kfull, skip, fmat)


CHUNK = 32
HALO = 8
SC_ROWS = 128


def _short_conv_block(src_ref, b, j, n, blk, w_ref, b_ref):
    pieces = []
    for r0 in range(j * blk, (j + 1) * blk, SC_ROWS):
        lo = max(r0 - HALO, 0)
        hi = min(r0 + SC_ROWS + HALO, n)
        off = r0 - lo
        win = src_ref[b, lo:hi, :]
        prev = pltpu.roll(win, 1, axis=0)[off:off + SC_ROWS]
        nxt = pltpu.roll(win, hi - lo - 1, axis=0)[off:off + SC_ROWS]
        u = win[off:off + SC_ROWS]
        rowid = lax.broadcasted_iota(jnp.int32, u.shape, 0)
        if r0 == 0:
            prev = jnp.where(rowid == 0, 0.0, prev)
        if r0 + SC_ROWS == n:
            nxt = jnp.where(rowid == SC_ROWS - 1, 0.0, nxt)
        pieces.append(prev * w_ref[0:1, :] + u * w_ref[1:2, :] + nxt * w_ref[2:3, :] + b_ref[...])
    return jnp.concatenate(pieces, axis=0)


def _conv_kernel(zin_ref, gin_ref, wz_ref, bz_ref, wg_ref, bg_ref, g_ref, f_ref, fi_ref,
                 o_ref, zf_scr, yf_scr, *, n, blk, bb, conv_z):
    nblk = n // blk
    tc = o_ref.shape[2]
    fmat = f_ref[...]
    for j in range(nblk):
        rows = slice(j * blk, (j + 1) * blk)
        zs = []
        for b in range(bb):
            if conv_z:
                z = _short_conv_block(zin_ref, b, j, n, blk, wz_ref, bz_ref)
            else:
                z = zin_ref[b, rows, :]
            zs.append(z.astype(BF16))
        zf_scr[j] = jnp.dot(fmat, jnp.concatenate(zs, axis=1), preferred_element_type=F32)
        for b in range(bb):
            o_ref[b, rows, :] = _short_conv_block(gin_ref, b, j, n, blk, wg_ref, bg_ref)

    def pairs(i, slot):
        for r in range(blk // CHUNK):
            re_rows = slice(r * CHUNK, (r + 1) * CHUNK)
            im_rows = slice(blk + r * CHUNK, blk + (r + 1) * CHUNK)
            acc_re = [jnp.zeros((CHUNK, tc), F32) for _ in range(bb)]
            acc_im = [jnp.zeros((CHUNK, tc), F32) for _ in range(bb)]
            for j in range(nblk):
                d = i - j + (nblk - 1)
                g_re = g_ref[d, re_rows, :]
                g_im = g_ref[d, im_rows, :]
                for b in range(bb):
                    cols = slice(b * tc, (b + 1) * tc)
                    z_re = zf_scr[j, re_rows, cols]
                    z_im = zf_scr[j, im_rows, cols]
                    acc_re[b] = acc_re[b] + (g_re * z_re - g_im * z_im)
                    acc_im[b] = acc_im[b] + (g_re * z_im + g_im * z_re)
            for b in range(bb):
                cols = slice(b * tc, (b + 1) * tc)
                yf_scr[slot, re_rows, cols] = acc_re[b].astype(BF16)
                yf_scr[slot, im_rows, cols] = acc_im[b].astype(BF16)

    def inverse(slot):
        return jnp.dot(fi_ref[...], yf_scr[slot], preferred_element_type=F32)

    def gate_out(i, y):
        rows = pl.ds(pl.multiple_of(i * blk, blk), blk)
        for b in range(bb):
            o_ref[b, rows, :] = o_ref[b, rows, :] * y[:, b * tc:(b + 1) * tc]

    pairs(0, 0)
    if nblk > 1:
        def body(i, carry):
            y = inverse((i - 1) & 1)
            pairs(i, i & 1)
            gate_out(i - 1, y)
            return carry
        lax.fori_loop(1, nblk, body, 0)
    gate_out(nblk - 1, inverse((nblk - 1) & 1))


def _long_conv(zsrc, zcol, gsrc, gcol, conv_w, conv_b, gspec, order, fmat, fimat,
               n_batch, n, blk, tc, bb, conv_z):
    nblk = n // blk
    nd = 2 * nblk - 1
    ct = D_HYENA // tc
    hy0 = GH_X1 // tc
    wz = (zcol - hy0) if conv_z else 0
    wg = gcol - hy0
    return pl.pallas_call(
        functools.partial(_conv_kernel, n=n, blk=blk, bb=bb, conv_z=conv_z),
        grid=(ct, n_batch // bb),
        in_specs=[
            pl.BlockSpec((bb, n, tc), lambda c, b: (b, 0, zcol + c)),
            pl.BlockSpec((bb, n, tc), lambda c, b: (b, 0, gcol + c)),
            pl.BlockSpec((3, tc), lambda c, b: (0, wz + c)),
            pl.BlockSpec((1, tc), lambda c, b: (0, wz + c)),
            pl.BlockSpec((3, tc), lambda c, b: (0, wg + c)),
            pl.BlockSpec((1, tc), lambda c, b: (0, wg + c)),
            _resident((nd, 2 * blk, tc), lambda c, b: (order, 0, c)),
            _resident((2 * blk, blk), lambda c, b: (0, 0)),
            _resident((blk, 2 * blk), lambda c, b: (0, 0)),
        ],
        out_specs=pl.BlockSpec((bb, n, tc), lambda c, b: (b, 0, c)),
        out_shape=jax.ShapeDtypeStruct((n_batch, n, D_HYENA), F32),
        scratch_shapes=[
            pltpu.VMEM((nblk, 2 * blk, bb * tc), F32),
            pltpu.VMEM((2, 2 * blk, bb * tc), BF16),
        ],
        compiler_params=_cparams(("parallel", "arbitrary")),
        name="hyena_conv",
    )(zsrc, gsrc, conv_w, conv_b, conv_w, conv_b, gspec, fmat, fimat)


def _hyena(gh, n_batch, n, blk, tc, bb, conv_w, conv_b, skip, filt):
    fmat, fimat = _dft_mats(blk)
    kfull = _filter_gen(n, *filt, tc=128)
    sk = skip.astype(F32).reshape(2, 1, D_HYENA)
    gspec = _filter_transform(kfull, sk, fmat, n, blk, tc=256)
    cw = conv_w.astype(F32)
    cb = conv_b.astype(F32).reshape(1, -1)
    gh3 = gh.reshape(n_batch, n, GH_COLS)
    z2 = _long_conv(gh3, GH_HV // tc, gh3, GH_X1 // tc, cw, cb, gspec, 0, fmat, fimat,
                    n_batch, n, blk, tc, bb, conv_z=True)
    y = _long_conv(z2, 0, gh3, GH_X2 // tc, cw, cb, gspec, 1, fmat, fimat,
                   n_batch, n, blk, tc, bb, conv_z=False)
    return y.reshape(n_batch * n, D_HYENA)


def _outproj_kernel(x_ref, attn_ref, hy_ref, ga_ref, gh_ref, mod_ref, wa_ref, wh_ref, wo_ref, o_ref):
    n_chunks = 2
    rc = x_ref.shape[0] // n_chunks

    def branches(c):
        rows = slice(c * rc, (c + 1) * rc)
        a = jnp.dot(attn_ref[rows, :].astype(BF16), wa_ref[...], preferred_element_type=F32)
        h = jnp.dot(hy_ref[rows, :].astype(BF16), wh_ref[...], preferred_element_type=F32)
        return a, h

    pending = {0: branches(0)}
    for c in range(n_chunks):
        if c + 1 < n_chunks:
            pending[c + 1] = branches(c + 1)
        a, h = pending.pop(c)
        rows = slice(c * rc, (c + 1) * rc)
        merged = ga_ref[rows, :] * a + gh_ref[rows, :] * h
        out = jnp.dot(merged.astype(BF16), wo_ref[...], preferred_element_type=F32)
        o_ref[rows, :] = x_ref[rows, :] + mod_ref[0, 5:6, :] * out


def _outproj(x, attn, hy, gh, mod3, tiles_per_mod, wa, wh, wo, tm):
    t = x.shape[0]
    mod_map = (lambda i: (i // tiles_per_mod, 0, 0)) if tiles_per_mod else (lambda i: (0, 0, 0))
    row = lambda i: (i, 0)
    wspec = _resident((D_MODEL, D_MODEL), lambda i: (0, 0))
    return pl.pallas_call(
        _outproj_kernel,
        grid=(t // tm,),
        in_specs=[
            pl.BlockSpec((tm, D_MODEL), row),
            pl.BlockSpec((tm, D_MODEL), row),
            pl.BlockSpec((tm, D_MODEL), row),
            pl.BlockSpec((tm, D_MODEL), lambda i: (i, GH_GA // D_MODEL)),
            pl.BlockSpec((tm, D_MODEL), lambda i: (i, GH_GH // D_MODEL)),
            pl.BlockSpec((1, N_MOD, D_MODEL), mod_map),
            wspec, wspec, wspec,
        ],
        out_specs=pl.BlockSpec((tm, D_MODEL), row),
        out_shape=jax.ShapeDtypeStruct((t, D_MODEL), F32),
        compiler_params=_cparams(("parallel",)),
        name="mixer_out",
    )(x, attn, hy, gh, gh, mod3, wa, wh, wo)


def _rope_tables(n):
    pos = np.arange(n)
    row = (pos // GRID_W).astype(np.float64)
    col = (pos % GRID_W).astype(np.float64)
    n_freq = HEAD_DIM // 4
    inv = ROPE_BASE ** (-np.arange(n_freq, dtype=np.float64) / n_freq)
    ar = row[:, None] * inv[None, :]
    ac = col[:, None] * inv[None, :]
    cos = np.concatenate([np.cos(ar), np.cos(ar), np.cos(ac), np.cos(ac)], axis=-1)
    sin = np.concatenate([-np.sin(ar), np.sin(ar), -np.sin(ac), np.sin(ac)], axis=-1)
    reps = SUB // HEAD_DIM
    return (jnp.asarray(np.tile(cos, (1, reps)), dtype=F32), jnp.asarray(np.tile(sin, (1, reps)), dtype=F32))


def _layer(x, mod3, per_batch_mod, n_batch, seq, w, ctx_kv, hy_blk, hy_tc, hy_bb):
    tmf = 1024
    ffn_tiles_per_mod = (seq // tmf) if per_batch_mod else 0
    x = _ffn(x, mod3, ffn_tiles_per_mod, w["norm_ffn1"], w["ffn1_wi"], w["ffn1_wo"], 0, tmf, 256)
    qkv = None
    tmi = 512
    inproj_tiles_per_mod = (seq // tmi) if per_batch_mod else 0
    if ctx_kv is None:
        qkv, gh = _inproj(x, mod3, inproj_tiles_per_mod, w["norm_mix"], w["w_qkv"], w["w_gh"], w["qg"],
                          w["kg"], w["bd"], None, tmi, attn_layout=False)
        attn = _attention_ctx(qkv, w["sink"], n_batch, seq)
    else:
        q, kp, vt, gh = _inproj(x, mod3, inproj_tiles_per_mod, w["norm_mix"], w["w_qkv"], w["w_gh"],
                                w["qg"], w["kg"], w["bd"], _rope_tables(seq), tmi, attn_layout=True)
        attn = _attention_lat(q, kp, vt, ctx_kv[0], ctx_kv[1], w["sink"], n_batch, seq,
                              ctx_kv[0].shape[0] // n_batch)
    hy = _hyena(gh, n_batch, seq, hy_blk, hy_tc, hy_bb, w["conv_w"], w["conv_b"], w["hyena_skip"], w["filt"])
    tmo = 512
    x = _outproj(x, attn, hy, gh, mod3, (seq // tmo) if per_batch_mod else 0,
                 w["wa"], w["wh"], w["wo"], tmo)
    x = _ffn(x, mod3, ffn_tiles_per_mod, w["norm_ffn2"], w["ffn2_wi"], w["ffn2_wo"], 6, tmf, 256)
    return x, qkv


def kernel(x_prompt, x_sample, cache_k, cache_v, c, c_ctx, w_mod, b_mod, norm_ffn1, ffn1_wi, ffn1_wo, norm_mix, w_in, q_norm, k_norm, attn_sink, conv_w, conv_b, filt_w1, filt_b1, filt_w2, filt_b2, filt_w3, filt_b3, filt_freq, filt_decay, hyena_skip, w_attn_branch, w_hyena_branch, w_out, norm_ffn2, ffn2_wi, ffn2_wo):
    batch, seq, _ = x_prompt.shape
    dec_batch, dec_seq, _ = x_sample.shape
    depth = w_mod.shape[0]
    past = cache_k.shape[2]

    yp = x_prompt.reshape(batch * seq, D_MODEL)
    ys = x_sample.reshape(dec_batch * dec_seq, D_MODEL)
    bd = jnp.asarray(np.kron(np.eye(SUB // HEAD_DIM), np.ones((HEAD_DIM, HEAD_DIM))), dtype=F32).astype(BF16)
    new_ks, new_vs = [], []
    for l in range(depth):
        c_rows = jnp.zeros((16, D_MODEL), F32).at[0:dec_batch].set(c).at[dec_batch].set(c_ctx)
        mod = _modulation(c_rows, w_mod[l], b_mod[l]).reshape(16, N_MOD, D_MODEL)
        mod_lat = mod[0:dec_batch]
        mod_ctx = mod[dec_batch:dec_batch + 1]
        wl = w_in[l]
        hy_end = QKV_COLS + 3 * D_HYENA
        w = {
            "norm_ffn1": norm_ffn1[l], "ffn1_wi": ffn1_wi[l].astype(BF16), "ffn1_wo": ffn1_wo[l].astype(BF16),
            "norm_mix": norm_mix[l],
            "w_qkv": wl[:, 0:QKV_COLS].astype(BF16),
            "w_gh": jnp.concatenate([wl[:, hy_end:], wl[:, QKV_COLS:hy_end]], axis=1).astype(BF16),
            "qg": jnp.tile(q_norm[l], SUB // HEAD_DIM).reshape(1, SUB),
            "kg": jnp.tile(k_norm[l], SUB // HEAD_DIM).reshape(1, SUB),
            "bd": bd, "sink": attn_sink[l],
            "conv_w": conv_w[l], "conv_b": conv_b[l], "hyena_skip": hyena_skip[l],
            "filt": (filt_w1[l], filt_b1[l], filt_w2[l], filt_b2[l], filt_w3[l], filt_b3[l],
                     filt_freq[l], filt_decay[l]),
            "wa": w_attn_branch[l].astype(BF16), "wh": w_hyena_branch[l].astype(BF16),
            "wo": w_out[l].astype(BF16),
            "norm_ffn2": norm_ffn2[l], "ffn2_wi": ffn2_wi[l].astype(BF16), "ffn2_wo": ffn2_wo[l].astype(BF16),
        }
        yp, qkv_p = _layer(yp, mod_ctx, False, batch, seq, w, None, hy_blk=256, hy_tc=128, hy_bb=8)
        ck = cache_k[:, l].reshape(dec_batch * past, N_KV_HEADS * HEAD_DIM)
        cv = cache_v[:, l].reshape(dec_batch * past, N_KV_HEADS * HEAD_DIM)
        ys, _ = _layer(ys, mod_lat, True, dec_batch, dec_seq, w, (ck, cv), hy_blk=1024, hy_tc=128, hy_bb=2)
        new_ks.append(qkv_p[:, QKV_K:QKV_K + SUB].reshape(batch, seq, N_KV_HEADS, HEAD_DIM))
        new_vs.append(qkv_p[:, QKV_V:QKV_V + SUB].reshape(batch, seq, N_KV_HEADS, HEAD_DIM))
    new_k = jnp.stack(new_ks, axis=1)
    new_v = jnp.stack(new_vs, axis=1)
    return (yp.reshape(batch, seq, D_MODEL), ys.reshape(dec_batch, dec_seq, D_MODEL), new_k, new_v)
```

```python
import functools
import math

import numpy as np
import jax
import jax.numpy as jnp
from jax import lax
from jax.experimental import pallas as pl
from jax.experimental.pallas import tpu as pltpu

F32 = jnp.float32
BF16 = jnp.bfloat16
HIGHEST = lax.Precision.HIGHEST

D_MODEL = 1024
N_HEADS = 16
N_KV_HEADS = 4
HEAD_DIM = 64
GRID_W = 64
WINDOW = 128
ROPE_BASE = 10000.0
D_HYENA = 1024
N_BANDS = 16
FILTER_EMB = 1 + 2 * N_BANDS
FILTER_WIDTH = 64
D_FF = 2816
N_MOD = 9
EPS = 1e-6
NEG_INF = -1e30
LOG2E = math.log2(math.e)

QKV_COLS = 1536
QKV_K = 1024
QKV_V = 1280
GH_COLS = 5120
GH_GA = 0
GH_GH = 1024
GH_X1 = 2048
GH_X2 = 3072
GH_HV = 4096
SUB = 256

LANES = 128
VMEM_LIMIT = 61 * 1024 * 1024


def _cparams(sem):
    return pltpu.CompilerParams(dimension_semantics=sem, vmem_limit_bytes=VMEM_LIMIT)


def _resident(shape, index_map):
    return pl.BlockSpec(shape, index_map, pipeline_mode=pl.Buffered(1))


def _mod_kernel(c_ref, w_ref, b_ref, o_ref):
    c = c_ref[...]
    s = c * jax.nn.sigmoid(c)
    o_ref[...] = jnp.dot(s, w_ref[...], precision=HIGHEST, preferred_element_type=F32) + b_ref[...]


def _modulation(c_rows, w_mod, b_mod):
    rows = c_rows.shape[0]
    n_out = w_mod.shape[1]
    tn = 1024
    return pl.pallas_call(
        _mod_kernel,
        grid=(n_out // tn,),
        in_specs=[
            pl.BlockSpec((rows, D_MODEL), lambda j: (0, 0)),
            pl.BlockSpec((D_MODEL, tn), lambda j: (0, j)),
            pl.BlockSpec((1, tn), lambda j: (0, j)),
        ],
        out_specs=pl.BlockSpec((rows, tn), lambda j: (0, j)),
        out_shape=jax.ShapeDtypeStruct((rows, n_out), F32),
        compiler_params=_cparams(("arbitrary",)),
        name="modulation",
    )(c_rows, w_mod, b_mod.reshape(1, n_out))


def _norm_modulate(x, g, shift, scale):
    ms = jnp.mean(x * x, axis=-1, keepdims=True)
    y = x * lax.rsqrt(ms + EPS) * g
    return y * (1.0 + scale) + shift


def _ffn_kernel(x_ref, mod_ref, g_ref, wi_ref, wo_ref, o_ref, *, mod_base, tf):
    x = x_ref[...]
    shift = mod_ref[0, mod_base:mod_base + 1, :]
    scale = mod_ref[0, mod_base + 1:mod_base + 2, :]
    h = _norm_modulate(x, g_ref[...], shift, scale).astype(BF16)
    nj = D_FF // tf

    def gate_up(j):
        gate = jnp.dot(h, wi_ref[:, j * tf:(j + 1) * tf], preferred_element_type=F32)
        up = jnp.dot(h, wi_ref[:, D_FF + j * tf:D_FF + (j + 1) * tf], preferred_element_type=F32)
        return gate, up

    ahead = 2
    pending = {j: gate_up(j) for j in range(ahead)}
    acc = None
    for j in range(nj):
        if j + ahead < nj:
            pending[j + ahead] = gate_up(j + ahead)
        gate, up = pending.pop(j)
        a = (gate * jax.nn.sigmoid(gate) * up).astype(BF16)
        d = jnp.dot(a, wo_ref[j * tf:(j + 1) * tf, :], preferred_element_type=F32)
        acc = d if acc is None else acc + d
    gmod = mod_ref[0, mod_base + 2:mod_base + 3, :]
    o_ref[...] = x + (0.5 * gmod) * acc


def _ffn(x, mod3, tiles_per_mod, g, wi, wo, mod_base, tm, tf):
    t = x.shape[0]
    mod_map = (lambda i: (i // tiles_per_mod, 0, 0)) if tiles_per_mod else (lambda i: (0, 0, 0))
    return pl.pallas_call(
        functools.partial(_ffn_kernel, mod_base=mod_base, tf=tf),
        grid=(t // tm,),
        in_specs=[
            pl.BlockSpec((tm, D_MODEL), lambda i: (i, 0)),
            pl.BlockSpec((1, N_MOD, D_MODEL), mod_map),
            pl.BlockSpec((1, D_MODEL), lambda i: (0, 0)),
            _resident((D_MODEL, 2 * D_FF), lambda i: (0, 0)),
            _resident((D_FF, D_MODEL), lambda i: (0, 0)),
        ],
        out_specs=pl.BlockSpec((tm, D_MODEL), lambda i: (i, 0)),
        out_shape=jax.ShapeDtypeStruct((t, D_MODEL), F32),
        compiler_params=_cparams(("parallel",)),
        name="ffn",
    )(x, mod3, g.reshape(1, D_MODEL), wi, wo)


def _rope_partner(y):
    lane = lax.broadcasted_iota(jnp.int32, (1, LANES), 1)
    first = (lane % 32) < 16
    parts = []
    for c in range(y.shape[1] // LANES):
        yc = y[:, c * LANES:(c + 1) * LANES]
        fwd = pltpu.roll(yc, LANES - 16, axis=1)
        bwd = pltpu.roll(yc, 16, axis=1)
        parts.append(jnp.where(first, fwd, bwd))
    return jnp.concatenate(parts, axis=1)


def _place_halves(x, own_half):
    lane = lax.broadcasted_iota(jnp.int32, (1, LANES), 1)
    in_half = [lane < 64, lane >= 64]
    swapped = pltpu.roll(x, 64, axis=1)
    out = [None, None]
    out[own_half] = jnp.where(in_half[own_half], x, 0.0).astype(BF16)
    out[1 - own_half] = jnp.where(in_half[1 - own_half], swapped, 0.0).astype(BF16)
    return out


def _store_placed_keys(k, dst_ref):
    for g in range(N_KV_HEADS):
        halves = _place_halves(k[:, (g // 2) * LANES:(g // 2 + 1) * LANES], g % 2)
        for a in range(2):
            dst_ref[:, (2 * g + a) * LANES:(2 * g + a + 1) * LANES] = halves[a]


ONES_ROWS = 16
VROWS = HEAD_DIM + ONES_ROWS
VT_ROWS = N_KV_HEADS * VROWS


def _store_values_t(v, dst_ref):
    vt = v.T
    ones = jnp.ones((ONES_ROWS, v.shape[0]), BF16)
    for g in range(N_KV_HEADS):
        dst_ref[g * VROWS:g * VROWS + HEAD_DIM, :] = vt[g * HEAD_DIM:(g + 1) * HEAD_DIM].astype(BF16)
        dst_ref[g * VROWS + HEAD_DIM:(g + 1) * VROWS, :] = ones


def _inproj_kernel(*refs, rope, attn_layout):
    refs = list(refs)
    x_ref, mod_ref, g_ref, wq_ref, wg_ref, qg_ref, kg_ref, bd_ref = refs[:8]
    refs = refs[8:]
    if rope:
        cos_ref, sin_ref = refs[:2]
        refs = refs[2:]
    if attn_layout:
        q_ref, kp_ref, vt_ref, gh_ref = refs
    else:
        qkv_ref, gh_ref = refs

    h = _norm_modulate(x_ref[...], g_ref[...], mod_ref[0, 3:4, :], mod_ref[0, 4:5, :]).astype(BF16)

    def head_norm(t, gain):
        ss = jnp.dot((t * t).astype(BF16), bd_ref[...], preferred_element_type=F32)
        y = t * lax.rsqrt(ss * (1.0 / HEAD_DIM) + EPS) * gain
        if rope:
            y = y * cos_ref[...] + _rope_partner(y) * sin_ref[...]
        return y

    def q_epilogue(acc, c0):
        gain = qg_ref[...] * (LOG2E / math.sqrt(HEAD_DIM))
        for c in range(acc.shape[1] // SUB):
            y = head_norm(acc[:, c * SUB:(c + 1) * SUB], gain)
            cols = slice(c0 + c * SUB, c0 + (c + 1) * SUB)
            if attn_layout:
                q_ref[:, cols] = y.astype(BF16)
            else:
                qkv_ref[:, cols] = y

    def kv_epilogue(acc, c0):
        k = head_norm(acc[:, 0:SUB], kg_ref[...])
        v = acc[:, SUB:2 * SUB]
        if attn_layout:
            _store_placed_keys(k, kp_ref)
            _store_values_t(v, vt_ref)
        else:
            qkv_ref[:, QKV_K:QKV_K + SUB] = k
            qkv_ref[:, QKV_V:QKV_V + SUB] = v

    def gate_epilogue(acc, c0):
        gh_ref[:, c0:c0 + acc.shape[1]] = jax.nn.sigmoid(acc)

    def raw_epilogue(acc, c0):
        gh_ref[:, c0:c0 + acc.shape[1]] = acc

    half = QKV_K // 2
    chunks = [(wq_ref, 0, half, q_epilogue), (wq_ref, half, half, q_epilogue),
              (wq_ref, QKV_K, 2 * SUB, kv_epilogue)]
    for c0 in range(0, GH_COLS, D_MODEL):
        chunks.append((wg_ref, c0, D_MODEL, gate_epilogue if c0 < GH_X1 else raw_epilogue))

    def matmul(idx):
        w_ref, c0, width, _ = chunks[idx]
        return jnp.dot(h, w_ref[:, c0:c0 + width], preferred_element_type=F32)

    ahead = 2
    pending = {idx: matmul(idx) for idx in range(ahead)}
    for idx in range(len(chunks)):
        if idx + ahead < len(chunks):
            pending[idx + ahead] = matmul(idx + ahead)
        chunks[idx][3](pending.pop(idx), chunks[idx][1])


def _inproj(x, mod3, tiles_per_mod, g, wq, wg, qg, kg, bd, rope_tabs, tm, attn_layout):
    t = x.shape[0]
    rope = rope_tabs is not None
    mod_map = (lambda i: (i // tiles_per_mod, 0, 0)) if tiles_per_mod else (lambda i: (0, 0, 0))
    row = lambda i: (i, 0)
    const = lambda i: (0, 0)
    in_specs = [
        pl.BlockSpec((tm, D_MODEL), row),
        pl.BlockSpec((1, N_MOD, D_MODEL), mod_map),
        pl.BlockSpec((1, D_MODEL), const),
        _resident((D_MODEL, QKV_COLS), const),
        _resident((D_MODEL, GH_COLS), const),
        pl.BlockSpec((1, SUB), const),
        pl.BlockSpec((1, SUB), const),
        pl.BlockSpec((SUB, SUB), const),
    ]
    args = [x, mod3, g.reshape(1, D_MODEL), wq, wg, qg, kg, bd]
    if rope:
        seq_tiles = rope_tabs[0].shape[0] // tm
        for tab in rope_tabs:
            in_specs.append(pl.BlockSpec((tm, SUB), lambda i: (i % seq_tiles, 0)))
            args.append(tab)
    gh_spec = pl.BlockSpec((tm, GH_COLS), row)
    gh_shape = jax.ShapeDtypeStruct((t, GH_COLS), F32)
    if attn_layout:
        kwidth = 2 * N_KV_HEADS * LANES
        out_specs = [pl.BlockSpec((tm, QKV_K), row), pl.BlockSpec((tm, kwidth), row),
                     pl.BlockSpec((VT_ROWS, tm), lambda i: (0, i)), gh_spec]
        out_shape = [jax.ShapeDtypeStruct((t, QKV_K), BF16), jax.ShapeDtypeStruct((t, kwidth), BF16),
                     jax.ShapeDtypeStruct((VT_ROWS, t), BF16), gh_shape]
    else:
        out_specs = [pl.BlockSpec((tm, QKV_COLS), row), gh_spec]
        out_shape = [jax.ShapeDtypeStruct((t, QKV_COLS), F32), gh_shape]
    return pl.pallas_call(
        functools.partial(_inproj_kernel, rope=rope, attn_layout=attn_layout),
        grid=(t // tm,),
        in_specs=in_specs,
        out_specs=out_specs,
        out_shape=out_shape,
        compiler_params=_cparams(("parallel",)),
        name="inproj",
    )(*args)


def _col_reduce(x, op):
    rows, cols = x.shape
    part = op(x.reshape(8, rows // 8, cols), axis=0)
    return op(part, axis=0, keepdims=True)


def _attn_kernel(*refs, window, tq, qb_per_step, n_steps):
    refs = list(refs)
    sink_ref, q_ref, kc_ref, vc_ref = refs[:4]
    refs = refs[4:]
    if window:
        n_kb = qb_per_step + 2
        kb_refs, vb_refs, bias_ref = refs[:n_kb], refs[n_kb:2 * n_kb], refs[2 * n_kb]
        refs = refs[2 * n_kb + 1:]
    o_ref, kcp_scr, vct_scr = refs
    i = pl.program_id(1)

    @pl.when(i == 0)
    def _():
        _store_placed_keys(kc_ref[...], kcp_scr)
        _store_values_t(vc_ref[...], vct_scr)

    if window:
        krow = lax.broadcasted_iota(jnp.int32, (3 * WINDOW, 2 * tq), 0)
        band = bias_ref[...]
        no_prev = jnp.where((krow < WINDOW) & (i == 0), NEG_INF, 0.0)
        no_next = jnp.where((krow >= 2 * WINDOW) & (i == n_steps - 1), NEG_INF, 0.0)
        biases = [band] * qb_per_step
        biases[0] = biases[0] + no_prev
        biases[-1] = biases[-1] + no_next
    lane = lax.broadcasted_iota(jnp.int32, (1, 2 * tq), 1)
    dn = (((1,), (1,)), ((), ()))
    nkc = kcp_scr.shape[0]
    heads_units = 2 * N_KV_HEADS
    n_units = qb_per_step * heads_units

    def scores(unit):
        qb, u = divmod(unit, heads_units)
        g, a = divmod(u, 2)
        rows = slice(qb * tq, (qb + 1) * tq)
        q0 = q_ref[rows, g * 2 * LANES:g * 2 * LANES + LANES]
        q1 = q_ref[rows, g * 2 * LANES + LANES:(g + 1) * 2 * LANES]
        qg = jnp.concatenate([q0, q1], axis=0).astype(BF16)
        kcols = slice(u * LANES, (u + 1) * LANES)
        keys = [kcp_scr[:, kcols]]
        if window:
            keys += [kb_refs[qb + t][:, kcols] for t in range(3)]
        return lax.dot_general(jnp.concatenate(keys, axis=0), qg, dn, preferred_element_type=F32)

    def softmax(unit, s):
        qb, u = divmod(unit, heads_units)
        g, a = divmod(u, 2)
        sink = jnp.where(lane < tq, sink_ref[4 * g + a], sink_ref[4 * g + 2 + a]) * LOG2E
        s_c = s[0:nkc]
        m = jnp.maximum(_col_reduce(s_c, jnp.max), sink)
        if window:
            s_w = s[nkc:] + biases[qb]
            m = jnp.maximum(m, _col_reduce(s_w, jnp.max))
        probs = [jnp.exp2(s_c - m).astype(BF16)]
        if window:
            probs.append(jnp.exp2(s_w - m).astype(BF16))
        return probs, jnp.exp2(sink - m)

    def values(unit, probs, sink_term):
        qb, u = divmod(unit, heads_units)
        g = u // 2
        vrows = slice(g * VROWS, (g + 1) * VROWS)
        o = jnp.dot(vct_scr[vrows, :], probs[0], preferred_element_type=F32)
        if window:
            vw_t = jnp.concatenate([vb_refs[qb + t][vrows, :] for t in range(3)], axis=1)
            o = o + jnp.dot(vw_t, probs[1], preferred_element_type=F32)
        l = o[HEAD_DIM:HEAD_DIM + 1, :] + sink_term
        return o[0:HEAD_DIM, :] * (1.0 / l)

    lead = 3
    outs = []
    s_q = {unit: scores(unit) for unit in range(lead)}
    p_q = {}
    for step in range(n_units + 1):
        if step + lead < n_units:
            s_q[step + lead] = scores(step + lead)
        if 0 <= step < n_units:
            p_q[step] = softmax(step, s_q.pop(step))
        unit = step - 1
        if 0 <= unit < n_units:
            outs.append(values(unit, *p_q.pop(unit)))
            if unit % 2 == 1:
                qb, u = divmod(unit, heads_units)
                g = u // 2
                rows = slice(qb * tq, (qb + 1) * tq)
                o_t = jnp.concatenate(outs[-2:], axis=0)
                o_ref[rows, g * 2 * LANES:g * 2 * LANES + LANES] = o_t[:, 0:tq].T.astype(BF16)
                o_ref[rows, g * 2 * LANES + LANES:(g + 1) * 2 * LANES] = o_t[:, tq:2 * tq].T.astype(BF16)


def _window_bias_t(tq):
    a = np.arange(WINDOW)[None, :]
    b = np.arange(WINDOW)[:, None]
    prev = np.where(b >= a, 0.0, NEG_INF)
    cur = np.zeros((WINDOW, WINDOW))
    nxt = np.where(b <= a, 0.0, NEG_INF)
    one = np.concatenate([prev, cur, nxt], axis=0).astype(np.float32)
    return jnp.asarray(np.concatenate([one, one], axis=1))


def _attention_ctx(qkv, sink, n_batch, seq):
    t = qkv.shape[0]
    tq = seq
    kblk = QKV_K // SUB
    vblk = QKV_V // SUB
    return pl.pallas_call(
        functools.partial(_attn_kernel, window=False, tq=tq, qb_per_step=1, n_steps=1),
        grid=(n_batch, 1),
        in_specs=[
            pl.BlockSpec(memory_space=pltpu.SMEM),
            pl.BlockSpec((tq, D_MODEL), lambda b, i: (b, 0)),
            pl.BlockSpec((seq, SUB), lambda b, i: (b, kblk)),
            pl.BlockSpec((seq, SUB), lambda b, i: (b, vblk)),
        ],
        out_specs=pl.BlockSpec((tq, D_MODEL), lambda b, i: (b, 0)),
        out_shape=jax.ShapeDtypeStruct((t, D_MODEL), BF16),
        scratch_shapes=[pltpu.VMEM((seq, 2 * N_KV_HEADS * LANES), BF16), pltpu.VMEM((VT_ROWS, seq), BF16)],
        compiler_params=_cparams(("parallel", "arbitrary")),
        name="attn_ctx",
    )(sink, qkv, qkv, qkv)


def _attention_lat(q, kp, vt, ck, cv, sink, n_batch, seq, past, qb_per_step=8):
    t = q.shape[0]
    tq = WINDOW
    nqb = seq // tq
    n_steps = nqb // qb_per_step
    kwidth = 2 * N_KV_HEADS * LANES

    def block_index(b, i, k):
        return b * nqb + jnp.clip(qb_per_step * i - 1 + k, 0, nqb - 1)

    kblocks = [pl.BlockSpec((tq, kwidth), functools.partial(lambda b, i, k: (block_index(b, i, k), 0), k=k))
               for k in range(qb_per_step + 2)]
    vblocks = [pl.BlockSpec((VT_ROWS, tq), functools.partial(lambda b, i, k: (0, block_index(b, i, k)), k=k))
               for k in range(qb_per_step + 2)]
    return pl.pallas_call(
        functools.partial(_attn_kernel, window=True, tq=tq, qb_per_step=qb_per_step, n_steps=n_steps),
        grid=(n_batch, n_steps),
        in_specs=[
            pl.BlockSpec(memory_space=pltpu.SMEM),
            pl.BlockSpec((qb_per_step * tq, D_MODEL), lambda b, i: (b * n_steps + i, 0)),
            pl.BlockSpec((past, SUB), lambda b, i: (b, 0)),
            pl.BlockSpec((past, SUB), lambda b, i: (b, 0)),
            *kblocks, *vblocks,
            pl.BlockSpec((3 * WINDOW, 2 * tq), lambda b, i: (0, 0)),
        ],
        out_specs=pl.BlockSpec((qb_per_step * tq, D_MODEL), lambda b, i: (b * n_steps + i, 0)),
        out_shape=jax.ShapeDtypeStruct((t, D_MODEL), BF16),
        scratch_shapes=[pltpu.VMEM((past, kwidth), BF16), pltpu.VMEM((VT_ROWS, past), BF16)],
        compiler_params=_cparams(("parallel", "arbitrary")),
        name="attn_lat",
    )(sink, q, ck, cv, *([kp] * (qb_per_step + 2)), *([vt] * (qb_per_step + 2)), _window_bias_t(tq))


def _fgen_kernel(z_ref, w1_ref, b1_ref, w2_ref, b2_ref, fr_ref,
                 w3b_ref, b3b_ref, dcb_ref, w3f_ref, b3f_ref, dcf_ref, o_ref, a2_scr, *, n):
    @pl.when((pl.program_id(0) == 0) & (pl.program_id(1) == 0))
    def _():
        fr = fr_ref[...]
        a1 = jnp.sin(fr * (jnp.dot(z_ref[...], w1_ref[...], precision=HIGHEST,
                                   preferred_element_type=F32) + b1_ref[...]))
        a2_scr[...] = jnp.sin(fr * (jnp.dot(a1, w2_ref[...], precision=HIGHEST,
                                            preferred_element_type=F32) + b2_ref[...]))

    tc = o_ref.shape[1]
    tf = jnp.broadcast_to(z_ref[:, 0:1], (n, tc))
    tb = jnp.broadcast_to(z_ref[:, FILTER_WIDTH:FILTER_WIDTH + 1], (n, tc))
    a2 = a2_scr[...]
    hb = (jnp.dot(a2, w3b_ref[...], precision=HIGHEST, preferred_element_type=F32)
          + b3b_ref[...]) * jnp.exp(-tb * jnp.abs(dcb_ref[...]))
    hf = (jnp.dot(a2, w3f_ref[...], precision=HIGHEST, preferred_element_type=F32)
          + b3f_ref[...]) * jnp.exp(-tf * jnp.abs(dcf_ref[...]))
    tot = (jnp.sum(jnp.abs(hb), axis=0, keepdims=True)
           + jnp.sum(jnp.abs(hf), axis=0, keepdims=True))
    inv = 1.0 / (tot + EPS)
    rowid = lax.broadcasted_iota(jnp.int32, (n, tc), 0)
    o_ref[0:n, :] = jnp.where(rowid == 0, 0.0, hb * inv)
    o_ref[n:2 * n, :] = hf * inv


def _filter_gen(n, filt_w1, filt_b1, filt_w2, filt_b2, filt_w3, filt_b3, filt_freq, filt_decay, tc):
    t = np.arange(n, dtype=np.float64) / max(n - 1, 1)
    bands = np.arange(1, N_BANDS + 1, dtype=np.float64)
    ang = 2.0 * math.pi * t[:, None] * bands[None, :]
    z = np.concatenate([t[:, None], np.cos(ang), np.sin(ang)], axis=-1)
    zb = np.concatenate([z[0:1], z[1:][::-1]], axis=0)
    fw = FILTER_WIDTH
    zpad = np.zeros((n, fw - FILTER_EMB))
    zpacked = jnp.asarray(np.concatenate([z, zpad, zb, zpad], axis=1), dtype=F32)
    w1p = jnp.pad(filt_w1.astype(F32), ((0, fw - FILTER_EMB), (0, 0)))
    zero = jnp.zeros((fw, fw), F32)
    w1 = jnp.block([[w1p, zero], [zero, w1p]])
    w2f = filt_w2.astype(F32)
    w2 = jnp.block([[w2f, zero], [zero, w2f]])
    twice = lambda v: jnp.tile(v.astype(F32), 2).reshape(1, LANES)
    b1, b2, fr = twice(filt_b1), twice(filt_b2), twice(filt_freq)
    w3 = filt_w3.astype(F32)
    ncol = w3.shape[1]
    w3_fwd = jnp.concatenate([w3, jnp.zeros_like(w3)], axis=0)
    w3_bwd = jnp.concatenate([jnp.zeros_like(w3), w3], axis=0)
    b3 = filt_b3.astype(F32).reshape(1, ncol)
    dc = filt_decay.astype(F32).reshape(1, ncol)
    ct = D_HYENA // tc
    full = lambda shape: pl.BlockSpec(shape, lambda o, c: (0, 0))
    bwd = lambda rows: pl.BlockSpec((rows, tc), lambda o, c: (0, (2 * o + 1) * ct + c))
    fwd = lambda rows: pl.BlockSpec((rows, tc), lambda o, c: (0, (2 * o) * ct + c))
    return pl.pallas_call(
        functools.partial(_fgen_kernel, n=n),
        grid=(2, ct),
        in_specs=[
            full((n, LANES)), full((LANES, LANES)), full((1, LANES)),
            full((LANES, LANES)), full((1, LANES)), full((1, LANES)),
            bwd(LANES), bwd(1), bwd(1), fwd(LANES), fwd(1), fwd(1),
        ],
        out_specs=pl.BlockSpec((2 * n, tc), lambda o, c: (0, o * ct + c)),
        out_shape=jax.ShapeDtypeStruct((2 * n, 2 * D_HYENA), F32),
        scratch_shapes=[pltpu.VMEM((n, LANES), F32)],
        compiler_params=_cparams(("arbitrary", "arbitrary")),
        name="hyena_filter_gen",
    )(zpacked, w1, b1, w2, b2, fr, w3_bwd, b3, dc, w3_fwd, b3, dc)


def _dft_mats(blk):
    f = np.arange(blk, dtype=np.int64)[:, None]
    s = np.arange(blk, dtype=np.int64)[None, :]
    theta = (np.pi / (2 * blk)) * (((2 * f + 1) * s) % (4 * blk)).astype(np.float64)
    fwd = np.concatenate([np.cos(theta), -np.sin(theta)], axis=0)
    inv = np.concatenate([np.cos(theta).T, -np.sin(theta).T], axis=1) / blk
    return (jnp.asarray(fwd, dtype=F32).astype(BF16), jnp.asarray(inv, dtype=F32).astype(BF16))


def _ftf_kernel(k_ref, skip_ref, f_ref, o_ref, *, nblk2, blk):
    tc = o_ref.shape[2]
    fmat = f_ref[...]
    fidx = lax.broadcasted_iota(jnp.int32, (blk, tc), 0)
    sgn = jnp.where(fidx % 2 == 0, 1.0, -1.0)
    prev = None
    for e in range(nblk2):
        p = jnp.dot(fmat, k_ref[e * blk:(e + 1) * blk, :].astype(BF16), preferred_element_type=F32)
        if e >= 1:
            re = p[0:blk] - sgn * prev[blk:2 * blk]
            im = p[blk:2 * blk] + sgn * prev[0:blk]
            if e == nblk2 // 2:
                re = re + skip_ref[...]
            o_ref[e - 1, 0:blk, :] = re
            o_ref[e - 1, blk:2 * blk, :] = im
        prev = p


def _filter_transform(kfull, skip, fmat, n, blk, tc):
    nblk2 = 2 * n // blk
    nd = nblk2 - 1
    ct = D_HYENA // tc
    return pl.pallas_call(
        functools.partial(_ftf_kernel, nblk2=nblk2, blk=blk),
        grid=(2, ct),
        in_specs=[
            pl.BlockSpec((2 * n, tc), lambda o, c: (0, o * ct + c)),
            pl.BlockSpec((None, 1, tc), lambda o, c: (o, 0, c)),
            pl.BlockSpec((2 * blk, blk), lambda o, c: (0, 0)),
        ],
        out_specs=pl.BlockSpec((nd, 2 * blk, tc), lambda o, c: (o, 0, c)),
        out_shape=jax.ShapeDtypeStruct((2 * nd, 2 * blk, D_HYENA), F32),
        compiler_params=_cparams(("parallel", "parallel")),
        name="hyena_filter_transform",
    )(kfull, skip, fmat)


CHUNK = 32
HALO = 8
SC_ROWS = 128


def _short_conv_block(src_ref, b, j, n, blk, w_ref, b_ref):
    pieces = []
    for r0 in range(j * blk, (j + 1) * blk, SC_ROWS):
        lo = max(r0 - HALO, 0)
        hi = min(r0 + SC_ROWS + HALO, n)
        off = r0 - lo
        win = src_ref[b, lo:hi, :]
        prev = pltpu.roll(win, 1, axis=0)[off:off + SC_ROWS]
        nxt = pltpu.roll(win, hi - lo - 1, axis=0)[off:off + SC_ROWS]
        u = win[off:off + SC_ROWS]
        rowid = lax.broadcasted_iota(jnp.int32, u.shape, 0)
        if r0 == 0:
            prev = jnp.where(rowid == 0, 0.0, prev)
        if r0 + SC_ROWS == n:
            nxt = jnp.where(rowid == SC_ROWS - 1, 0.0, nxt)
        pieces.append(prev * w_ref[0:1, :] + u * w_ref[1:2, :] + nxt * w_ref[2:3, :] + b_ref[...])
    return jnp.concatenate(pieces, axis=0)


def _conv_kernel(zin_ref, gin_ref, wz_ref, bz_ref, wg_ref, bg_ref, g_ref, f_ref, fi_ref,
                 o_ref, zf_scr, yf_scr, *, n, blk, bb, conv_z):
    nblk = n // blk
    tc = o_ref.shape[2]
    fmat = f_ref[...]
    for j in range(nblk):
        rows = slice(j * blk, (j + 1) * blk)
        zs = []
        for b in range(bb):
            if conv_z:
                z = _short_conv_block(zin_ref, b, j, n, blk, wz_ref, bz_ref)
            else:
                z = zin_ref[b, rows, :]
            zs.append(z.astype(BF16))
        zf_scr[j] = jnp.dot(fmat, jnp.concatenate(zs, axis=1), preferred_element_type=F32)
        for b in range(bb):
            o_ref[b, rows, :] = _short_conv_block(gin_ref, b, j, n, blk, wg_ref, bg_ref)

    def pairs(i, slot):
        for r in range(blk // CHUNK):
            re_rows = slice(r * CHUNK, (r + 1) * CHUNK)
            im_rows = slice(blk + r * CHUNK, blk + (r + 1) * CHUNK)
            acc_re = [jnp.zeros((CHUNK, tc), F32) for _ in range(bb)]
            acc_im = [jnp.zeros((CHUNK, tc), F32) for _ in range(bb)]
            for j in range(nblk):
                d = i - j + (nblk - 1)
                g_re = g_ref[d, re_rows, :]
                g_im = g_ref[d, im_rows, :]
                for b in range(bb):
                    cols = slice(b * tc, (b + 1) * tc)
                    z_re = zf_scr[j, re_rows, cols]
                    z_im = zf_scr[j, im_rows, cols]
                    acc_re[b] = acc_re[b] + (g_re * z_re - g_im * z_im)
                    acc_im[b] = acc_im[b] + (g_re * z_im + g_im * z_re)
            for b in range(bb):
                cols = slice(b * tc, (b + 1) * tc)
                yf_scr[slot, re_rows, cols] = acc_re[b].astype(BF16)
                yf_scr[slot, im_rows, cols] = acc_im[b].astype(BF16)

    def inverse(slot):
        return jnp.dot(fi_ref[...], yf_scr[slot], preferred_element_type=F32)

    def gate_out(i, y):
        rows = pl.ds(pl.multiple_of(i * blk, blk), blk)
        for b in range(bb):
            o_ref[b, rows, :] = o_ref[b, rows, :] * y[:, b * tc:(b + 1) * tc]

    pairs(0, 0)
    if nblk > 1:
        def body(i, carry):
            y = inverse((i - 1) & 1)
            pairs(i, i & 1)
            gate_out(i - 1, y)
            return carry
        lax.fori_loop(1, nblk, body, 0)
    gate_out(nblk - 1, inverse((nblk - 1) & 1))


def _long_conv(zsrc, zcol, gsrc, gcol, conv_w, conv_b, gspec, order, fmat, fimat,
               n_batch, n, blk, tc, bb, conv_z):
    nblk = n // blk
    nd = 2 * nblk - 1
    ct = D_HYENA // tc
    hy0 = GH_X1 // tc
    wz = (zcol - hy0) if conv_z else 0
    wg = gcol - hy0
    return pl.pallas_call(
        functools.partial(_conv_kernel, n=n, blk=blk, bb=bb, conv_z=conv_z),
        grid=(ct, n_batch // bb),
        in_specs=[
            pl.BlockSpec((bb, n, tc), lambda c, b: (b, 0, zcol + c)),
            pl.BlockSpec((bb, n, tc), lambda c, b: (b, 0, gcol + c)),
            pl.BlockSpec((3, tc), lambda c, b: (0, wz + c)),
            pl.BlockSpec((1, tc), lambda c, b: (0, wz + c)),
            pl.BlockSpec((3, tc), lambda c, b: (0, wg + c)),
            pl.BlockSpec((1, tc), lambda c, b: (0, wg + c)),
            _resident((nd, 2 * blk, tc), lambda c, b: (order, 0, c)),
            _resident((2 * blk, blk), lambda c, b: (0, 0)),
            _resident((blk, 2 * blk), lambda c, b: (0, 0)),
        ],
        out_specs=pl.BlockSpec((bb, n, tc), lambda c, b: (b, 0, c)),
        out_shape=jax.ShapeDtypeStruct((n_batch, n, D_HYENA), F32),
        scratch_shapes=[
            pltpu.VMEM((nblk, 2 * blk, bb * tc), F32),
            pltpu.VMEM((2, 2 * blk, bb * tc), BF16),
        ],
        compiler_params=_cparams(("parallel", "arbitrary")),
        name="hyena_conv",
    )(zsrc, gsrc, conv_w, conv_b, conv_w, conv_b, gspec, fmat, fimat)


def _hyena(gh, n_batch, n, blk, tc, bb, conv_w, conv_b, skip, filt):
    fmat, fimat = _dft_mats(blk)
    kfull = _filter_gen(n, *filt, tc=128)
    sk = skip.astype(F32).reshape(2, 1, D_HYENA)
    gspec = _filter_transform(kfull, sk, fmat, n, blk, tc=256)
    cw = conv_w.astype(F32)
    cb = conv_b.astype(F32).reshape(1, -1)
    gh3 = gh.reshape(n_batch, n, GH_COLS)
    z2 = _long_conv(gh3, GH_HV // tc, gh3, GH_X1 // tc, cw, cb, gspec, 0, fmat, fimat,
                    n_batch, n, blk, tc, bb, conv_z=True)
    y = _long_conv(z2, 0, gh3, GH_X2 // tc, cw, cb, gspec, 1, fmat, fimat,
                   n_batch, n, blk, tc, bb, conv_z=False)
    return y.reshape(n_batch * n, D_HYENA)


def _outproj_kernel(x_ref, attn_ref, hy_ref, ga_ref, gh_ref, mod_ref, wa_ref, wh_ref, wo_ref, o_ref):
    n_chunks = 2
    rc = x_ref.shape[0] // n_chunks

    def branches(c):
        rows = slice(c * rc, (c + 1) * rc)
        a = jnp.dot(attn_ref[rows, :].astype(BF16), wa_ref[...], preferred_element_type=F32)
        h = jnp.dot(hy_ref[rows, :].astype(BF16), wh_ref[...], preferred_element_type=F32)
        return a, h

    pending = {0: branches(0)}
    for c in range(n_chunks):
        if c + 1 < n_chunks:
            pending[c + 1] = branches(c + 1)
        a, h = pending.pop(c)
        rows = slice(c * rc, (c + 1) * rc)
        merged = ga_ref[rows, :] * a + gh_ref[rows, :] * h
        out = jnp.dot(merged.astype(BF16), wo_ref[...], preferred_element_type=F32)
        o_ref[rows, :] = x_ref[rows, :] + mod_ref[0, 5:6, :] * out


def _outproj(x, attn, hy, gh, mod3, tiles_per_mod, wa, wh, wo, tm):
    t = x.shape[0]
    mod_map = (lambda i: (i // tiles_per_mod, 0, 0)) if tiles_per_mod else (lambda i: (0, 0, 0))
    row = lambda i: (i, 0)
    wspec = _resident((D_MODEL, D_MODEL), lambda i: (0, 0))
    return pl.pallas_call(
        _outproj_kernel,
        grid=(t // tm,),
        in_specs=[
            pl.BlockSpec((tm, D_MODEL), row),
            pl.BlockSpec((tm, D_MODEL), row),
            pl.BlockSpec((tm, D_MODEL), row),
            pl.BlockSpec((tm, D_MODEL), lambda i: (i, GH_GA // D_MODEL)),
            pl.BlockSpec((tm, D_MODEL), lambda i: (i, GH_GH // D_MODEL)),
            pl.BlockSpec((1, N_MOD, D_MODEL), mod_map),
            wspec, wspec, wspec,
        ],
        out_specs=pl.BlockSpec((tm, D_MODEL), row),
        out_shape=jax.ShapeDtypeStruct((t, D_MODEL), F32),
        compiler_params=_cparams(("parallel",)),
        name="mixer_out",
    )(x, attn, hy, gh, gh, mod3, wa, wh, wo)


def _rope_tables(n):
    pos = np.arange(n)
    row = (pos // GRID_W).astype(np.float64)
    col = (pos % GRID_W).astype(np.float64)
    n_freq = HEAD_DIM // 4
    inv = ROPE_BASE ** (-np.arange(n_freq, dtype=np.float64) / n_freq)
    ar = row[:, None] * inv[None, :]
    ac = col[:, None] * inv[None, :]
    cos = np.concatenate([np.cos(ar), np.cos(ar), np.cos(ac), np.cos(ac)], axis=-1)
    sin = np.concatenate([-np.sin(ar), np.sin(ar), -np.sin(ac), np.sin(ac)], axis=-1)
    reps = SUB // HEAD_DIM
    return (jnp.asarray(np.tile(cos, (1, reps)), dtype=F32), jnp.asarray(np.tile(sin, (1, reps)), dtype=F32))


def _layer(x, mod3, per_batch_mod, n_batch, seq, w, ctx_kv, hy_blk, hy_tc, hy_bb):
    tmf = 1024
    ffn_tiles_per_mod = (seq // tmf) if per_batch_mod else 0
    x = _ffn(x, mod3, ffn_tiles_per_mod, w["norm_ffn1"], w["ffn1_wi"], w["ffn1_wo"], 0, tmf, 256)
    qkv = None
    tmi = 512
    inproj_tiles_per_mod = (seq // tmi) if per_batch_mod else 0
    if ctx_kv is None:
        qkv, gh = _inproj(x, mod3, inproj_tiles_per_mod, w["norm_mix"], w["w_qkv"], w["w_gh"], w["qg"],
                          w["kg"], w["bd"], None, tmi, attn_layout=False)
        attn = _attention_ctx(qkv, w["sink"], n_batch, seq)
    else:
        q, kp, vt, gh = _inproj(x, mod3, inproj_tiles_per_mod, w["norm_mix"], w["w_qkv"], w["w_gh"],
                                w["qg"], w["kg"], w["bd"], _rope_tables(seq), tmi, attn_layout=True)
        attn = _attention_lat(q, kp, vt, ctx_kv[0], ctx_kv[1], w["sink"], n_batch, seq,
                              ctx_kv[0].shape[0] // n_batch)
    hy = _hyena(gh, n_batch, seq, hy_blk, hy_tc, hy_bb, w["conv_w"], w["conv_b"], w["hyena_skip"], w["filt"])
    tmo = 512
    x = _outproj(x, attn, hy, gh, mod3, (seq // tmo) if per_batch_mod else 0,
                 w["wa"], w["wh"], w["wo"], tmo)
    x = _ffn(x, mod3, ffn_tiles_per_mod, w["norm_ffn2"], w["ffn2_wi"], w["ffn2_wo"], 6, tmf, 256)
    return x, qkv


def kernel(x_prompt, x_sample, cache_k, cache_v, c, c_ctx, w_mod, b_mod, norm_ffn1, ffn1_wi, ffn1_wo, norm_mix, w_in, q_norm, k_norm, attn_sink, conv_w, conv_b, filt_w1, filt_b1, filt_w2, filt_b2, filt_w3, filt_b3, filt_freq, filt_decay, hyena_skip, w_attn_branch, w_hyena_branch, w_out, norm_ffn2, ffn2_wi, ffn2_wo):
    batch, seq, _ = x_prompt.shape
    dec_batch, dec_seq, _ = x_sample.shape
    depth = w_mod.shape[0]
    past = cache_k.shape[2]

    yp = x_prompt.reshape(batch * seq, D_MODEL)
    ys = x_sample.reshape(dec_batch * dec_seq, D_MODEL)
    bd = jnp.asarray(np.kron(np.eye(SUB // HEAD_DIM), np.ones((HEAD_DIM, HEAD_DIM))), dtype=F32).astype(BF16)
    new_ks, new_vs = [], []
    for l in range(depth):
        c_rows = jnp.zeros((16, D_MODEL), F32).at[0:dec_batch].set(c).at[dec_batch].set(c_ctx)
        mod = _modulation(c_rows, w_mod[l], b_mod[l]).reshape(16, N_MOD, D_MODEL)
        mod_lat = mod[0:dec_batch]
        mod_ctx = mod[dec_batch:dec_batch + 1]
        wl = w_in[l]
        hy_end = QKV_COLS + 3 * D_HYENA
        w = {
            "norm_ffn1": norm_ffn1[l], "ffn1_wi": ffn1_wi[l].astype(BF16), "ffn1_wo": ffn1_wo[l].astype(BF16),
            "norm_mix": norm_mix[l],
            "w_qkv": wl[:, 0:QKV_COLS].astype(BF16),
            "w_gh": jnp.concatenate([wl[:, hy_end:], wl[:, QKV_COLS:hy_end]], axis=1).astype(BF16),
            "qg": jnp.tile(q_norm[l], SUB // HEAD_DIM).reshape(1, SUB),
            "kg": jnp.tile(k_norm[l], SUB // HEAD_DIM).reshape(1, SUB),
            "bd": bd, "sink": attn_sink[l],
            "conv_w": conv_w[l], "conv_b": conv_b[l], "hyena_skip": hyena_skip[l],
            "filt": (filt_w1[l], filt_b1[l], filt_w2[l], filt_b2[l], filt_w3[l], filt_b3[l],
                     filt_freq[l], filt_decay[l]),
            "wa": w_attn_branch[l].astype(BF16), "wh": w_hyena_branch[l].astype(BF16),
            "wo": w_out[l].astype(BF16),
            "norm_ffn2": norm_ffn2[l], "ffn2_wi": ffn2_wi[l].astype(BF16), "ffn2_wo": ffn2_wo[l].astype(BF16),
        }
        yp, qkv_p = _layer(yp, mod_ctx, False, batch, seq, w, None, hy_blk=256, hy_tc=128, hy_bb=8)
        ck = cache_k[:, l].reshape(dec_batch * past, N_KV_HEADS * HEAD_DIM)
        cv = cache_v[:, l].reshape(dec_batch * past, N_KV_HEADS * HEAD_DIM)
        ys, _ = _layer(ys, mod_lat, True, dec_batch, dec_seq, w, (ck, cv), hy_blk=1024, hy_tc=128, hy_bb=2)
        new_ks.append(qkv_p[:, QKV_K:QKV_K + SUB].reshape(batch, seq, N_KV_HEADS, HEAD_DIM))
        new_vs.append(qkv_p[:, QKV_V:QKV_V + SUB].reshape(batch, seq, N_KV_HEADS, HEAD_DIM))
    new_k = jnp.stack(new_ks, axis=1)
    new_v = jnp.stack(new_vs, axis=1)
    return (yp.reshape(batch, seq, D_MODEL), ys.reshape(dec_batch, dec_seq, D_MODEL), new_k, new_v)
```
